```python
import math, functools
import jax, jax.numpy as jnp
from jax import lax
import numpy as np


D_MODEL = 1024
BATCH = 2
SEQ = 8192
DEPTH = 4
DEC_BATCH = 32
DEC_SEQ = 1
PAST_LEN = 8192
PAGE_SIZE = 128

A_HEADS = 4
A_HEAD_DIM = 64
A_WIDTH = A_HEADS * A_HEAD_DIM
A_CHUNK = 128
B_HEADS = 6
B_KV_HEADS = 2
B_GROUP = B_HEADS // B_KV_HEADS
B_HEAD_DIM = 64
B_WIDTH = B_HEADS * B_HEAD_DIM
B_KV_WIDTH = B_KV_HEADS * B_HEAD_DIM
N_BRANCH = 3
CMP_BLOCK = 32
SEL_BLOCK = 64
SEL_TOPK = 16
WINDOW = 512
Q_BLOCK = 128
FORCE_SCORE = 1e4
C_HEADS = 6
C_HEAD_DIM = 64
C_WIDTH = C_HEADS * C_HEAD_DIM
C_GROUPS = 2
C_STATE = 64
C_CONV = 4
C_CONV_DIM = C_WIDTH + 2 * C_GROUPS * C_STATE
SSD_CHUNK = 128
D_MIX = A_WIDTH + B_WIDTH + C_WIDTH
IN_SPLITS = (A_WIDTH, A_WIDTH, B_WIDTH, 6 * B_KV_WIDTH, N_BRANCH * B_HEADS, C_WIDTH, C_CONV_DIM, C_HEADS)
D_IN = 2 * A_WIDTH + B_WIDTH + 6 * B_KV_WIDTH + N_BRANCH * B_HEADS + C_WIDTH + C_CONV_DIM + C_HEADS
D_FF = 2816
REL_BUCKETS = 32
REL_MAX_EXACT = 16
REL_MAX_DIST = 128
EPS = 1e-6
NEG_INF = -1e30

kernel_name = 'hymba_gmlp_nsa_ssd_macaron_step'


def rmsnorm(x, g):
    x32 = x.astype(jnp.float32)
    y = x32 * lax.rsqrt(jnp.mean(x32 * x32, axis=-1, keepdims=True) + EPS)
    return (y * g.astype(jnp.float32)).astype(x.dtype)


def layernorm(x, g):
    x32 = x.astype(jnp.float32)
    mu = jnp.mean(x32, axis=-1, keepdims=True)
    var = jnp.mean(jnp.square(x32 - mu), axis=-1, keepdims=True)
    return ((x32 - mu) * lax.rsqrt(var + EPS) * g.astype(jnp.float32)).astype(x.dtype)


def half_ffn(x, g_pre, g_post, wg, wu, wd):
    h = rmsnorm(x, g_pre)
    y = (jax.nn.silu(h @ wg) * (h @ wu)) @ wd
    return x + 0.5 * rmsnorm(y, g_post)


def masked_softmax(logits, mask):
    l = jnp.where(mask, logits.astype(jnp.float32), NEG_INF)
    m = jnp.max(l, axis=-1, keepdims=True)
    e = jnp.where(mask, jnp.exp(l - m), 0.0)
    return e / jnp.maximum(jnp.sum(e, axis=-1, keepdims=True), 1e-20)


def t5_bucket(dist):
    n = jnp.maximum(dist, 0)
    nf = jnp.maximum(n, 1).astype(jnp.float32)
    large = REL_MAX_EXACT + (jnp.log(nf / REL_MAX_EXACT) / math.log(REL_MAX_DIST / REL_MAX_EXACT)
                             * (REL_BUCKETS - REL_MAX_EXACT)).astype(jnp.int32)
    large = jnp.minimum(large, REL_BUCKETS - 1)
    return jnp.where(n < REL_MAX_EXACT, n, large)


def pair_bias(rel_bias, dist):
    b = rel_bias[t5_bucket(dist)]
    return b.reshape(dist.shape + (B_KV_HEADS, B_GROUP)).astype(jnp.float32).transpose(0, 2, 3, 1)


def compress(rows, pe, w):
    b, tk = rows.shape[:2]
    blk = rows.reshape(b, tk // CMP_BLOCK, CMP_BLOCK, B_KV_HEADS, B_HEAD_DIM)
    return jnp.mean(blk + pe[None, None, :, None, :], axis=2) @ w


def sel_blocks(rows):
    b, tk = rows.shape[:2]
    return rows.reshape(b, tk // SEL_BLOCK, SEL_BLOCK, B_KV_HEADS, B_HEAD_DIM).transpose(0, 3, 1, 2, 4)


def nsa_attend(q, gates, qpos, kc, vc, ks, vs, kw, vw, kwpos, rel_bias):
    nb, tq = q.shape[:2]
    n_cmp = kc.shape[1]
    n_sel = ks.shape[2]
    cmp_end = jnp.arange(n_cmp, dtype=jnp.int32) * CMP_BLOCK + (CMP_BLOCK - 1)
    dist_c = qpos[:, None] - cmp_end[None, :]
    logit_c = jnp.einsum('bqhgd,bchd->bqhgc', q, kc).astype(jnp.float32) + pair_bias(rel_bias, dist_c)
    p_c = masked_softmax(logit_c, (dist_c >= 0)[:, None, None, :])
    o_c = jnp.einsum('bqhgc,bchd->bqhgd', p_c.astype(vc.dtype), vc)
    imp = p_c.sum(axis=3).reshape(nb, tq, B_KV_HEADS, n_sel, SEL_BLOCK // CMP_BLOCK).sum(axis=-1)
    blk = jnp.arange(n_sel, dtype=jnp.int32)
    cur = qpos // SEL_BLOCK
    forced = (blk[None, :] == cur[:, None]) | (blk[None, :] == 0)
    started = blk[None, :] * SEL_BLOCK <= qpos[:, None]
    imp = jnp.where(forced[:, None, :], FORCE_SCORE, jnp.where(started[:, None, :], imp, -1.0))
    _, idx = lax.top_k(imp, min(SEL_TOPK, n_sel))
    bi = jnp.arange(nb)[:, None, None, None]
    hi = jnp.arange(B_KV_HEADS)[None, None, :, None]
    k_sel = ks[bi, hi, idx]
    v_sel = vs[bi, hi, idx]
    kpos = idx[..., None] * SEL_BLOCK + jnp.arange(SEL_BLOCK, dtype=jnp.int32)
    dist_s = qpos[None, :, None, None, None] - kpos
    rb = rel_bias.reshape(REL_BUCKETS, B_KV_HEADS, B_GROUP).transpose(1, 0, 2)
    bias_s = jnp.moveaxis(rb[hi[..., None], t5_bucket(dist_s)], -1, 3).astype(jnp.float32)
    logit_s = jnp.einsum('bqhgd,bqhksd->bqhgks', q, k_sel).astype(jnp.float32) + bias_s
    shp = logit_s.shape
    p_s = masked_softmax(logit_s.reshape(shp[:4] + (-1,)),
                         (dist_s >= 0).reshape(nb, tq, B_KV_HEADS, 1, -1)).reshape(shp)
    o_s = jnp.einsum('bqhgks,bqhksd->bqhgd', p_s.astype(vs.dtype), v_sel)
    dist_w = qpos[:, None] - kwpos[None, :]
    mask_w = (dist_w >= 0) & (dist_w < WINDOW) & (kwpos[None, :] >= 0)
    logit_w = jnp.einsum('bqhgd,bshd->bqhgs', q, kw).astype(jnp.float32) + pair_bias(rel_bias, dist_w)
    p_w = masked_softmax(logit_w, mask_w[:, None, None, :])
    o_w = jnp.einsum('bqhgs,bshd->bqhgd', p_w.astype(vw.dtype), vw)
    o = gates[..., 0:1] * o_c + gates[..., 1:2] * o_s + gates[..., 2:3] * o_w
    return o.reshape(nb, tq, B_WIDTH)


def nsa_prompt(q, gates, kv6, pe_cmp, w_cmp, rel_bias):
    b, t = q.shape[:2]
    kc = compress(kv6[:, :, 0], pe_cmp[0], w_cmp[0])
    vc = compress(kv6[:, :, 1], pe_cmp[1], w_cmp[1])
    ks = sel_blocks(kv6[:, :, 2])
    vs = sel_blocks(kv6[:, :, 3])
    pad = ((0, 0), (WINDOW, 0), (0, 0), (0, 0))
    kw_pad = jnp.pad(kv6[:, :, 4], pad)
    vw_pad = jnp.pad(kv6[:, :, 5], pad)

    def block(i):
        s = i * Q_BLOCK
        qb = lax.dynamic_slice_in_dim(q, s, Q_BLOCK, axis=1)
        gb = lax.dynamic_slice_in_dim(gates, s, Q_BLOCK, axis=1)
        kwb = lax.dynamic_slice_in_dim(kw_pad, s, WINDOW + Q_BLOCK, axis=1)
        vwb = lax.dynamic_slice_in_dim(vw_pad, s, WINDOW + Q_BLOCK, axis=1)
        qpos = s + jnp.arange(Q_BLOCK, dtype=jnp.int32)
        kwpos = s - WINDOW + jnp.arange(WINDOW + Q_BLOCK, dtype=jnp.int32)
        return nsa_attend(qb, gb, qpos, kc, vc, ks, vs, kwb, vwb, kwpos, rel_bias)

    out = lax.map(block, jnp.arange(t // Q_BLOCK, dtype=jnp.int32))
    out = out.transpose(1, 0, 2, 3).reshape(b, t, B_WIDTH)
    return out, kv6[:, t - min(WINDOW, t):, 4:6]


def nsa_sample(q, gates, kv6, past_cmp, past_slc, win_buf, pe_cmp, w_cmp, rel_bias):
    s = q.shape[1]
    past = past_cmp.shape[1]
    n_pad = (-(past + s)) % SEL_BLOCK
    pad = ((0, 0), (0, n_pad), (0, 0), (0, 0), (0, 0))
    full_c = jnp.pad(jnp.concatenate([past_cmp, kv6[:, :, 0:2]], axis=1), pad)
    full_s = jnp.pad(jnp.concatenate([past_slc, kv6[:, :, 2:4]], axis=1), pad)
    kc = compress(full_c[:, :, 0], pe_cmp[0], w_cmp[0])
    vc = compress(full_c[:, :, 1], pe_cmp[1], w_cmp[1])
    ks = sel_blocks(full_s[:, :, 0])
    vs = sel_blocks(full_s[:, :, 1])
    wb = win_buf.shape[1]
    win = jnp.concatenate([win_buf.astype(kv6.dtype), kv6[:, :, 4:6]], axis=1)
    kwpos = past - wb + jnp.arange(wb + s, dtype=jnp.int32)
    qpos = past + jnp.arange(s, dtype=jnp.int32)
    out = nsa_attend(q, gates, qpos, kc, vc, ks, vs, win[:, :, 0], win[:, :, 1], kwpos, rel_bias)
    return out, win[:, s:]


def gmlp_mix(u, v, norm_g, w_s, b_s):
    b, t = u.shape[:2]
    u = jax.nn.gelu(u)
    v = layernorm(jax.nn.gelu(v), norm_g)
    L = min(t, A_CHUNK)
    n = t // L
    ws = w_s[:, :L, :L] * jnp.tril(jnp.ones((L, L), w_s.dtype))
    vh = v.reshape(b, n, L, A_HEADS, A_HEAD_DIM)
    sgate = jnp.einsum('hij,bnjhd->bnihd', ws, vh) + b_s[:, :L].T[None, None, :, :, None]
    return u * sgate.reshape(b, t, A_WIDTH), v


def ssd_scan(x, dt, a, bm, cm, h0, chunk):
    b, t = x.shape[:2]
    n = t // chunk
    rep = C_HEADS // C_GROUPS
    bh = jnp.repeat(bm.astype(jnp.float32), rep, axis=2)
    ch = jnp.repeat(cm.astype(jnp.float32), rep, axis=2)
    a32 = a.astype(jnp.float32)
    causal = jnp.tril(jnp.ones((chunk, chunk), bool))[None, :, :, None]

    def to_chunks(z):
        return jnp.moveaxis(z.reshape((b, n, chunk) + z.shape[2:]), 1, 0)

    def step(h, inp):
        xc, dtc, bc, cc = inp
        acum = jnp.cumsum(dtc * a32, axis=1)
        seg = acum[:, :, None, :] - acum[:, None, :, :]
        decay = jnp.where(causal, jnp.exp(jnp.where(causal, seg, 0.0)), 0.0)
        scores = jnp.einsum('bihn,bjhn->bijh', cc, bc) * decay * dtc[:, None, :, :]
        y = (jnp.einsum('bijh,bjhd->bihd', scores, xc)
             + jnp.einsum('bihn,bhdn->bihd', cc, h) * jnp.exp(acum)[..., None])
        w_end = jnp.exp(acum[:, -1:, :] - acum) * dtc
        h = h * jnp.exp(acum[:, -1])[:, :, None, None] + jnp.einsum('bjh,bjhd,bjhn->bhdn', w_end, xc, bc)
        return h, y

    h, ys = lax.scan(step, h0.astype(jnp.float32),
                     (to_chunks(x.astype(jnp.float32)), to_chunks(dt), to_chunks(bh), to_chunks(ch)))
    y = jnp.moveaxis(ys, 0, 1).reshape(b, t, C_HEADS, C_HEAD_DIM)
    return y.astype(x.dtype), h


def mamba_mix(z, xbc, dt, conv_prev, h0, conv_w, conv_b, dt_bias, a_log, d_skip, norm_g):
    b, t = z.shape[:2]
    full = jnp.concatenate([conv_prev.astype(xbc.dtype), xbc], axis=1)
    conv = conv_b + sum(full[:, k:k + t] * conv_w[k] for k in range(C_CONV))
    xbc_c = jax.nn.silu(conv)
    xs, bm, cm = jnp.split(xbc_c, [C_WIDTH, C_WIDTH + C_GROUPS * C_STATE], axis=-1)
    xs = xs.reshape(b, t, C_HEADS, C_HEAD_DIM)
    bm = bm.reshape(b, t, C_GROUPS, C_STATE)
    cm = cm.reshape(b, t, C_GROUPS, C_STATE)
    dt = jax.nn.softplus((dt + dt_bias).astype(jnp.float32))
    a = -jnp.exp(a_log.astype(jnp.float32))
    y, h = ssd_scan(xs, dt, a, bm, cm, h0, min(t, SSD_CHUNK))
    y = y + d_skip[:, None] * xs
    y = rmsnorm(y.reshape(b, t, C_WIDTH) * jax.nn.silu(z), norm_g)
    return y, full[:, t:], h


def parallel_mixer(x, g_pre, g_post, w_in, w_out, gmlp_norm_g, gmlp_w_s, gmlp_b_s, conv_w, conv_b,
                   dt_bias, a_log, d_skip, ssm_norm_g, conv_prev, h0, nsa_fn):
    b, t = x.shape[:2]
    h = rmsnorm(x, g_pre)
    offsets = np.cumsum(IN_SPLITS)[:-1].tolist()
    u_a, v_a, q_b, kv_b, gate_b, z_c, xbc_c, dt_c = jnp.split(h @ w_in, offsets, axis=-1)
    o_a, v_rows = gmlp_mix(u_a, v_a, gmlp_norm_g, gmlp_w_s, gmlp_b_s)
    q = q_b.reshape(b, t, B_KV_HEADS, B_GROUP, B_HEAD_DIM) * (B_HEAD_DIM ** -0.5)
    kv6 = kv_b.reshape(b, t, 6, B_KV_HEADS, B_HEAD_DIM)
    gates = jax.nn.sigmoid(gate_b.astype(jnp.float32)).reshape(b, t, B_KV_HEADS, B_GROUP, N_BRANCH).astype(x.dtype)
    o_b, win_rows = nsa_fn(q, gates, kv6)
    o_c, conv_rows, h_ssm = mamba_mix(z_c, xbc_c, dt_c, conv_prev, h0, conv_w, conv_b, dt_bias, a_log,
                                      d_skip, ssm_norm_g)
    y = jnp.concatenate([o_a, o_b, o_c], axis=-1) @ w_out
    return x + rmsnorm(y, g_post), v_rows, kv6, win_rows, conv_rows, h_ssm


def setup_inputs(seed: int = 0) -> dict:
    key = jax.random.key(seed)
    ks = jax.random.split(key, 32)

    def nrm(k, shape, scale):
        return jax.random.normal(k, shape, jnp.float32) * scale

    n_pages = PAST_LEN // PAGE_SIZE
    n_used = DEC_BATCH * n_pages
    n_phys = n_used + n_used // 4
    win_buf = min(WINDOW, PAST_LEN)
    kv_page_shape = (DEPTH, n_phys, PAGE_SIZE, 2, B_KV_HEADS, B_HEAD_DIM)
    page_table = jax.random.permutation(ks[4], n_phys)[:n_used].reshape(DEC_BATCH, n_pages).astype(jnp.int32)
    dt0 = jnp.exp(jax.random.uniform(ks[21], (DEPTH, C_HEADS)) * (math.log(0.1) - math.log(0.001)) + math.log(0.001))
    return {
        'x_prompt': nrm(ks[0], (BATCH, SEQ, D_MODEL), 1.0),
        'x_sample': nrm(ks[1], (DEC_BATCH, DEC_SEQ, D_MODEL), 1.0),
        'cache_cmp_kv': nrm(ks[2], kv_page_shape, 1.0),
        'cache_slc_kv': nrm(ks[3], kv_page_shape, 1.0),
        'page_table': page_table,
        'state_win_kv': nrm(ks[5], (DEPTH, DEC_BATCH, win_buf, 2, B_KV_HEADS, B_HEAD_DIM), 1.0),
        'state_conv': nrm(ks[6], (DEPTH, DEC_BATCH, C_CONV - 1, C_CONV_DIM), 1.0),
        'state_ssm': nrm(ks[7], (DEPTH, DEC_BATCH, C_HEADS, C_HEAD_DIM, C_STATE), 0.1),
        'norm_g': 1.0 + nrm(ks[8], (DEPTH, 6, D_MODEL), 0.05),
        'ffn_w_gate': nrm(ks[9], (DEPTH, 2, D_MODEL, D_FF), D_MODEL ** -0.5),
        'ffn_w_up': nrm(ks[10], (DEPTH, 2, D_MODEL, D_FF), D_MODEL ** -0.5),
        'ffn_w_down': nrm(ks[11], (DEPTH, 2, D_FF, D_MODEL), D_FF ** -0.5),
        'w_in': nrm(ks[12], (DEPTH, D_MODEL, D_IN), D_MODEL ** -0.5),
        'w_out': nrm(ks[13], (DEPTH, D_MIX, D_MODEL), D_MIX ** -0.5),
        'gmlp_norm_g': 1.0 + nrm(ks[14], (DEPTH, A_WIDTH), 0.05),
        'gmlp_w_s': nrm(ks[15], (DEPTH, A_HEADS, A_CHUNK, A_CHUNK), A_CHUNK ** -0.5),
        'gmlp_b_s': 1.0 + nrm(ks[16], (DEPTH, A_HEADS, A_CHUNK), 0.1),
        'nsa_pe_cmp': nrm(ks[17], (DEPTH, 2, CMP_BLOCK, B_HEAD_DIM), 0.1),
        'nsa_w_cmp': nrm(ks[18], (DEPTH, 2, B_HEAD_DIM, B_HEAD_DIM), B_HEAD_DIM ** -0.5),
        'rel_bias': nrm(ks[19], (REL_BUCKETS, B_HEADS), 0.2),
        'conv_w': nrm(ks[20], (DEPTH, C_CONV, C_CONV_DIM), 0.5),
        'conv_b': nrm(ks[22], (DEPTH, C_CONV_DIM), 0.02),
        'dt_bias': dt0 + jnp.log(-jnp.expm1(-dt0)),
        'a_log': jnp.log(jax.random.uniform(ks[23], (DEPTH, C_HEADS), jnp.float32, 1.0, 16.0)),
        'd_skip': 1.0 + nrm(ks[24], (DEPTH, C_HEADS), 0.1),
        'ssm_norm_g': 1.0 + nrm(ks[25], (DEPTH, C_WIDTH), 0.05),
    }


def reference(x_prompt, x_sample, cache_cmp_kv, cache_slc_kv, page_table, state_win_kv, state_conv, state_ssm,
              norm_g, ffn_w_gate, ffn_w_up, ffn_w_down, w_in, w_out, gmlp_norm_g, gmlp_w_s, gmlp_b_s,
              nsa_pe_cmp, nsa_w_cmp, rel_bias, conv_w, conv_b, dt_bias, a_log, d_skip, ssm_norm_g):
    xp, xs = x_prompt, x_sample
    bp = xp.shape[0]
    bs = xs.shape[0]
    n_pages = page_table.shape[1]
    past = n_pages * PAGE_SIZE
    cmp_p, slc_p, win_p, conv_p, ssm_p = [], [], [], [], []
    cmp_s, slc_s, win_s, conv_s, ssm_s, v_s = [], [], [], [], [], []
    for l in range(DEPTH):
        g = norm_g[l]
        xp = half_ffn(xp, g[0], g[1], ffn_w_gate[l, 0], ffn_w_up[l, 0], ffn_w_down[l, 0])
        xs = half_ffn(xs, g[0], g[1], ffn_w_gate[l, 0], ffn_w_up[l, 0], ffn_w_down[l, 0])
        shared = (w_in[l], w_out[l], gmlp_norm_g[l], gmlp_w_s[l], gmlp_b_s[l], conv_w[l], conv_b[l],
                  dt_bias[l], a_log[l], d_skip[l], ssm_norm_g[l])
        prompt_nsa = functools.partial(nsa_prompt, pe_cmp=nsa_pe_cmp[l], w_cmp=nsa_w_cmp[l], rel_bias=rel_bias)
        xp, _, kv6_p, win_rows_p, conv_rows_p, h_p = parallel_mixer(
            xp, g[2], g[3], *shared,
            conv_prev=jnp.zeros((bp, C_CONV - 1, C_CONV_DIM), xp.dtype),
            h0=jnp.zeros((bp, C_HEADS, C_HEAD_DIM, C_STATE), jnp.float32),
            nsa_fn=prompt_nsa)
        past_cmp = cache_cmp_kv[l][page_table].reshape(bs, past, 2, B_KV_HEADS, B_HEAD_DIM)
        past_slc = cache_slc_kv[l][page_table].reshape(bs, past, 2, B_KV_HEADS, B_HEAD_DIM)
        sample_nsa = functools.partial(nsa_sample, past_cmp=past_cmp, past_slc=past_slc, win_buf=state_win_kv[l],
                                       pe_cmp=nsa_pe_cmp[l], w_cmp=nsa_w_cmp[l], rel_bias=rel_bias)
        xs, v_rows_s, kv6_s, win_rows_s, conv_rows_s, h_s = parallel_mixer(
            xs, g[2], g[3], *shared, conv_prev=state_conv[l], h0=state_ssm[l], nsa_fn=sample_nsa)
        xp = half_ffn(xp, g[4], g[5], ffn_w_gate[l, 1], ffn_w_up[l, 1], ffn_w_down[l, 1])
        xs = half_ffn(xs, g[4], g[5], ffn_w_gate[l, 1], ffn_w_up[l, 1], ffn_w_down[l, 1])
        cmp_p.append(kv6_p[:, :, 0:2])
        slc_p.append(kv6_p[:, :, 2:4])
        win_p.append(win_rows_p)
        conv_p.append(conv_rows_p)
        ssm_p.append(h_p)
        cmp_s.append(kv6_s[:, :, 0:2])
        slc_s.append(kv6_s[:, :, 2:4])
        win_s.append(win_rows_s)
        conv_s.append(conv_rows_s)
        ssm_s.append(h_s)
        v_s.append(v_rows_s)
    return (xp, xs, jnp.stack(cmp_p), jnp.stack(slc_p), jnp.stack(win_p), jnp.stack(conv_p), jnp.stack(ssm_p),
            jnp.stack(cmp_s), jnp.stack(slc_s), jnp.stack(win_s), jnp.stack(conv_s), jnp.stack(ssm_s), jnp.stack(v_s))
```

```python
import functools
import math

import numpy as np
import jax
import jax.numpy as jnp
from jax import lax
from jax.experimental import pallas as pl
from jax.experimental.pallas import tpu as pltpu

D_MODEL = 1024
DEPTH = 4
PAGE_SIZE = 128
A_HEADS, A_HEAD_DIM, A_CHUNK = 4, 64, 128
A_WIDTH = A_HEADS * A_HEAD_DIM
B_HEADS, B_KV_HEADS, B_HEAD_DIM = 6, 2, 64
B_GROUP = B_HEADS // B_KV_HEADS
B_WIDTH = B_HEADS * B_HEAD_DIM
B_KV_WIDTH = B_KV_HEADS * B_HEAD_DIM
N_BRANCH = 3
CMP_BLOCK, SEL_BLOCK, SEL_TOPK, WINDOW, Q_BLOCK = 32, 64, 16, 512, 128
FORCE_SCORE = 1e4
C_HEADS, C_HEAD_DIM, C_GROUPS, C_STATE, C_CONV = 6, 64, 2, 64, 4
C_WIDTH = C_HEADS * C_HEAD_DIM
C_CONV_DIM = C_WIDTH + 2 * C_GROUPS * C_STATE
SSD_CHUNK = 128
D_FF = 2816
REL_BUCKETS, REL_MAX_EXACT, REL_MAX_DIST = 32, 16, 128
EPS = 1e-6
NEG_INF = -1e30
SOFTMAX_FLOOR = -1e20

LANES = 128
SUBLANES = 8
VMEM_LIMIT_BYTES = 56 * 1024 * 1024

_MXU = jnp.bfloat16
_F32 = jnp.float32

_GATE_COLS = N_BRANCH * B_HEADS
_SEG = (("uv", 2 * A_WIDTH), ("q", B_WIDTH), ("kvc", 2 * B_KV_WIDTH), ("kvs", 2 * B_KV_WIDTH),
        ("kvw", 2 * B_KV_WIDTH), ("z", C_WIDTH), ("xbc", C_CONV_DIM), ("gate", LANES), ("dt", LANES))
_D_IN_PAD = sum(w for _, w in _SEG)


def _dot(a, b):
    return jnp.dot(a.astype(_MXU), b.astype(_MXU), preferred_element_type=_F32)


def _dot_nt(a, b):
    return lax.dot_general(a.astype(_MXU), b.astype(_MXU), (((1,), (1,)), ((), ())),
                           preferred_element_type=_F32)


def _dot_tn(a, b):
    return lax.dot_general(a.astype(_MXU), b.astype(_MXU), (((0,), (0,)), ((), ())),
                           preferred_element_type=_F32)


def _dot_exact(a, b):
    return jnp.dot(a, b, preferred_element_type=_F32, precision=lax.Precision.HIGHEST)


def _rms(x, g):
    return x * lax.rsqrt(jnp.mean(x * x, axis=-1, keepdims=True) + EPS) * g


def _silu(x):
    return x * jax.nn.sigmoid(x)


def _cparams(sem):
    return pltpu.CompilerParams(dimension_semantics=sem, vmem_limit_bytes=VMEM_LIMIT_BYTES)


def _bucket_np(dist):
    n = np.maximum(dist, 0)
    nf = np.maximum(n, 1).astype(np.float32)
    large = REL_MAX_EXACT + (np.log(nf / np.float32(REL_MAX_EXACT))
                             / np.float32(math.log(REL_MAX_DIST / REL_MAX_EXACT))
                             * np.float32(REL_BUCKETS - REL_MAX_EXACT)).astype(np.int32)
    large = np.minimum(large, REL_BUCKETS - 1)
    return np.where(n < REL_MAX_EXACT, n, large).astype(np.int32)


def _ffn_kernel(x_ref, g_ref, wg_ref, wu_ref, wd_ref, o_ref, h_ref, acc_ref):
    f = pl.program_id(1)

    @pl.when(f == 0)
    def _():
        h_ref[...] = _rms(x_ref[...], g_ref[0:1, :]).astype(h_ref.dtype)
        acc_ref[...] = jnp.zeros_like(acc_ref)

    h = h_ref[...]
    a = _silu(_dot(h, wg_ref[...])) * _dot(h, wu_ref[...])
    acc_ref[...] += _dot(a, wd_ref[...])

    @pl.when(f == pl.num_programs(1) - 1)
    def _():
        o_ref[...] = x_ref[...] + 0.5 * _rms(acc_ref[...], g_ref[1:2, :])


def _ffn(x, g2, wg, wu, wd, l, j, tm, tf=256):
    rows = x.shape[0]
    return pl.pallas_call(
        _ffn_kernel,
        grid=(rows // tm, D_FF // tf),
        in_specs=[pl.BlockSpec((tm, D_MODEL), lambda r, f: (r, 0)),
                  pl.BlockSpec((2, D_MODEL), lambda r, f: (0, 0)),
                  pl.BlockSpec((None, None, D_MODEL, tf), lambda r, f: (l, j, 0, f)),
                  pl.BlockSpec((None, None, D_MODEL, tf), lambda r, f: (l, j, 0, f)),
                  pl.BlockSpec((None, None, tf, D_MODEL), lambda r, f: (l, j, f, 0))],
        out_specs=pl.BlockSpec((tm, D_MODEL), lambda r, f: (r, 0)),
        out_shape=jax.ShapeDtypeStruct((rows, D_MODEL), _F32),
        scratch_shapes=[pltpu.VMEM((tm, D_MODEL), _MXU), pltpu.VMEM((tm, D_MODEL), _F32)],
        compiler_params=_cparams(("parallel", "arbitrary")),
        name="half_ffn",
    )(x, g2, wg, wu, wd)


def _inproj_kernel(x_ref, g_ref, w_ref, *o_refs):
    h = _rms(x_ref[...], g_ref[...]).astype(_MXU)
    off = 0
    for (_, width), o_ref in zip(_SEG, o_refs):
        o_ref[...] = _dot(h, w_ref[:, off:off + width])
        off += width


def _inproj(x, g_row, w_in_p, l, tm):
    rows = x.shape[0]
    return pl.pallas_call(
        _inproj_kernel,
        grid=(rows // tm,),
        in_specs=[pl.BlockSpec((tm, D_MODEL), lambda r: (r, 0)),
                  pl.BlockSpec((1, D_MODEL), lambda r: (0, 0)),
                  pl.BlockSpec((None, D_MODEL, _D_IN_PAD), lambda r: (l, 0, 0))],
        out_specs=[pl.BlockSpec((tm, w), lambda r: (r, 0)) for _, w in _SEG],
        out_shape=[jax.ShapeDtypeStruct((rows, w), _F32) for _, w in _SEG],
        compiler_params=_cparams(("parallel",)),
        name="in_proj",
    )(x, g_row, w_in_p)


def _outproj_kernel(x_ref, oa_ref, ob_ref, oc_ref, g_ref, w_ref, o_ref):
    y = (_dot(oa_ref[...], w_ref[0:A_WIDTH, :])
         + _dot(ob_ref[...], w_ref[A_WIDTH:A_WIDTH + B_WIDTH, :])
         + _dot(oc_ref[...], w_ref[A_WIDTH + B_WIDTH:, :]))
    o_ref[...] = x_ref[...] + _rms(y, g_ref[...])


def _outproj(x, oa, ob, oc, g_row, w_out, l, tm):
    rows = x.shape[0]
    return pl.pallas_call(
        _outproj_kernel,
        grid=(rows // tm,),
        in_specs=[pl.BlockSpec((tm, D_MODEL), lambda r: (r, 0)),
                  pl.BlockSpec((tm, A_WIDTH), lambda r: (r, 0)),
                  pl.BlockSpec((tm, B_WIDTH), lambda r: (r, 0)),
                  pl.BlockSpec((tm, C_WIDTH), lambda r: (r, 0)),
                  pl.BlockSpec((1, D_MODEL), lambda r: (0, 0)),
                  pl.BlockSpec((None, D_MODEL, D_MODEL), lambda r: (l, 0, 0))],
        out_specs=pl.BlockSpec((tm, D_MODEL), lambda r: (r, 0)),
        out_shape=jax.ShapeDtypeStruct((rows, D_MODEL), _F32),
        compiler_params=_cparams(("parallel",)),
        name="out_proj",
    )(x, oa, ob, oc, g_row, w_out)


def _gelu_ln(uv, ng):
    u = jax.nn.gelu(uv[:, :A_WIDTH])
    v = jax.nn.gelu(uv[:, A_WIDTH:])
    mu = jnp.mean(v, axis=-1, keepdims=True)
    var = jnp.mean(jnp.square(v - mu), axis=-1, keepdims=True)
    return u, (v - mu) * lax.rsqrt(var + EPS) * ng


def _gmlp_kernel(uv_ref, ng_ref, ws_ref, bs_ref, o_ref, *, chunks):
    row = lax.broadcasted_iota(jnp.int32, (A_CHUNK, A_CHUNK), 0)
    col = lax.broadcasted_iota(jnp.int32, (A_CHUNK, A_CHUNK), 1)
    ws = [jnp.where(col <= row, ws_ref[h], 0.0).astype(_MXU) for h in range(A_HEADS)]
    for c in range(chunks):
        u, v = _gelu_ln(uv_ref[c * A_CHUNK:(c + 1) * A_CHUNK, :], ng_ref[...])
        sg = [_dot(ws[h], v[:, h * A_HEAD_DIM:(h + 1) * A_HEAD_DIM]) + bs_ref[:, h:h + 1]
              for h in range(A_HEADS)]
        o_ref[c * A_CHUNK:(c + 1) * A_CHUNK, :] = u * jnp.concatenate(sg, axis=-1)


def _gmlp_prompt(uv, ng_row, ws, bs_t, l, chunks=4):
    rows = uv.shape[0]
    tm = chunks * A_CHUNK
    return pl.pallas_call(
        functools.partial(_gmlp_kernel, chunks=chunks),
        grid=(rows // tm,),
        in_specs=[pl.BlockSpec((tm, 2 * A_WIDTH), lambda r: (r, 0)),
                  pl.BlockSpec((1, A_WIDTH), lambda r: (0, 0)),
                  pl.BlockSpec((None, A_HEADS, A_CHUNK, A_CHUNK), lambda r: (l, 0, 0, 0)),
                  pl.BlockSpec((None, A_CHUNK, A_HEADS), lambda r: (l, 0, 0))],
        out_specs=pl.BlockSpec((tm, A_WIDTH), lambda r: (r, 0)),
        out_shape=jax.ShapeDtypeStruct((rows, A_WIDTH), _F32),
        compiler_params=_cparams(("parallel",)),
        name="gmlp_prompt",
    )(uv, ng_row, ws, bs_t)


def _mamba_kernel(z_ref, xbc_ref, dt_ref, cw_ref, cb_ref, hp_ref, dsk_ref, ng_ref,
                  o_ref, hout_ref, xp_ref, hs_ref):
    t = pl.program_id(1)
    L = SSD_CHUNK
    hist = SUBLANES

    @pl.when(t == 0)
    def _():
        xp_ref[0:hist, :] = jnp.zeros((hist, C_CONV_DIM), _F32)
        hs_ref[...] = jnp.zeros_like(hs_ref)

    xp_ref[hist:hist + L, :] = xbc_ref[0]
    conv = cb_ref[...]
    for k in range(C_CONV):
        conv = conv + xp_ref[pl.ds(hist - (C_CONV - 1) + k, L), :] * cw_ref[k:k + 1, :]
    xp_ref[0:hist, :] = xp_ref[L:L + hist, :]
    xc = _silu(conv)
    xs = xc[:, :C_WIDTH]
    bm = xc[:, C_WIDTH:C_WIDTH + C_GROUPS * C_STATE]
    cm = xc[:, C_WIDTH + C_GROUPS * C_STATE:]

    dt = jax.nn.softplus(dt_ref[0] + hp_ref[0:1, :])
    a_row = -jnp.exp(hp_ref[1:2, :])
    row = lax.broadcasted_iota(jnp.int32, (L, L), 0)
    col = lax.broadcasted_iota(jnp.int32, (L, L), 1)
    causal = col <= row
    acum = _dot_exact(jnp.where(causal, 1.0, 0.0), dt * a_row)
    acum_t = acum.T
    dt_t = dt.T
    cb = [_dot_nt(cm[:, g * C_STATE:(g + 1) * C_STATE], bm[:, g * C_STATE:(g + 1) * C_STATE])
          for g in range(C_GROUPS)]
    ys = []
    for h in range(C_HEADS):
        g = h // (C_HEADS // C_GROUPS)
        ac_col = acum[:, h:h + 1]
        seg = ac_col - acum_t[h:h + 1, :]
        decay = jnp.where(causal, jnp.exp(jnp.where(causal, seg, 0.0)), 0.0)
        scores = cb[g] * decay * dt_t[h:h + 1, :]
        x_h = xs[:, h * C_HEAD_DIM:(h + 1) * C_HEAD_DIM]
        b_g = bm[:, g * C_STATE:(g + 1) * C_STATE]
        c_g = cm[:, g * C_STATE:(g + 1) * C_STATE]
        hs = hs_ref[h]
        ys.append(_dot(scores, x_h) + _dot_nt(c_g, hs) * jnp.exp(ac_col))
        ac_last = acum[L - 1:L, h:h + 1]
        w_end = jnp.exp(ac_last - ac_col) * dt[:, h:h + 1]
        hs_ref[h] = hs * jnp.exp(ac_last) + _dot_tn(x_h * w_end, b_g)
    y = jnp.concatenate(ys, axis=-1) + dsk_ref[...] * xs
    o_ref[0] = _rms(y * _silu(z_ref[0]), ng_ref[...])

    @pl.when(t == pl.num_programs(1) - 1)
    def _():
        hout_ref[0] = hs_ref[...]


def _mamba_prompt(z, xbc, dt, conv_w, conv_b_row, hp, dsk_row, ng_row, l):
    nb, t = z.shape[:2]
    L = SSD_CHUNK
    return pl.pallas_call(
        _mamba_kernel,
        grid=(nb, t // L),
        in_specs=[pl.BlockSpec((1, L, C_WIDTH), lambda b, c: (b, c, 0)),
                  pl.BlockSpec((1, L, C_CONV_DIM), lambda b, c: (b, c, 0)),
                  pl.BlockSpec((1, L, LANES), lambda b, c: (b, c, 0)),
                  pl.BlockSpec((None, C_CONV, C_CONV_DIM), lambda b, c: (l, 0, 0)),
                  pl.BlockSpec((1, C_CONV_DIM), lambda b, c: (0, 0)),
                  pl.BlockSpec((SUBLANES, LANES), lambda b, c: (0, 0)),
                  pl.BlockSpec((1, C_WIDTH), lambda b, c: (0, 0)),
                  pl.BlockSpec((1, C_WIDTH), lambda b, c: (0, 0))],
        out_specs=[pl.BlockSpec((1, L, C_WIDTH), lambda b, c: (b, c, 0)),
                   pl.BlockSpec((1, C_HEADS, C_HEAD_DIM, C_STATE), lambda b, c: (b, 0, 0, 0))],
        out_shape=[jax.ShapeDtypeStruct((nb, t, C_WIDTH), _F32),
                   jax.ShapeDtypeStruct((nb, C_HEADS, C_HEAD_DIM, C_STATE), _F32)],
        scratch_shapes=[pltpu.VMEM((L + 2 * SUBLANES, C_CONV_DIM), _F32),
                        pltpu.VMEM((C_HEADS, C_HEAD_DIM, C_STATE), _F32)],
        compiler_params=_cparams(("arbitrary", "arbitrary")),
        name="mamba_prompt",
    )(z, xbc, dt, conv_w, conv_b_row, hp, dsk_row, ng_row)


def _pe_sums(pe_ref):
    pk = jnp.sum(pe_ref[0], axis=0, keepdims=True)
    pv = jnp.sum(pe_ref[1], axis=0, keepdims=True)
    return jnp.concatenate([pk, pk, pv, pv], axis=-1)


def _compress_rows(m, w_ref):
    hd = B_HEAD_DIM
    return jnp.concatenate([_dot(m[:, 0:hd], w_ref[0]), _dot(m[:, hd:2 * hd], w_ref[0]),
                            _dot(m[:, 2 * hd:3 * hd], w_ref[1]), _dot(m[:, 3 * hd:], w_ref[1])], axis=-1)


def _compress_kernel(kv_ref, pe_ref, w_ref, o_ref, *, nblk):
    sums = jnp.sum(kv_ref[0].reshape(nblk, CMP_BLOCK, 2 * B_KV_WIDTH), axis=1)
    m = (sums + _pe_sums(pe_ref)) * (1.0 / CMP_BLOCK)
    o_ref[0] = _compress_rows(m, w_ref)


def _compress_prompt(kvc, pe, w, l, nblk=64):
    nb, t = kvc.shape[:2]
    rows = nblk * CMP_BLOCK
    return pl.pallas_call(
        functools.partial(_compress_kernel, nblk=nblk),
        grid=(nb, t // rows),
        in_specs=[pl.BlockSpec((1, rows, 2 * B_KV_WIDTH), lambda b, c: (b, c, 0)),
                  pl.BlockSpec((None, 2, CMP_BLOCK, B_HEAD_DIM), lambda b, c: (l, 0, 0, 0)),
                  pl.BlockSpec((None, 2, B_HEAD_DIM, B_HEAD_DIM), lambda b, c: (l, 0, 0, 0))],
        out_specs=pl.BlockSpec((1, nblk, 2 * B_KV_WIDTH), lambda b, c: (b, c, 0)),
        out_shape=jax.ShapeDtypeStruct((nb, t // CMP_BLOCK, 2 * B_KV_WIDTH), _F32),
        compiler_params=_cparams(("parallel", "parallel")),
        name="nsa_compress_prompt",
    )(kvc, pe, w)


_KEY_CHUNK = 512
_FRONT_PAD = Q_BLOCK
_NEAR = 2 * Q_BLOCK


def _nsa_tables(n_sel):
    r = np.arange(Q_BLOCK)[:, None]
    cmp_idx = np.full((2, Q_BLOCK, n_sel), REL_BUCKETS - 1, np.int32)
    for par in range(2):
        for u in (-2, -1, 0, 1):
            dist = r[:, 0] - (CMP_BLOCK - 1) - CMP_BLOCK * (2 * u + par)
            cmp_idx[par, :, u % n_sel] = _bucket_np(dist)
    c = np.arange(_NEAR)[None, :]
    dist = Q_BLOCK + r - c
    near_idx = np.where(dist >= 0, _bucket_np(dist), -1).astype(np.int32)
    c = np.arange(WINDOW + Q_BLOCK)[None, :]
    dist = r + WINDOW - c
    win_idx = np.where((dist >= 0) & (dist < WINDOW), _bucket_np(dist), -1).astype(np.int32)
    return cmp_idx, near_idx, win_idx


def _fill_bias(idx, rb_ref, hg, rel_to_last):
    base = rb_ref[(REL_BUCKETS - 1) * B_HEADS + hg] if rel_to_last else 0.0
    tile = jnp.where(idx < 0, NEG_INF, 0.0)
    for b in range(REL_BUCKETS):
        tile = jnp.where(idx == b, rb_ref[b * B_HEADS + hg] - base, tile)
    return tile


def _topk_mask(score, k):
    n = score.shape[-1]
    lane = lax.broadcasted_iota(jnp.int32, score.shape, 1)

    def body(_, carry):
        s, sel = carry
        m = jnp.max(s, axis=-1, keepdims=True)
        first = jnp.min(jnp.where(s == m, lane, n), axis=-1, keepdims=True)
        hit = lane == first
        return jnp.where(hit, -jnp.inf, s), jnp.where(hit, 1.0, sel)

    _, sel = lax.fori_loop(0, k, body, (score, jnp.zeros(score.shape, _F32)))
    return sel > 0.5


def _masked_softmax(l):
    m = jnp.maximum(jnp.max(l, axis=-1, keepdims=True), SOFTMAX_FLOOR)
    e = jnp.exp(l - m)
    return e / jnp.maximum(jnp.sum(e, axis=-1, keepdims=True), 1e-20)


def _online_step(s, v, carry):
    m_run, l_run, acc = carry
    m_new = jnp.maximum(m_run, jnp.max(s, axis=-1, keepdims=True))
    alpha = jnp.exp(m_run - m_new)
    p = jnp.exp(s - m_new)
    return (m_new, alpha * l_run + jnp.sum(p, axis=-1, keepdims=True), alpha * acc + _dot(p, v))


def _nsa_kernel(q_ref, gt_ref, kc_ref, vc_ref, ksa_ref, vs_ref, kwa_ref, vwa_ref,
                cidx_ref, nidx_ref, widx_ref, rb_ref, o_ref, bc_ref, bn_ref, bw_ref, *, n_sel):
    i = pl.program_id(1)
    hd, G = B_HEAD_DIM, B_GROUP
    QB = Q_BLOCK

    @pl.when((pl.program_id(0) == 0) & (i == 0))
    def _():
        for hg in range(B_HEADS):
            for par in range(2):
                bc_ref[hg, par] = _fill_bias(cidx_ref[par], rb_ref, hg, False)
            bn_ref[hg] = _fill_bias(nidx_ref[...], rb_ref, hg, True)
            bw_ref[hg] = _fill_bias(widx_ref[...], rb_ref, hg, False)

    q = q_ref[0] * (hd ** -0.5)
    gate = jax.nn.sigmoid(gt_ref[0])
    r_col = lax.broadcasted_iota(jnp.int32, (QB, 1), 0)
    qpos = i * QB + r_col
    n_lane = lax.broadcasted_iota(jnp.int32, (QB, n_sel), 1)
    cur = qpos // SEL_BLOCK
    vis = [CMP_BLOCK * (2 * n_lane + par) + (CMP_BLOCK - 1) <= qpos for par in range(2)]
    vis3 = jnp.concatenate([jnp.concatenate(vis, axis=-1)] * G, axis=0)
    c_near = lax.broadcasted_iota(jnp.int32, (QB, _NEAR), 1)
    near_ok = jnp.concatenate([c_near >= _FRONT_PAD - i * QB] * G, axis=0)
    c_win = lax.broadcasted_iota(jnp.int32, (QB, WINDOW + QB), 1)
    win_ok = jnp.concatenate([c_win >= WINDOW - i * QB] * G, axis=0)
    n_main = (jnp.maximum(i - 1, 0) * QB + _KEY_CHUNK - 1) // _KEY_CHUNK
    zeros_h = jnp.zeros((G * QB, hd), _F32)

    outs = []
    for h in range(B_KV_HEADS):
        q3 = jnp.concatenate([q[:, (h * G + g) * hd:(h * G + g + 1) * hd] for g in range(G)], axis=0)
        q3h = jnp.concatenate([q3, zeros_h] if h == 0 else [zeros_h, q3], axis=-1)

        lc = _dot_nt(q3, kc_ref[0, :, h * hd:(h + 1) * hd])
        bias_c = jnp.concatenate(
            [jnp.concatenate([pltpu.roll(bc_ref[h * G + g, par], 2 * i, 1) for par in range(2)], axis=-1)
             for g in range(G)], axis=0)
        p_c = _masked_softmax(jnp.where(vis3, lc + bias_c, NEG_INF))
        o_c = _dot(p_c, vc_ref[0, :, h * hd:(h + 1) * hd])
        imp = sum(p_c[g * QB:(g + 1) * QB, :n_sel] + p_c[g * QB:(g + 1) * QB, n_sel:] for g in range(G))

        forced = (n_lane == cur) | (n_lane == 0)
        started = n_lane <= cur
        imp = jnp.where(forced, FORCE_SCORE, jnp.where(started, imp, -1.0))
        allowed = _topk_mask(imp, min(SEL_TOPK, n_sel)) & started
        m_main = jnp.where(allowed & (n_lane < 2 * i - 2), 0.0, NEG_INF)
        m_near = jnp.where(allowed, 0.0, NEG_INF)
        qa_main = jnp.concatenate([q3h, jnp.concatenate([m_main] * G, axis=0)], axis=-1).astype(_MXU)
        qa_near = jnp.concatenate([q3h, jnp.concatenate([m_near] * G, axis=0)], axis=-1).astype(_MXU)

        def main_body(c, carry, qa_main=qa_main, h=h):
            r0 = pl.multiple_of(_FRONT_PAD + c * _KEY_CHUNK, Q_BLOCK)
            s = _dot_nt(qa_main, ksa_ref[0, pl.ds(r0, _KEY_CHUNK), :])
            return _online_step(s, vs_ref[0, pl.ds(r0, _KEY_CHUNK), h * hd:(h + 1) * hd], carry)

        init = (jnp.full((G * QB, 1), SOFTMAX_FLOOR, _F32), jnp.zeros((G * QB, 1), _F32), zeros_h)
        carry = lax.fori_loop(0, n_main, main_body, init)
        r0 = pl.multiple_of(i * QB, Q_BLOCK)
        s = _dot_nt(qa_near, ksa_ref[0, pl.ds(r0, _NEAR), :])
        corr = jnp.concatenate([bn_ref[h * G + g] for g in range(G)], axis=0)
        s = jnp.where(near_ok, s + corr, NEG_INF)
        _, l_s, acc_s = _online_step(s, vs_ref[0, pl.ds(r0, _NEAR), h * hd:(h + 1) * hd], carry)
        o_s = acc_s / l_s

        lw = _dot_nt(q3h, kwa_ref[0, pl.ds(r0, WINDOW + QB), :])
        bias_w = jnp.concatenate([bw_ref[h * G + g] for g in range(G)], axis=0)
        p_w = _masked_softmax(jnp.where(win_ok, lw + bias_w, NEG_INF))
        o_w = _dot(p_w, vwa_ref[0, pl.ds(r0, WINDOW + QB), h * hd:(h + 1) * hd])

        for g in range(G):
            k0 = (h * G + g) * N_BRANCH
            rows = slice(g * QB, (g + 1) * QB)
            outs.append(gate[:, k0:k0 + 1] * o_c[rows] + gate[:, k0 + 1:k0 + 2] * o_s[rows]
                        + gate[:, k0 + 2:k0 + 3] * o_w[rows])
    o_ref[0] = jnp.concatenate(outs, axis=-1)


def _nsa_prompt(q, gate, kcp, vcp, ksa, vs, kwa, vwa, rb_flat):
    nb, t = q.shape[:2]
    n_sel = t // SEL_BLOCK
    cidx, nidx, widx = _nsa_tables(n_sel)
    tp, tw = ksa.shape[1], kwa.shape[1]
    full = lambda shape: pl.BlockSpec(shape, lambda b, i: (0,) * len(shape))
    return pl.pallas_call(
        functools.partial(_nsa_kernel, n_sel=n_sel),
        grid=(nb, t // Q_BLOCK),
        in_specs=[pl.BlockSpec((1, Q_BLOCK, B_WIDTH), lambda b, i: (b, i, 0)),
                  pl.BlockSpec((1, Q_BLOCK, LANES), lambda b, i: (b, i, 0)),
                  pl.BlockSpec((1, 2 * n_sel, B_KV_WIDTH), lambda b, i: (b, 0, 0)),
                  pl.BlockSpec((1, 2 * n_sel, B_KV_WIDTH), lambda b, i: (b, 0, 0)),
                  pl.BlockSpec((1, tp, B_KV_WIDTH + n_sel), lambda b, i: (b, 0, 0)),
                  pl.BlockSpec((1, tp, B_KV_WIDTH), lambda b, i: (b, 0, 0)),
                  pl.BlockSpec((1, tw, B_KV_WIDTH), lambda b, i: (b, 0, 0)),
                  pl.BlockSpec((1, tw, B_KV_WIDTH), lambda b, i: (b, 0, 0)),
                  full(cidx.shape), full(nidx.shape), full(widx.shape),
                  pl.BlockSpec(memory_space=pltpu.SMEM)],
        out_specs=pl.BlockSpec((1, Q_BLOCK, B_WIDTH), lambda b, i: (b, i, 0)),
        out_shape=jax.ShapeDtypeStruct((nb, t, B_WIDTH), _F32),
        scratch_shapes=[pltpu.VMEM((B_HEADS, 2, Q_BLOCK, n_sel), _F32),
                        pltpu.VMEM((B_HEADS, Q_BLOCK, _NEAR), _F32),
                        pltpu.VMEM((B_HEADS, Q_BLOCK, WINDOW + Q_BLOCK), _F32)],
        compiler_params=_cparams(("arbitrary", "arbitrary")),
        name="nsa_prompt",
    )(q, gate, kcp, vcp, ksa, vs, kwa, vwa, jnp.asarray(cidx), jnp.asarray(nidx), jnp.asarray(widx), rb_flat)


def _nsa_prompt_inputs(kvcmp, kvs, kvw):
    nb, t = kvs.shape[:2]
    n_sel = t // SEL_BLOCK
    kvp = kvcmp.reshape(nb, n_sel, 2, 2 * B_KV_WIDTH).transpose(0, 2, 1, 3).reshape(nb, 2 * n_sel, 2 * B_KV_WIDTH)
    kcp = kvp[..., :B_KV_WIDTH].astype(_MXU)
    vcp = kvp[..., B_KV_WIDTH:].astype(_MXU)
    blk = (jnp.arange(t)[:, None] // SEL_BLOCK == jnp.arange(n_sel)[None, :]).astype(_MXU)
    ksa = jnp.concatenate([kvs[..., :B_KV_WIDTH].astype(_MXU), jnp.broadcast_to(blk, (nb, t, n_sel))], axis=-1)
    tail = _KEY_CHUNK
    ksa = jnp.pad(ksa, ((0, 0), (_FRONT_PAD, tail - _FRONT_PAD), (0, 0)))
    vs = jnp.pad(kvs[..., B_KV_WIDTH:].astype(_MXU), ((0, 0), (_FRONT_PAD, tail - _FRONT_PAD), (0, 0)))
    kwa = jnp.pad(kvw[..., :B_KV_WIDTH].astype(_MXU), ((0, 0), (WINDOW, 0), (0, 0)))
    vwa = jnp.pad(kvw[..., B_KV_WIDTH:].astype(_MXU), ((0, 0), (WINDOW, 0), (0, 0)))
    return kcp, vcp, ksa, vs, kwa, vwa


def _bias_rows(dist, rbt):
    n = jnp.maximum(dist, 0)
    nf = jnp.maximum(n, 1).astype(_F32)
    large = REL_MAX_EXACT + (jnp.log(nf / REL_MAX_EXACT) / math.log(REL_MAX_DIST / REL_MAX_EXACT)
                             * (REL_BUCKETS - REL_MAX_EXACT)).astype(jnp.int32)
    bucket = jnp.where(n < REL_MAX_EXACT, n, jnp.minimum(large, REL_BUCKETS - 1))
    out = jnp.zeros((SUBLANES, dist.shape[-1]), _F32)
    for b in range(REL_BUCKETS):
        out = jnp.where(bucket == b, rbt[:, b:b + 1], out)
    return out


def _scmp_kernel(pt_ref, *refs, n_pages, group):
    pages = refs[:group]
    q_ref, pe_ref, w_ref, rbt_ref, oc_ref, imp_ref, kvm_ref = refs[group:]
    pg = pl.program_id(1)
    per = PAGE_SIZE // CMP_BLOCK * group
    rows = jnp.concatenate([p[...] for p in pages], axis=0)
    kvm_ref[pl.ds(pl.multiple_of(pg * per, SUBLANES), per), :] = jnp.sum(
        rows.reshape(per, CMP_BLOCK, 2 * B_KV_WIDTH), axis=1)

    @pl.when(pg == pl.num_programs(1) - 1)
    def _():
        hd = B_HEAD_DIM
        n_cmp = n_pages * (PAGE_SIZE // CMP_BLOCK)
        n_sel = n_cmp // 2
        past = n_pages * PAGE_SIZE
        kv = _compress_rows((kvm_ref[...] + _pe_sums(pe_ref)) * (1.0 / CMP_BLOCK), w_ref)
        q8 = q_ref[0] * (hd ** -0.5)
        row = lax.broadcasted_iota(jnp.int32, (SUBLANES, 1), 0)
        head0 = row < B_GROUP
        lc = jnp.where(head0, _dot_nt(q8, kv[:, 0:hd]), _dot_nt(q8, kv[:, hd:2 * hd]))
        blk = lax.broadcasted_iota(jnp.int32, (1, n_cmp), 1)
        dist = past - (blk * CMP_BLOCK + CMP_BLOCK - 1)
        p = _masked_softmax(jnp.where(dist >= 0, lc + _bias_rows(dist, rbt_ref[...]), NEG_INF))
        oc_ref[0] = jnp.where(head0, _dot(p, kv[:, 2 * hd:3 * hd]), _dot(p, kv[:, 3 * hd:]))
        pool = (lax.broadcasted_iota(jnp.int32, (n_cmp, n_sel), 0) // (SEL_BLOCK // CMP_BLOCK)
                == lax.broadcasted_iota(jnp.int32, (n_cmp, n_sel), 1))
        pp = _dot_exact(p, jnp.where(pool, 1.0, 0.0))
        imp0 = jnp.sum(pp[0:B_GROUP], axis=0, keepdims=True)
        imp1 = jnp.sum(pp[B_GROUP:2 * B_GROUP], axis=0, keepdims=True)
        imp_ref[0] = jnp.where(row == 0, imp0, jnp.where(row == 1, imp1, 0.0))


def _sample_cmp(page_table, cache, q8, pe, w, rbt, l, group=8):
    ns, n_pages = page_table.shape
    n_cmp = n_pages * (PAGE_SIZE // CMP_BLOCK)
    page_spec = lambda k: pl.BlockSpec((None, None, PAGE_SIZE, 2 * B_KV_WIDTH),
                                       lambda b, g, pt: (l, pt[b, g * group + k], 0, 0))
    return pl.pallas_call(
        functools.partial(_scmp_kernel, n_pages=n_pages, group=group),
        grid_spec=pltpu.PrefetchScalarGridSpec(
            num_scalar_prefetch=1,
            grid=(ns, n_pages // group),
            in_specs=[page_spec(k) for k in range(group)] + [
                pl.BlockSpec((1, SUBLANES, B_HEAD_DIM), lambda b, g, pt: (b, 0, 0)),
                pl.BlockSpec((None, 2, CMP_BLOCK, B_HEAD_DIM), lambda b, g, pt: (l, 0, 0, 0)),
                pl.BlockSpec((None, 2, B_HEAD_DIM, B_HEAD_DIM), lambda b, g, pt: (l, 0, 0, 0)),
                pl.BlockSpec((SUBLANES, REL_BUCKETS), lambda b, g, pt: (0, 0))],
            out_specs=[pl.BlockSpec((1, SUBLANES, B_HEAD_DIM), lambda b, g, pt: (b, 0, 0)),
                       pl.BlockSpec((1, SUBLANES, n_cmp // 2), lambda b, g, pt: (b, 0, 0))],
            scratch_shapes=[pltpu.VMEM((n_cmp, 2 * B_KV_WIDTH), _F32)]),
        out_shape=[jax.ShapeDtypeStruct((ns, SUBLANES, B_HEAD_DIM), _F32),
                   jax.ShapeDtypeStruct((ns, SUBLANES, n_cmp // 2), _F32)],
        compiler_params=_cparams(("arbitrary", "arbitrary")),
        name="nsa_sample_cmp",
    )(page_table, *([cache] * group), q8, pe, w, rbt)


def _stopk_kernel(imp_ref, idx_ref, *, k):
    s = imp_ref[...]
    n = s.shape[-1]
    lane = lax.broadcasted_iota(jnp.int32, s.shape, 1)
    s = jnp.where(lane == 0, FORCE_SCORE, s)

    def body(it, carry):
        s, out = carry
        m = jnp.max(s, axis=-1, keepdims=True)
        first = jnp.min(jnp.where(s == m, lane, n), axis=-1, keepdims=True)
        return jnp.where(lane == first, -jnp.inf, s), jnp.where(lane == it, first, out)

    _, out = lax.fori_loop(0, k, body, (s, jnp.zeros(s.shape, jnp.int32)))
    idx_ref[...] = out


def _sample_topk(imp2d, k):
    return pl.pallas_call(
        functools.partial(_stopk_kernel, k=k),
        out_shape=jax.ShapeDtypeStruct(imp2d.shape, jnp.int32),
        name="nsa_sample_topk",
    )(imp2d)


def _sattn_kernel(pt_ref, idx_ref, *refs, n_blk, past):
    blocks = refs[:2 * n_blk]
    q_ref, gt_ref, new_ref, oc_ref, win_ref, rbt_ref, o_ref = refs[2 * n_blk:]
    b = pl.program_id(0)
    hd = B_HEAD_DIM
    q8 = q_ref[0] * (hd ** -0.5)
    rbt = rbt_ref[...]
    row = lax.broadcasted_iota(jnp.int32, (SUBLANES, 1), 0)
    head0 = row < B_GROUP
    new = new_ref[0]
    bias0 = _bias_rows(jnp.zeros((1, 1), jnp.int32), rbt)

    def attend(s, v_of_head, k_new, v_new):
        s_new = jnp.sum(q8 * k_new, axis=-1, keepdims=True) + bias0
        m = jnp.maximum(jnp.max(s, axis=-1, keepdims=True), s_new)
        p = jnp.exp(s - m)
        p_new = jnp.exp(s_new - m)
        den = jnp.sum(p, axis=-1, keepdims=True) + p_new
        num = jnp.where(head0, _dot(p, v_of_head(0)), _dot(p, v_of_head(1))) + p_new * v_new
        return num / den

    def per_head(a0, a1):
        return jnp.where(head0, a0, a1)

    t_in = lax.broadcasted_iota(jnp.int32, (1, SEL_BLOCK), 1)
    s_h, ks, vs = [], [], []
    for h in range(B_KV_HEADS):
        kb = jnp.concatenate([blocks[h * n_blk + k][:, h * hd:(h + 1) * hd] for k in range(n_blk)], axis=0)
        vs.append(jnp.concatenate(
            [blocks[h * n_blk + k][:, B_KV_WIDTH + h * hd:B_KV_WIDTH + (h + 1) * hd] for k in range(n_blk)], axis=0))
        dist = jnp.concatenate([past - (idx_ref[b, h, k] * SEL_BLOCK + t_in) for k in range(n_blk)], axis=-1)
        s_h.append(jnp.where(dist >= 0, _dot_nt(q8, kb) + _bias_rows(dist, rbt), NEG_INF))
    ksn = new[:, 2 * B_KV_WIDTH:3 * B_KV_WIDTH]
    vsn = new[:, 3 * B_KV_WIDTH:4 * B_KV_WIDTH]
    o_s = attend(per_head(s_h[0], s_h[1]), lambda h: vs[h],
                 per_head(ksn[:, :hd], ksn[:, hd:]), per_head(vsn[:, :hd], vsn[:, hd:]))

    win = win_ref[0]
    wb = win.shape[0]
    dist = wb - lax.broadcasted_iota(jnp.int32, (1, wb), 1)
    okw = (dist < WINDOW) & (past - dist >= 0)
    lw = per_head(_dot_nt(q8, win[:, 0:hd]), _dot_nt(q8, win[:, hd:2 * hd]))
    sw = jnp.where(okw, lw + _bias_rows(dist, rbt), NEG_INF)
    kwn = new[:, 4 * B_KV_WIDTH:5 * B_KV_WIDTH]
    vwn = new[:, 5 * B_KV_WIDTH:6 * B_KV_WIDTH]
    o_w = attend(sw, lambda h: win[:, B_KV_WIDTH + h * hd:B_KV_WIDTH + (h + 1) * hd],
                 per_head(kwn[:, :hd], kwn[:, hd:]), per_head(vwn[:, :hd], vwn[:, hd:]))

    gate = jax.nn.sigmoid(gt_ref[0])
    o_ref[0] = gate[:, 0:1] * oc_ref[0] + gate[:, 1:2] * o_s + gate[:, 2:3] * o_w


def _sample_attn(page_table, idx, cache, q8, gate8, kv_new, oc, win, rbt, l):
    ns, n_pages = page_table.shape
    n_blk = idx.shape[-1]
    past = n_pages * PAGE_SIZE
    per_page = PAGE_SIZE // SEL_BLOCK

    def blk_spec(h, k):
        return pl.BlockSpec((None, None, SEL_BLOCK, 2 * B_KV_WIDTH),
                            lambda b, pt, ix: (l, pt[b, ix[b, h, k] // per_page], ix[b, h, k] % per_page, 0))

    wb = win.shape[2]
    return pl.pallas_call(
        functools.partial(_sattn_kernel, n_blk=n_blk, past=past),
        grid_spec=pltpu.PrefetchScalarGridSpec(
            num_scalar_prefetch=2,
            grid=(ns,),
            in_specs=[blk_spec(h, k) for h in range(B_KV_HEADS) for k in range(n_blk)] + [
                pl.BlockSpec((1, SUBLANES, B_HEAD_DIM), lambda b, pt, ix: (b, 0, 0)),
                pl.BlockSpec((1, SUBLANES, LANES), lambda b, pt, ix: (b, 0, 0)),
                pl.BlockSpec((1, 1, 6 * B_KV_WIDTH), lambda b, pt, ix: (b, 0, 0)),
                pl.BlockSpec((1, SUBLANES, B_HEAD_DIM), lambda b, pt, ix: (b, 0, 0)),
                pl.BlockSpec((None, 1, wb, 2 * B_KV_WIDTH), lambda b, pt, ix: (l, b, 0, 0)),
                pl.BlockSpec((SUBLANES, REL_BUCKETS), lambda b, pt, ix: (0, 0))],
            out_specs=pl.BlockSpec((1, SUBLANES, B_HEAD_DIM), lambda b, pt, ix: (b, 0, 0))),
        out_shape=jax.ShapeDtypeStruct((ns, SUBLANES, B_HEAD_DIM), _F32),
        compiler_params=_cparams(("arbitrary",)),
        name="nsa_sample_attn",
    )(page_table, idx, *([cache] * (B_KV_HEADS * n_blk)), q8, gate8, kv_new, oc, win, rbt)


def _smix_kernel(uv_ref, ng_ref, wd_ref, b0_ref, xbc_ref, st_ref, cw_ref, cb_ref, dt_ref, hp_ref,
                 dsk_ref, ex_ref, oa_ref, v_ref, xdt_ref, ea_ref, y1_ref, bc_ref):
    u, v = _gelu_ln(uv_ref[...], ng_ref[...])
    v_ref[...] = v
    oa_ref[...] = u * (v * wd_ref[...] + b0_ref[...])
    conv = cb_ref[...] + xbc_ref[...] * cw_ref[C_CONV - 1:C_CONV, :]
    for k in range(C_CONV - 1):
        conv = conv + st_ref[k] * cw_ref[k:k + 1, :]
    xc = _silu(conv)
    xs = xc[:, :C_WIDTH]
    bm = xc[:, C_WIDTH:C_WIDTH + C_GROUPS * C_STATE]
    cm = xc[:, C_WIDTH + C_GROUPS * C_STATE:]
    bc_ref[...] = xc[:, C_WIDTH:]
    dt = jax.nn.softplus(dt_ref[...] + hp_ref[0:1, :])
    acum = dt * (-jnp.exp(hp_ref[1:2, :]))
    dt_rep = _dot_exact(dt, ex_ref[...])
    ea_ref[...] = jnp.exp(_dot_exact(acum, ex_ref[...]))
    xdt = dt_rep * xs
    xdt_ref[...] = xdt
    per_g = C_WIDTH // C_GROUPS
    cb = [jnp.sum(cm[:, g * C_STATE:(g + 1) * C_STATE] * bm[:, g * C_STATE:(g + 1) * C_STATE],
                  axis=-1, keepdims=True) for g in range(C_GROUPS)]
    lane = lax.broadcasted_iota(jnp.int32, xs.shape, 1)
    y1_ref[...] = jnp.where(lane < per_g, cb[0], cb[1]) * xdt + dsk_ref[...] * xs


def _sample_mix(uv, ng_row, wd_row, b0_row, xbc, st, conv_w_l, conv_b_row, dt, hp, dsk_row, expand):
    ns = uv.shape[0]
    f = lambda w: jax.ShapeDtypeStruct((ns, w), _F32)
    return pl.pallas_call(
        _smix_kernel,
        out_shape=[f(A_WIDTH), f(A_WIDTH), f(C_WIDTH), f(C_WIDTH), f(C_WIDTH), f(2 * C_GROUPS * C_STATE)],
        name="sample_gmlp_conv",
    )(uv, ng_row, wd_row, b0_row, xbc, st, conv_w_l, conv_b_row, dt, hp, dsk_row, expand)


def _sssm_kernel(h0_ref, xdt_ref, ea_ref, y1_ref, bc_ref, z_ref, ng_ref, o_ref, hout_ref):
    per_g = C_WIDTH // C_GROUPS
    h0 = h0_ref[0]
    bc = bc_ref[0]
    bm = bc[:, :C_GROUPS * C_STATE]
    cm = bc[:, C_GROUPS * C_STATE:]
    c8 = [jnp.broadcast_to(cm[:, g * C_STATE:(g + 1) * C_STATE], (SUBLANES, C_STATE)) for g in range(C_GROUPS)]
    ch = jnp.concatenate([_dot_nt(c8[g], h0[g * per_g:(g + 1) * per_g, :])[0:1] for g in range(C_GROUPS)], axis=-1)
    y = y1_ref[0] + ch * ea_ref[0]
    o_ref[0] = _rms(y * _silu(z_ref[0]), ng_ref[...])
    row = lax.broadcasted_iota(jnp.int32, (LANES, C_WIDTH), 0)
    cols = jnp.where(row == 0, xdt_ref[0], jnp.where(row == 1, ea_ref[0], 0.0)).T
    rsel = lax.broadcasted_iota(jnp.int32, (C_WIDTH, C_STATE), 0) < per_g
    b_full = jnp.where(rsel, bm[:, :C_STATE], bm[:, C_STATE:])
    hout_ref[0] = h0 * cols[:, 1:2] + cols[:, 0:1] * b_full


def _sample_ssm(h0, xdt, ea, y1, bc, z, ng_row, l):
    ns = xdt.shape[0]
    r3 = lambda a: a.reshape(ns, 1, a.shape[-1])
    row_spec = lambda w: pl.BlockSpec((1, 1, w), lambda b: (b, 0, 0))
    out, hout = pl.pallas_call(
        _sssm_kernel,
        grid=(ns,),
        in_specs=[pl.BlockSpec((None, 1, C_WIDTH, C_STATE), lambda b: (l, b, 0, 0)),
                  row_spec(C_WIDTH), row_spec(C_WIDTH), row_spec(C_WIDTH), row_spec(2 * C_GROUPS * C_STATE),
                  row_spec(C_WIDTH), pl.BlockSpec((1, C_WIDTH), lambda b: (0, 0))],
        out_specs=[row_spec(C_WIDTH), pl.BlockSpec((1, C_WIDTH, C_STATE), lambda b: (b, 0, 0))],
        out_shape=[jax.ShapeDtypeStruct((ns, 1, C_WIDTH), _F32),
                   jax.ShapeDtypeStruct((ns, C_WIDTH, C_STATE), _F32)],
        compiler_params=_cparams(("parallel",)),
        name="sample_ssm",
    )(h0, r3(xdt), r3(ea), r3(y1), r3(bc), r3(z), ng_row)
    return out.reshape(ns, C_WIDTH), hout


def _pad_cols(a, width):
    return jnp.pad(a, ((0, 0),) * (a.ndim - 1) + ((0, width - a.shape[-1]),))


def _pack_w_in(w_in):
    o = np.cumsum((0, A_WIDTH, A_WIDTH, B_WIDTH, 6 * B_KV_WIDTH, _GATE_COLS, C_WIDTH, C_CONV_DIM, C_HEADS))
    u0, q0, kv0, gate0, z0, xbc0, dt0, end = o[0], o[2], o[3], o[4], o[5], o[6], o[7], o[8]
    parts = [w_in[..., u0:gate0], w_in[..., z0:xbc0], w_in[..., xbc0:dt0],
             _pad_cols(w_in[..., gate0:z0], LANES), _pad_cols(w_in[..., dt0:end], LANES)]
    return jnp.concatenate(parts, axis=-1).astype(_MXU)


def _head_rows(dt_bias, a_log):
    hp = jnp.zeros((DEPTH, SUBLANES, LANES), _F32)
    hp = hp.at[:, 0, :C_HEADS].set(dt_bias)
    return hp.at[:, 1, :C_HEADS].set(a_log)


def kernel(x_prompt, x_sample, cache_cmp_kv, cache_slc_kv, page_table, state_win_kv, state_conv, state_ssm,
           norm_g, ffn_w_gate, ffn_w_up, ffn_w_down, w_in, w_out, gmlp_norm_g, gmlp_w_s, gmlp_b_s,
           nsa_pe_cmp, nsa_w_cmp, rel_bias, conv_w, conv_b, dt_bias, a_log, d_skip, ssm_norm_g):
    bp, t = x_prompt.shape[:2]
    ns = x_sample.shape[0]
    n_pages = page_table.shape[1]
    n_phys = cache_cmp_kv.shape[1]
    kvw = 2 * B_KV_WIDTH
    tm_p, tm_s = 512, ns

    wg, wu, wd = (w.astype(_MXU) for w in (ffn_w_gate, ffn_w_up, ffn_w_down))
    w_in_p = _pack_w_in(w_in)
    w_out_b = w_out.astype(_MXU)
    bs_t = jnp.swapaxes(gmlp_b_s, 1, 2)
    wdiag = jnp.repeat(gmlp_w_s[:, :, 0, 0], A_HEAD_DIM, axis=-1)
    b0 = jnp.repeat(gmlp_b_s[:, :, 0], A_HEAD_DIM, axis=-1)
    hp = _head_rows(dt_bias, a_log)
    dsk = jnp.repeat(d_skip, C_HEAD_DIM, axis=-1)
    expand = (jnp.arange(LANES)[:, None] == jnp.arange(C_WIDTH)[None, :] // C_HEAD_DIM).astype(_F32)
    rb_flat = rel_bias.reshape(-1)
    rbt = _pad_cols(rel_bias, SUBLANES).T
    cache_c = cache_cmp_kv.reshape(DEPTH, n_phys, PAGE_SIZE, kvw)
    cache_s = cache_slc_kv.reshape(DEPTH, n_phys, PAGE_SIZE, kvw)
    wbuf = state_win_kv.shape[2]
    win_state = state_win_kv.reshape(DEPTH, ns, wbuf, kvw)
    ssm_state = state_ssm.reshape(DEPTH, ns, C_WIDTH, C_STATE)

    xp = x_prompt.reshape(bp * t, D_MODEL)
    xs = x_sample.reshape(ns, D_MODEL)
    outs = [[] for _ in range(11)]
    for l in range(DEPTH):
        g = norm_g[l]
        row = lambda a: a.reshape(1, -1)
        xp = _ffn(xp, g[0:2], wg, wu, wd, l, 0, tm_p)
        xs = _ffn(xs, g[0:2], wg, wu, wd, l, 0, tm_s)

        uv, q, kvc, kvs, kvwin, z, xbc, gate, dt = _inproj(xp, row(g[2]), w_in_p, l, tm_p)
        oa = _gmlp_prompt(uv, row(gmlp_norm_g[l]), gmlp_w_s, bs_t, l)
        b3 = lambda a: a.reshape(bp, t, a.shape[-1])
        kvcmp = _compress_prompt(b3(kvc), nsa_pe_cmp, nsa_w_cmp, l)
        ob = _nsa_prompt(b3(q), b3(gate), *_nsa_prompt_inputs(kvcmp, b3(kvs), b3(kvwin)), rb_flat)
        oc, h_p = _mamba_prompt(b3(z), b3(xbc), b3(dt), conv_w, row(conv_b[l]), hp[l], row(dsk[l]),
                                row(ssm_norm_g[l]), l)
        xp = _outproj(xp, oa, ob.reshape(bp * t, B_WIDTH), oc.reshape(bp * t, C_WIDTH), row(g[3]), w_out_b, l, tm_p)
        wkeep = min(WINDOW, t)
        outs[0].append(kvc.reshape(bp, t, 2, B_KV_HEADS, B_HEAD_DIM))
        outs[1].append(kvs.reshape(bp, t, 2, B_KV_HEADS, B_HEAD_DIM))
        outs[2].append(b3(kvwin)[:, t - wkeep:].reshape(bp, wkeep, 2, B_KV_HEADS, B_HEAD_DIM))
        outs[3].append(b3(xbc)[:, t - (C_CONV - 1):])
        outs[4].append(h_p)

        uv, q, kvc, kvs, kvwin, z, xbc, gate, dt = _inproj(xs, row(g[2]), w_in_p, l, tm_s)
        q8 = jnp.pad(q.reshape(ns, B_HEADS, B_HEAD_DIM), ((0, 0), (0, SUBLANES - B_HEADS), (0, 0)))
        gate8 = jnp.pad(gate[:, :_GATE_COLS].reshape(ns, B_HEADS, N_BRANCH),
                        ((0, 0), (0, SUBLANES - B_HEADS), (0, LANES - N_BRANCH)))
        o_cmp, imp = _sample_cmp(page_table, cache_c, q8, nsa_pe_cmp, nsa_w_cmp, rbt, l)
        n_sel_past = imp.shape[-1]
        idx = _sample_topk(imp.reshape(ns * SUBLANES, n_sel_past), SEL_TOPK - 1)
        idx = idx.reshape(ns, SUBLANES, n_sel_past)[:, :B_KV_HEADS, :SEL_TOPK - 1]
        kv_new = jnp.concatenate([kvc, kvs, kvwin], axis=-1).reshape(ns, 1, 6 * B_KV_WIDTH)
        ob8 = _sample_attn(page_table, idx, cache_s, q8, gate8, kv_new, o_cmp, win_state, rbt, l)
        ob = ob8[:, :B_HEADS].reshape(ns, B_WIDTH)
        st = jnp.swapaxes(state_conv[l], 0, 1)
        oa, v_rows, xdt, ea, y1, bc = _sample_mix(uv, row(gmlp_norm_g[l]), row(wdiag[l]), row(b0[l]), xbc, st,
                                                  conv_w[l], row(conv_b[l]), dt, hp[l], row(dsk[l]), expand)
        oc, h_s = _sample_ssm(ssm_state, xdt, ea, y1, bc, z, row(ssm_norm_g[l]), l)
        xs = _outproj(xs, oa, ob, oc, row(g[3]), w_out_b, l, tm_s)
        outs[5].append(kvc.reshape(ns, 1, 2, B_KV_HEADS, B_HEAD_DIM))
        outs[6].append(kvs.reshape(ns, 1, 2, B_KV_HEADS, B_HEAD_DIM))
        outs[7].append(jnp.concatenate([state_win_kv[l][:, 1:], kvwin.reshape(ns, 1, 2, B_KV_HEADS, B_HEAD_DIM)],
                                       axis=1))
        outs[8].append(jnp.concatenate([state_conv[l][:, 1:], xbc[:, None]], axis=1))
        outs[9].append(h_s.reshape(ns, C_HEADS, C_HEAD_DIM, C_STATE))
        outs[10].append(v_rows.reshape(ns, 1, A_WIDTH))

        xp = _ffn(xp, g[4:6], wg, wu, wd, l, 1, tm_p)
        xs = _ffn(xs, g[4:6], wg, wu, wd, l, 1, tm_s)
    stacked = [jnp.stack(o) for o in outs]
    return (xp.reshape(bp, t, D_MODEL), xs.reshape(ns, 1, D_MODEL), *stacked)
```

```python
import functools
import math

import numpy as np
import jax
import jax.numpy as jnp
from jax import lax
from jax.experimental import pallas as pl
from jax.experimental.pallas import tpu as pltpu

D_MODEL = 1024
DEPTH = 4
PAGE_SIZE = 128
A_HEADS, A_HEAD_DIM, A_CHUNK = 4, 64, 128
A_WIDTH = A_HEADS * A_HEAD_DIM
B_HEADS, B_KV_HEADS, B_HEAD_DIM = 6, 2, 64
B_GROUP = B_HEADS // B_KV_HEADS
B_WIDTH = B_HEADS * B_HEAD_DIM
B_KV_WIDTH = B_KV_HEADS * B_HEAD_DIM
N_BRANCH = 3
CMP_BLOCK, SEL_BLOCK, SEL_TOPK, WINDOW, Q_BLOCK = 32, 64, 16, 512, 128
FORCE_SCORE = 1e4
C_HEADS, C_HEAD_DIM, C_GROUPS, C_STATE, C_CONV = 6, 64, 2, 64, 4
C_WIDTH = C_HEADS * C_HEAD_DIM
C_CONV_DIM = C_WIDTH + 2 * C_GROUPS * C_STATE
SSD_CHUNK = 128
D_FF = 2816
REL_BUCKETS, REL_MAX_EXACT, REL_MAX_DIST = 32, 16, 128
EPS = 1e-6
NEG_INF = -1e30
SOFTMAX_FLOOR = -1e20

LANES = 128
SUBLANES = 8
VMEM_LIMIT_BYTES = 56 * 1024 * 1024

_MXU = jnp.bfloat16
_F32 = jnp.float32

_GATE_COLS = N_BRANCH * B_HEADS
_SEG = (("uv", 2 * A_WIDTH), ("q", B_WIDTH), ("z", C_WIDTH), ("xbc", C_CONV_DIM), ("gate", LANES), ("dt", LANES))
_D_IN_PAD = sum(w for _, w in _SEG)
_KV_ROWS = 2 * B_KV_WIDTH
_N_KV = 3


def _dot(a, b):
    return jnp.dot(a.astype(_MXU), b.astype(_MXU), preferred_element_type=_F32)


def _dot_nt(a, b):
    return lax.dot_general(a.astype(_MXU), b.astype(_MXU), (((1,), (1,)), ((), ())),
                           preferred_element_type=_F32)


def _dot_tn(a, b):
    return lax.dot_general(a.astype(_MXU), b.astype(_MXU), (((0,), (0,)), ((), ())),
                           preferred_element_type=_F32)


def _dot_exact(a, b):
    return jnp.dot(a, b, preferred_element_type=_F32, precision=lax.Precision.HIGHEST)


def _rms(x, g):
    return x * lax.rsqrt(jnp.mean(x * x, axis=-1, keepdims=True) + EPS) * g


def _silu(x):
    return x * jax.nn.sigmoid(x)


def _cparams(sem):
    return pltpu.CompilerParams(dimension_semantics=sem, vmem_limit_bytes=VMEM_LIMIT_BYTES)


def _bucket_np(dist):
    n = np.maximum(dist, 0)
    nf = np.maximum(n, 1).astype(np.float32)
    large = REL_MAX_EXACT + (np.log(nf / np.float32(REL_MAX_EXACT))
                             / np.float32(math.log(REL_MAX_DIST / REL_MAX_EXACT))
                             * np.float32(REL_BUCKETS - REL_MAX_EXACT)).astype(np.int32)
    large = np.minimum(large, REL_BUCKETS - 1)
    return np.where(n < REL_MAX_EXACT, n, large).astype(np.int32)


def _ffn_kernel(x_ref, g_ref, wg_ref, wu_ref, wd_ref, o_ref, h_ref, acc_ref):
    f = pl.program_id(1)

    @pl.when(f == 0)
    def _():
        h_ref[...] = _rms(x_ref[...], g_ref[0:1, :]).astype(h_ref.dtype)
        acc_ref[...] = jnp.zeros_like(acc_ref)

    h = h_ref[...]
    a = _silu(_dot(h, wg_ref[...])) * _dot(h, wu_ref[...])
    acc_ref[...] += _dot(a, wd_ref[...])

    @pl.when(f == pl.num_programs(1) - 1)
    def _():
        o_ref[...] = x_ref[...] + 0.5 * _rms(acc_ref[...], g_ref[1:2, :])


def _ffn(x, g2, wg, wu, wd, l, j, tm, tf=256):
    rows = x.shape[0]
    return pl.pallas_call(
        _ffn_kernel,
        grid=(rows // tm, D_FF // tf),
        in_specs=[pl.BlockSpec((tm, D_MODEL), lambda r, f: (r, 0)),
                  pl.BlockSpec((2, D_MODEL), lambda r, f: (0, 0)),
                  pl.BlockSpec((None, None, D_MODEL, tf), lambda r, f: (l, j, 0, f)),
                  pl.BlockSpec((None, None, D_MODEL, tf), lambda r, f: (l, j, 0, f)),
                  pl.BlockSpec((None, None, tf, D_MODEL), lambda r, f: (l, j, f, 0))],
        out_specs=pl.BlockSpec((tm, D_MODEL), lambda r, f: (r, 0)),
        out_shape=jax.ShapeDtypeStruct((rows, D_MODEL), _F32),
        scratch_shapes=[pltpu.VMEM((tm, D_MODEL), _MXU), pltpu.VMEM((tm, D_MODEL), _F32)],
        compiler_params=_cparams(("parallel", "arbitrary")),
        name="half_ffn",
    )(x, g2, wg, wu, wd)


def _inproj_kernel(x_ref, g_ref, w_ref, wkv_ref, *o_refs):
    h = _rms(x_ref[0], g_ref[...]).astype(_MXU)
    off = 0
    for (_, width), o_ref in zip(_SEG, o_refs):
        o_ref[0] = _dot(h, w_ref[:, off:off + width])
        off += width
    for k, o_ref in enumerate(o_refs[len(_SEG):]):
        o_ref[0] = _dot_nt(wkv_ref[k * _KV_ROWS:(k + 1) * _KV_ROWS, :], h)


def _inproj(x, g_row, w_in_p, w_kv_t, l, tm):
    nb, t = x.shape[:2]
    return pl.pallas_call(
        _inproj_kernel,
        grid=(nb, t // tm),
        in_specs=[pl.BlockSpec((1, tm, D_MODEL), lambda b, r: (b, r, 0)),
                  pl.BlockSpec((1, D_MODEL), lambda b, r: (0, 0)),
                  pl.BlockSpec((None, D_MODEL, _D_IN_PAD), lambda b, r: (l, 0, 0)),
                  pl.BlockSpec((None, _N_KV * _KV_ROWS, D_MODEL), lambda b, r: (l, 0, 0))],
        out_specs=[pl.BlockSpec((1, tm, w), lambda b, r: (b, r, 0)) for _, w in _SEG]
        + [pl.BlockSpec((1, _KV_ROWS, tm), lambda b, r: (b, 0, r))] * _N_KV,
        out_shape=[jax.ShapeDtypeStruct((nb, t, w), _F32) for _, w in _SEG]
        + [jax.ShapeDtypeStruct((nb, _KV_ROWS, t), _F32)] * _N_KV,
        compiler_params=_cparams(("parallel", "parallel")),
        name="in_proj",
    )(x, g_row, w_in_p, w_kv_t)


def _outproj_kernel(x_ref, oa_ref, ob_ref, oc_ref, g_ref, w_ref, o_ref):
    y = (_dot(oa_ref[...], w_ref[0:A_WIDTH, :])
         + _dot(ob_ref[...], w_ref[A_WIDTH:A_WIDTH + B_WIDTH, :])
         + _dot(oc_ref[...], w_ref[A_WIDTH + B_WIDTH:, :]))
    o_ref[...] = x_ref[...] + _rms(y, g_ref[...])


def _outproj(x, oa, ob, oc, g_row, w_out, l, tm):
    rows = x.shape[0]
    return pl.pallas_call(
        _outproj_kernel,
        grid=(rows // tm,),
        in_specs=[pl.BlockSpec((tm, D_MODEL), lambda r: (r, 0)),
                  pl.BlockSpec((tm, A_WIDTH), lambda r: (r, 0)),
                  pl.BlockSpec((tm, B_WIDTH), lambda r: (r, 0)),
                  pl.BlockSpec((tm, C_WIDTH), lambda r: (r, 0)),
                  pl.BlockSpec((1, D_MODEL), lambda r: (0, 0)),
                  pl.BlockSpec((None, D_MODEL, D_MODEL), lambda r: (l, 0, 0))],
        out_specs=pl.BlockSpec((tm, D_MODEL), lambda r: (r, 0)),
        out_shape=jax.ShapeDtypeStruct((rows, D_MODEL), _F32),
        compiler_params=_cparams(("parallel",)),
        name="out_proj",
    )(x, oa, ob, oc, g_row, w_out)


def _gelu_ln(uv, ng):
    u = jax.nn.gelu(uv[:, :A_WIDTH])
    v = jax.nn.gelu(uv[:, A_WIDTH:])
    mu = jnp.mean(v, axis=-1, keepdims=True)
    var = jnp.mean(jnp.square(v - mu), axis=-1, keepdims=True)
    return u, (v - mu) * lax.rsqrt(var + EPS) * ng


def _gmlp_kernel(uv_ref, ng_ref, ws_ref, bs_ref, o_ref, *, chunks):
    row = lax.broadcasted_iota(jnp.int32, (A_CHUNK, A_CHUNK), 0)
    col = lax.broadcasted_iota(jnp.int32, (A_CHUNK, A_CHUNK), 1)
    ws = [jnp.where(col <= row, ws_ref[h], 0.0).astype(_MXU) for h in range(A_HEADS)]
    for c in range(chunks):
        u, v = _gelu_ln(uv_ref[c * A_CHUNK:(c + 1) * A_CHUNK, :], ng_ref[...])
        sg = [_dot(ws[h], v[:, h * A_HEAD_DIM:(h + 1) * A_HEAD_DIM]) + bs_ref[:, h:h + 1]
              for h in range(A_HEADS)]
        o_ref[c * A_CHUNK:(c + 1) * A_CHUNK, :] = u * jnp.concatenate(sg, axis=-1)


def _gmlp_prompt(uv, ng_row, ws, bs_t, l, chunks=4):
    rows = uv.shape[0]
    tm = chunks * A_CHUNK
    return pl.pallas_call(
        functools.partial(_gmlp_kernel, chunks=chunks),
        grid=(rows // tm,),
        in_specs=[pl.BlockSpec((tm, 2 * A_WIDTH), lambda r: (r, 0)),
                  pl.BlockSpec((1, A_WIDTH), lambda r: (0, 0)),
                  pl.BlockSpec((None, A_HEADS, A_CHUNK, A_CHUNK), lambda r: (l, 0, 0, 0)),
                  pl.BlockSpec((None, A_CHUNK, A_HEADS), lambda r: (l, 0, 0))],
        out_specs=pl.BlockSpec((tm, A_WIDTH), lambda r: (r, 0)),
        out_shape=jax.ShapeDtypeStruct((rows, A_WIDTH), _F32),
        compiler_params=_cparams(("parallel",)),
        name="gmlp_prompt",
    )(uv, ng_row, ws, bs_t)


def _mamba_kernel(z_ref, xbc_ref, dt_ref, cw_ref, cb_ref, hp_ref, dsk_ref, ng_ref,
                  o_ref, hout_ref, xp_ref, hs_ref):
    t = pl.program_id(1)
    L = SSD_CHUNK
    hist = SUBLANES

    @pl.when(t == 0)
    def _():
        xp_ref[0:hist, :] = jnp.zeros((hist, C_CONV_DIM), _F32)
        hs_ref[...] = jnp.zeros_like(hs_ref)

    xp_ref[hist:hist + L, :] = xbc_ref[0]
    conv = cb_ref[...]
    for k in range(C_CONV):
        conv = conv + xp_ref[pl.ds(hist - (C_CONV - 1) + k, L), :] * cw_ref[k:k + 1, :]
    xp_ref[0:hist, :] = xp_ref[L:L + hist, :]
    xc = _silu(conv)
    xs = xc[:, :C_WIDTH]
    bm = xc[:, C_WIDTH:C_WIDTH + C_GROUPS * C_STATE]
    cm = xc[:, C_WIDTH + C_GROUPS * C_STATE:]

    dt = jax.nn.softplus(dt_ref[0] + hp_ref[0:1, :])
    a_row = -jnp.exp(hp_ref[1:2, :])
    row = lax.broadcasted_iota(jnp.int32, (L, L), 0)
    col = lax.broadcasted_iota(jnp.int32, (L, L), 1)
    causal = col <= row
    acum = _dot_exact(jnp.where(causal, 1.0, 0.0), dt * a_row)
    acum_t = acum.T
    dt_t = dt.T
    cb = [_dot_nt(cm[:, g * C_STATE:(g + 1) * C_STATE], bm[:, g * C_STATE:(g + 1) * C_STATE])
          for g in range(C_GROUPS)]
    ys = []
    for h in range(C_HEADS):
        g = h // (C_HEADS // C_GROUPS)
        ac_col = acum[:, h:h + 1]
        seg = ac_col - acum_t[h:h + 1, :]
        decay = jnp.where(causal, jnp.exp(jnp.where(causal, seg, 0.0)), 0.0)
        scores = cb[g] * decay * dt_t[h:h + 1, :]
        x_h = xs[:, h * C_HEAD_DIM:(h + 1) * C_HEAD_DIM]
        b_g = bm[:, g * C_STATE:(g + 1) * C_STATE]
        c_g = cm[:, g * C_STATE:(g + 1) * C_STATE]
        hs = hs_ref[h]
        ys.append(_dot(scores, x_h) + _dot_nt(c_g, hs) * jnp.exp(ac_col))
        ac_last = acum[L - 1:L, h:h + 1]
        w_end = jnp.exp(ac_last - ac_col) * dt[:, h:h + 1]
        hs_ref[h] = hs * jnp.exp(ac_last) + _dot_tn(x_h * w_end, b_g)
    y = jnp.concatenate(ys, axis=-1) + dsk_ref[...] * xs
    o_ref[0] = _rms(y * _silu(z_ref[0]), ng_ref[...])

    @pl.when(t == pl.num_programs(1) - 1)
    def _():
        hout_ref[0] = hs_ref[...]


def _mamba_prompt(z, xbc, dt, conv_w, conv_b_row, hp, dsk_row, ng_row, l):
    nb, t = z.shape[:2]
    L = SSD_CHUNK
    return pl.pallas_call(
        _mamba_kernel,
        grid=(nb, t // L),
        in_specs=[pl.BlockSpec((1, L, C_WIDTH), lambda b, c: (b, c, 0)),
                  pl.BlockSpec((1, L, C_CONV_DIM), lambda b, c: (b, c, 0)),
                  pl.BlockSpec((1, L, LANES), lambda b, c: (b, c, 0)),
                  pl.BlockSpec((None, C_CONV, C_CONV_DIM), lambda b, c: (l, 0, 0)),
                  pl.BlockSpec((1, C_CONV_DIM), lambda b, c: (0, 0)),
                  pl.BlockSpec((SUBLANES, LANES), lambda b, c: (0, 0)),
                  pl.BlockSpec((1, C_WIDTH), lambda b, c: (0, 0)),
                  pl.BlockSpec((1, C_WIDTH), lambda b, c: (0, 0))],
        out_specs=[pl.BlockSpec((1, L, C_WIDTH), lambda b, c: (b, c, 0)),
                   pl.BlockSpec((1, C_HEADS, C_HEAD_DIM, C_STATE), lambda b, c: (b, 0, 0, 0))],
        out_shape=[jax.ShapeDtypeStruct((nb, t, C_WIDTH), _F32),
                   jax.ShapeDtypeStruct((nb, C_HEADS, C_HEAD_DIM, C_STATE), _F32)],
        scratch_shapes=[pltpu.VMEM((L + 2 * SUBLANES, C_CONV_DIM), _F32),
                        pltpu.VMEM((C_HEADS, C_HEAD_DIM, C_STATE), _F32)],
        compiler_params=_cparams(("arbitrary", "arbitrary")),
        name="mamba_prompt",
    )(z, xbc, dt, conv_w, conv_b_row, hp, dsk_row, ng_row)


def _split_dot(x, p):
    hi = x.astype(_MXU)
    lo = (x - hi.astype(_F32)).astype(_MXU)
    return jnp.dot(hi, p, preferred_element_type=_F32) + jnp.dot(lo, p, preferred_element_type=_F32)


def _pool_matrix(n_tok, per, n_out, first=0):
    r = lax.broadcasted_iota(jnp.int32, (n_tok, n_out), 0)
    c = lax.broadcasted_iota(jnp.int32, (n_tok, n_out), 1)
    return jnp.where(c == first + r // per, 1.0, 0.0).astype(_MXU)


def _pe_sums_t(pet_ref):
    pk = jnp.sum(pet_ref[0], axis=-1, keepdims=True)
    pv = jnp.sum(pet_ref[1], axis=-1, keepdims=True)
    return jnp.concatenate([pk, pk, pv, pv], axis=0)


def _compress_cols(m, wt_ref):
    hd = B_HEAD_DIM
    return jnp.concatenate([_dot(wt_ref[0], m[0:hd]), _dot(wt_ref[0], m[hd:2 * hd]),
                            _dot(wt_ref[1], m[2 * hd:3 * hd]), _dot(wt_ref[1], m[3 * hd:])], axis=0)


def _compress_kernel(kv_ref, pet_ref, wt_ref, o_ref, *, chunk):
    t = kv_ref.shape[-1]
    pool = _pool_matrix(chunk, CMP_BLOCK, chunk // CMP_BLOCK)
    sums = jnp.concatenate([_split_dot(kv_ref[0, :, c * chunk:(c + 1) * chunk], pool) for c in range(t // chunk)],
                           axis=-1)
    o_ref[0] = _compress_cols((sums + _pe_sums_t(pet_ref)) * (1.0 / CMP_BLOCK), wt_ref)


def _compress_prompt(kvc_t, pe_t, w_t, l, chunk=2048):
    nb, _, t = kvc_t.shape
    return pl.pallas_call(
        functools.partial(_compress_kernel, chunk=min(chunk, t)),
        grid=(nb,),
        in_specs=[pl.BlockSpec((1, _KV_ROWS, t), lambda b: (b, 0, 0)),
                  pl.BlockSpec((None, 2, B_HEAD_DIM, CMP_BLOCK), lambda b: (l, 0, 0, 0)),
                  pl.BlockSpec((None, 2, B_HEAD_DIM, B_HEAD_DIM), lambda b: (l, 0, 0, 0))],
        out_specs=pl.BlockSpec((1, _KV_ROWS, t // CMP_BLOCK), lambda b: (b, 0, 0)),
        out_shape=jax.ShapeDtypeStruct((nb, _KV_ROWS, t // CMP_BLOCK), _F32),
        compiler_params=_cparams(("parallel",)),
        name="nsa_compress_prompt",
    )(kvc_t, pe_t, w_t)


_KEY_CHUNK = 512
_FRONT_PAD = Q_BLOCK
_NEAR = 2 * Q_BLOCK


def _nsa_tables(n_sel):
    r = np.arange(Q_BLOCK)[:, None]
    cmp_idx = np.full((2, Q_BLOCK, n_sel), REL_BUCKETS - 1, np.int32)
    for par in range(2):
        for u in (-2, -1, 0, 1):
            dist = r[:, 0] - (CMP_BLOCK - 1) - CMP_BLOCK * (2 * u + par)
            cmp_idx[par, :, u % n_sel] = _bucket_np(dist)
    c = np.arange(_NEAR)[None, :]
    dist = Q_BLOCK + r - c
    near_idx = np.where(dist >= 0, _bucket_np(dist), -1).astype(np.int32)
    c = np.arange(WINDOW + Q_BLOCK)[None, :]
    dist = r + WINDOW - c
    win_idx = np.where((dist >= 0) & (dist < WINDOW), _bucket_np(dist), -1).astype(np.int32)
    return cmp_idx, near_idx, win_idx


def _fill_bias(idx, rb_ref, hg, rel_to_last):
    base = rb_ref[(REL_BUCKETS - 1) * B_HEADS + hg] if rel_to_last else 0.0
    tile = jnp.where(idx < 0, NEG_INF, 0.0)
    for b in range(REL_BUCKETS):
        tile = jnp.where(idx == b, rb_ref[b * B_HEADS + hg] - base, tile)
    return tile


def _topk_mask(score, k):
    n = score.shape[-1]
    lane = lax.broadcasted_iota(jnp.int32, score.shape, 1)

    def body(_, carry):
        s, sel = carry
        m = jnp.max(s, axis=-1, keepdims=True)
        first = jnp.min(jnp.where(s == m, lane, n), axis=-1, keepdims=True)
        hit = lane == first
        return jnp.where(hit, -jnp.inf, s), jnp.where(hit, 1.0, sel)

    _, sel = lax.fori_loop(0, k, body, (score, jnp.zeros(score.shape, _F32)))
    return sel > 0.5


def _masked_softmax(l):
    m = jnp.maximum(jnp.max(l, axis=-1, keepdims=True), SOFTMAX_FLOOR)
    e = jnp.exp(l - m)
    return e / jnp.maximum(jnp.sum(e, axis=-1, keepdims=True), 1e-20)


def _online_step(s, vt, carry):
    m_run, l_run, acc = carry
    m_new = jnp.maximum(m_run, jnp.max(s, axis=-1, keepdims=True))
    alpha = jnp.exp(m_run - m_new)
    p = jnp.exp(s - m_new)
    return (m_new, alpha * l_run + jnp.sum(p, axis=-1, keepdims=True), alpha * acc + _dot_nt(p, vt))


def _key_tiles(ref, first, n, rows=slice(None)):
    return jnp.concatenate([ref[0, first + j, rows, :] for j in range(n)], axis=-1)


def _nsa_kernel(q_ref, gt_ref, kc_ref, vc_ref, ksa_ref, vs_ref, kwa_ref, vwa_ref,
                cidx_ref, nidx_ref, widx_ref, rb_ref, o_ref, bc_ref, bn_ref, bw_ref, *, n_sel):
    i = pl.program_id(1)
    hd, G = B_HEAD_DIM, B_GROUP
    QB = Q_BLOCK

    @pl.when((pl.program_id(0) == 0) & (i == 0))
    def _():
        for hg in range(B_HEADS):
            for par in range(2):
                bc_ref[hg, par] = _fill_bias(cidx_ref[par], rb_ref, hg, False)
            bn_ref[hg] = _fill_bias(nidx_ref[...], rb_ref, hg, True)
            bw_ref[hg] = _fill_bias(widx_ref[...], rb_ref, hg, False)

    q = q_ref[0] * (hd ** -0.5)
    gate = jax.nn.sigmoid(gt_ref[0])
    r_col = lax.broadcasted_iota(jnp.int32, (QB, 1), 0)
    qpos = i * QB + r_col
    n_lane = lax.broadcasted_iota(jnp.int32, (QB, n_sel), 1)
    cur = qpos // SEL_BLOCK
    vis = [CMP_BLOCK * (2 * n_lane + par) + (CMP_BLOCK - 1) <= qpos for par in range(2)]
    vis3 = jnp.concatenate([jnp.concatenate(vis, axis=-1)] * G, axis=0)
    c_near = lax.broadcasted_iota(jnp.int32, (QB, _NEAR), 1)
    near_ok = jnp.concatenate([c_near >= _FRONT_PAD - i * QB] * G, axis=0)
    c_win = lax.broadcasted_iota(jnp.int32, (QB, WINDOW + QB), 1)
    win_ok = jnp.concatenate([c_win >= WINDOW - i * QB] * G, axis=0)
    n_main = (jnp.maximum(i - 1, 0) * QB + _KEY_CHUNK - 1) // _KEY_CHUNK
    zeros_h = jnp.zeros((G * QB, hd), _F32)

    outs = []
    for h in range(B_KV_HEADS):
        q3 = jnp.concatenate([q[:, (h * G + g) * hd:(h * G + g + 1) * hd] for g in range(G)], axis=0)
        q3h = jnp.concatenate([q3, zeros_h] if h == 0 else [zeros_h, q3], axis=-1)

        lc = _dot(q3, kc_ref[0, h * hd:(h + 1) * hd, :])
        bias_c = jnp.concatenate(
            [jnp.concatenate([pltpu.roll(bc_ref[h * G + g, par], 2 * i, 1) for par in range(2)], axis=-1)
             for g in range(G)], axis=0)
        p_c = _masked_softmax(jnp.where(vis3, lc + bias_c, NEG_INF))
        o_c = _dot_nt(p_c, vc_ref[0, h * hd:(h + 1) * hd, :])
        imp = sum(p_c[g * QB:(g + 1) * QB, :n_sel] + p_c[g * QB:(g + 1) * QB, n_sel:] for g in range(G))

        forced = (n_lane == cur) | (n_lane == 0)
        started = n_lane <= cur
        imp = jnp.where(forced, FORCE_SCORE, jnp.where(started, imp, -1.0))
        allowed = _topk_mask(imp, min(SEL_TOPK, n_sel)) & started
        m_main = jnp.where(allowed & (n_lane < 2 * i - 2), 0.0, NEG_INF)
        m_near = jnp.where(allowed, 0.0, NEG_INF)
        qa_main = jnp.concatenate([q3h, jnp.concatenate([m_main] * G, axis=0)], axis=-1).astype(_MXU)
        qa_near = jnp.concatenate([q3h, jnp.concatenate([m_near] * G, axis=0)], axis=-1).astype(_MXU)

        hrows = slice(h * hd, (h + 1) * hd)
        per_chunk = _KEY_CHUNK // LANES

        def main_body(c, carry, qa_main=qa_main, hrows=hrows):
            t0 = _FRONT_PAD // LANES + c * per_chunk
            s = _dot(qa_main, _key_tiles(ksa_ref, t0, per_chunk))
            return _online_step(s, _key_tiles(vs_ref, t0, per_chunk, hrows), carry)

        init = (jnp.full((G * QB, 1), SOFTMAX_FLOOR, _F32), jnp.zeros((G * QB, 1), _F32), zeros_h)
        carry = lax.fori_loop(0, n_main, main_body, init)
        t0 = i * (QB // LANES)
        s = _dot(qa_near, _key_tiles(ksa_ref, t0, _NEAR // LANES))
        corr = jnp.concatenate([bn_ref[h * G + g] for g in range(G)], axis=0)
        s = jnp.where(near_ok, s + corr, NEG_INF)
        _, l_s, acc_s = _online_step(s, _key_tiles(vs_ref, t0, _NEAR // LANES, hrows), carry)
        o_s = acc_s / l_s

        n_win = (WINDOW + QB) // LANES
        lw = _dot(q3h, _key_tiles(kwa_ref, t0, n_win))
        bias_w = jnp.concatenate([bw_ref[h * G + g] for g in range(G)], axis=0)
        p_w = _masked_softmax(jnp.where(win_ok, lw + bias_w, NEG_INF))
        o_w = _dot_nt(p_w, _key_tiles(vwa_ref, t0, n_win, hrows))

        for g in range(G):
            k0 = (h * G + g) * N_BRANCH
            rows = slice(g * QB, (g + 1) * QB)
            outs.append(gate[:, k0:k0 + 1] * o_c[rows] + gate[:, k0 + 1:k0 + 2] * o_s[rows]
                        + gate[:, k0 + 2:k0 + 3] * o_w[rows])
    o_ref[0] = jnp.concatenate(outs, axis=-1)


def _nsa_prompt(q, gate, kcp, vcp, ksa, vs, kwa, vwa, rb_flat):
    nb, t = q.shape[:2]
    n_sel = t // SEL_BLOCK
    cidx, nidx, widx = _nsa_tables(n_sel)
    tp, tw = ksa.shape[1], kwa.shape[1]
    full = lambda shape: pl.BlockSpec(shape, lambda b, i: (0,) * len(shape))
    return pl.pallas_call(
        functools.partial(_nsa_kernel, n_sel=n_sel),
        grid=(nb, t // Q_BLOCK),
        in_specs=[pl.BlockSpec((1, Q_BLOCK, B_WIDTH), lambda b, i: (b, i, 0)),
                  pl.BlockSpec((1, Q_BLOCK, LANES), lambda b, i: (b, i, 0)),
                  pl.BlockSpec((1, B_KV_WIDTH, 2 * n_sel), lambda b, i: (b, 0, 0)),
                  pl.BlockSpec((1, B_KV_WIDTH, 2 * n_sel), lambda b, i: (b, 0, 0)),
                  pl.BlockSpec((1, tp, B_KV_WIDTH + n_sel, LANES), lambda b, i: (b, 0, 0, 0)),
                  pl.BlockSpec((1, tp, B_KV_WIDTH, LANES), lambda b, i: (b, 0, 0, 0)),
                  pl.BlockSpec((1, tw, B_KV_WIDTH, LANES), lambda b, i: (b, 0, 0, 0)),
                  pl.BlockSpec((1, tw, B_KV_WIDTH, LANES), lambda b, i: (b, 0, 0, 0)),
                  full(cidx.shape), full(nidx.shape), full(widx.shape),
                  pl.BlockSpec(memory_space=pltpu.SMEM)],
        out_specs=pl.BlockSpec((1, Q_BLOCK, B_WIDTH), lambda b, i: (b, i, 0)),
        out_shape=jax.ShapeDtypeStruct((nb, t, B_WIDTH), _F32),
        scratch_shapes=[pltpu.VMEM((B_HEADS, 2, Q_BLOCK, n_sel), _F32),
                        pltpu.VMEM((B_HEADS, Q_BLOCK, _NEAR), _F32),
                        pltpu.VMEM((B_HEADS, Q_BLOCK, WINDOW + Q_BLOCK), _F32)],
        compiler_params=_cparams(("arbitrary", "arbitrary")),
        name="nsa_prompt",
    )(q, gate, kcp, vcp, ksa, vs, kwa, vwa, jnp.asarray(cidx), jnp.asarray(nidx), jnp.asarray(widx), rb_flat)


def _nsa_prompt_inputs(kvcmp_t, kvs_t, kvw_t):
    nb, _, t = kvs_t.shape
    n_sel = t // SEL_BLOCK
    kvp = kvcmp_t.reshape(nb, _KV_ROWS, n_sel, 2).transpose(0, 1, 3, 2).reshape(nb, _KV_ROWS, 2 * n_sel)
    kcp = kvp[:, :B_KV_WIDTH].astype(_MXU)
    vcp = kvp[:, B_KV_WIDTH:].astype(_MXU)
    blk = (jnp.arange(n_sel)[:, None] == jnp.arange(t)[None, :] // SEL_BLOCK).astype(_MXU)
    ksa = jnp.concatenate([kvs_t[:, :B_KV_WIDTH].astype(_MXU), jnp.broadcast_to(blk, (nb, n_sel, t))], axis=1)
    pad_s = ((0, 0), (0, 0), (_FRONT_PAD, _KEY_CHUNK - _FRONT_PAD))
    pad_w = ((0, 0), (0, 0), (WINDOW, 0))
    def tiles(a, pad):
        a = jnp.pad(a, pad)
        return a.reshape(nb, a.shape[1], a.shape[2] // LANES, LANES).transpose(0, 2, 1, 3)

    vs = tiles(kvs_t[:, B_KV_WIDTH:].astype(_MXU), pad_s)
    kwa = tiles(kvw_t[:, :B_KV_WIDTH].astype(_MXU), pad_w)
    vwa = tiles(kvw_t[:, B_KV_WIDTH:].astype(_MXU), pad_w)
    return kcp, vcp, tiles(ksa, pad_s), vs, kwa, vwa


def _bias_rows(dist, rbt):
    n = jnp.maximum(dist, 0)
    nf = jnp.maximum(n, 1).astype(_F32)
    large = REL_MAX_EXACT + (jnp.log(nf / REL_MAX_EXACT) / math.log(REL_MAX_DIST / REL_MAX_EXACT)
                             * (REL_BUCKETS - REL_MAX_EXACT)).astype(jnp.int32)
    bucket = jnp.where(n < REL_MAX_EXACT, n, jnp.minimum(large, REL_BUCKETS - 1))
    out = jnp.zeros((SUBLANES, dist.shape[-1]), _F32)
    for b in range(REL_BUCKETS):
        out = jnp.where(bucket == b, rbt[:, b:b + 1], out)
    return out


def _scmp_kernel(pt_ref, *refs, n_pages, group):
    pages = refs[:group]
    q_ref, pet_ref, wt_ref, rbt_ref, oc_ref, imp_ref, kvm_ref = refs[group:]
    pg = pl.program_id(1)
    n_cmp = n_pages * (PAGE_SIZE // CMP_BLOCK)
    per = PAGE_SIZE // CMP_BLOCK * group

    @pl.when(pg == 0)
    def _():
        kvm_ref[...] = jnp.zeros_like(kvm_ref)

    x = jnp.concatenate([p[...] for p in pages], axis=-1)
    kvm_ref[...] += _split_dot(x, _pool_matrix(group * PAGE_SIZE, CMP_BLOCK, n_cmp, first=pg * per))

    @pl.when(pg == pl.num_programs(1) - 1)
    def _():
        hd = B_HEAD_DIM
        n_sel = n_cmp // 2
        past = n_pages * PAGE_SIZE
        kv = _compress_cols((kvm_ref[...] + _pe_sums_t(pet_ref)) * (1.0 / CMP_BLOCK), wt_ref)
        q8 = q_ref[0] * (hd ** -0.5)
        row = lax.broadcasted_iota(jnp.int32, (SUBLANES, 1), 0)
        head0 = row < B_GROUP
        lc = jnp.where(head0, _dot(q8, kv[0:hd]), _dot(q8, kv[hd:2 * hd]))
        blk = lax.broadcasted_iota(jnp.int32, (1, n_cmp), 1)
        dist = past - (blk * CMP_BLOCK + CMP_BLOCK - 1)
        p = _masked_softmax(jnp.where(dist >= 0, lc + _bias_rows(dist, rbt_ref[...]), NEG_INF))
        oc_ref[0] = jnp.where(head0, _dot_nt(p, kv[2 * hd:3 * hd]), _dot_nt(p, kv[3 * hd:]))
        pool = (lax.broadcasted_iota(jnp.int32, (n_cmp, n_sel), 0) // (SEL_BLOCK // CMP_BLOCK)
                == lax.broadcasted_iota(jnp.int32, (n_cmp, n_sel), 1))
        pp = _dot_exact(p, jnp.where(pool, 1.0, 0.0))
        imp0 = jnp.sum(pp[0:B_GROUP], axis=0, keepdims=True)
        imp1 = jnp.sum(pp[B_GROUP:2 * B_GROUP], axis=0, keepdims=True)
        imp_ref[0] = jnp.where(row == 0, imp0, jnp.where(row == 1, imp1, 0.0))


def _sample_cmp(page_table, cache_t, q8, pe_t, w_t, rbt, l, group=8):
    ns, n_pages = page_table.shape
    n_cmp = n_pages * (PAGE_SIZE // CMP_BLOCK)
    page_spec = lambda k: pl.BlockSpec((None, None, _KV_ROWS, PAGE_SIZE),
                                       lambda b, g, pt: (l, pt[b, g * group + k], 0, 0))
    return pl.pallas_call(
        functools.partial(_scmp_kernel, n_pages=n_pages, group=group),
        grid_spec=pltpu.PrefetchScalarGridSpec(
            num_scalar_prefetch=1,
            grid=(ns, n_pages // group),
            in_specs=[page_spec(k) for k in range(group)] + [
                pl.BlockSpec((1, SUBLANES, B_HEAD_DIM), lambda b, g, pt: (b, 0, 0)),
                pl.BlockSpec((None, 2, B_HEAD_DIM, CMP_BLOCK), lambda b, g, pt: (l, 0, 0, 0)),
                pl.BlockSpec((None, 2, B_HEAD_DIM, B_HEAD_DIM), lambda b, g, pt: (l, 0, 0, 0)),
                pl.BlockSpec((SUBLANES, REL_BUCKETS), lambda b, g, pt: (0, 0))],
            out_specs=[pl.BlockSpec((1, SUBLANES, B_HEAD_DIM), lambda b, g, pt: (b, 0, 0)),
                       pl.BlockSpec((1, SUBLANES, n_cmp // 2), lambda b, g, pt: (b, 0, 0))],
            scratch_shapes=[pltpu.VMEM((_KV_ROWS, n_cmp), _F32)]),
        out_shape=[jax.ShapeDtypeStruct((ns, SUBLANES, B_HEAD_DIM), _F32),
                   jax.ShapeDtypeStruct((ns, SUBLANES, n_cmp // 2), _F32)],
        compiler_params=_cparams(("arbitrary", "arbitrary")),
        name="nsa_sample_cmp",
    )(page_table, *([cache_t] * group), q8, pe_t, w_t, rbt)


def _stopk_kernel(imp_ref, idx_ref, *, k):
    s = imp_ref[...]
    n = s.shape[-1]
    lane = lax.broadcasted_iota(jnp.int32, s.shape, 1)
    s = jnp.where(lane == 0, FORCE_SCORE, s)

    def body(it, carry):
        s, out = carry
        m = jnp.max(s, axis=-1, keepdims=True)
        first = jnp.min(jnp.where(s == m, lane, n), axis=-1, keepdims=True)
        return jnp.where(lane == first, -jnp.inf, s), jnp.where(lane == it, first, out)

    _, out = lax.fori_loop(0, k, body, (s, jnp.zeros(s.shape, jnp.int32)))
    idx_ref[...] = out


def _sample_topk(imp2d, k):
    return pl.pallas_call(
        functools.partial(_stopk_kernel, k=k),
        out_shape=jax.ShapeDtypeStruct(imp2d.shape, jnp.int32),
        name="nsa_sample_topk",
    )(imp2d)


def _sattn_kernel(pt_ref, idx_ref, *refs, n_blk, past):
    blocks = refs[:2 * n_blk]
    q_ref, gt_ref, new_ref, oc_ref, win_ref, rbt_ref, o_ref = refs[2 * n_blk:]
    b = pl.program_id(0)
    hd = B_HEAD_DIM
    q8 = q_ref[0] * (hd ** -0.5)
    rbt = rbt_ref[...]
    row = lax.broadcasted_iota(jnp.int32, (SUBLANES, 1), 0)
    head0 = row < B_GROUP
    new = new_ref[0]
    bias0 = _bias_rows(jnp.zeros((1, 1), jnp.int32), rbt)

    def attend(s, v_of_head, k_new, v_new):
        s_new = jnp.sum(q8 * k_new, axis=-1, keepdims=True) + bias0
        m = jnp.maximum(jnp.max(s, axis=-1, keepdims=True), s_new)
        p = jnp.exp(s - m)
        p_new = jnp.exp(s_new - m)
        den = jnp.sum(p, axis=-1, keepdims=True) + p_new
        num = jnp.where(head0, _dot_nt(p, v_of_head(0)), _dot_nt(p, v_of_head(1))) + p_new * v_new
        return num / den

    def per_head(a0, a1):
        return jnp.where(head0, a0, a1)

    per_page = PAGE_SIZE // SEL_BLOCK
    t_in = lax.broadcasted_iota(jnp.int32, (1, PAGE_SIZE), 1)
    s_h, vs = [], []
    for h in range(B_KV_HEADS):
        kt = jnp.concatenate([blocks[h * n_blk + k][h * hd:(h + 1) * hd, :] for k in range(n_blk)], axis=-1)
        vs.append(jnp.concatenate(
            [blocks[h * n_blk + k][B_KV_WIDTH + h * hd:B_KV_WIDTH + (h + 1) * hd, :] for k in range(n_blk)], axis=-1))
        dist = []
        for k in range(n_blk):
            blk = idx_ref[b, h, k]
            in_blk = t_in // SEL_BLOCK == blk % per_page
            dist.append(jnp.where(in_blk, past - ((blk // per_page) * PAGE_SIZE + t_in), -1))
        dist = jnp.concatenate(dist, axis=-1)
        s_h.append(jnp.where(dist >= 0, _dot(q8, kt) + _bias_rows(dist, rbt), NEG_INF))
    ksn = new[:, 2 * B_KV_WIDTH:3 * B_KV_WIDTH]
    vsn = new[:, 3 * B_KV_WIDTH:4 * B_KV_WIDTH]
    o_s = attend(per_head(s_h[0], s_h[1]), lambda h: vs[h],
                 per_head(ksn[:, :hd], ksn[:, hd:]), per_head(vsn[:, :hd], vsn[:, hd:]))

    win = win_ref[0]
    wb = win.shape[1]
    dist = wb - lax.broadcasted_iota(jnp.int32, (1, wb), 1)
    okw = (dist < WINDOW) & (past - dist >= 0)
    lw = per_head(_dot(q8, win[0:hd]), _dot(q8, win[hd:2 * hd]))
    sw = jnp.where(okw, lw + _bias_rows(dist, rbt), NEG_INF)
    kwn = new[:, 4 * B_KV_WIDTH:5 * B_KV_WIDTH]
    vwn = new[:, 5 * B_KV_WIDTH:6 * B_KV_WIDTH]
    o_w = attend(sw, lambda h: win[B_KV_WIDTH + h * hd:B_KV_WIDTH + (h + 1) * hd],
                 per_head(kwn[:, :hd], kwn[:, hd:]), per_head(vwn[:, :hd], vwn[:, hd:]))

    gate = jax.nn.sigmoid(gt_ref[0])
    o_ref[0] = gate[:, 0:1] * oc_ref[0] + gate[:, 1:2] * o_s + gate[:, 2:3] * o_w


def _sample_attn(page_table, idx, cache_t, q8, gate8, kv_new, oc, win_t, rbt, l):
    ns, n_pages = page_table.shape
    n_blk = idx.shape[-1]
    past = n_pages * PAGE_SIZE
    per_page = PAGE_SIZE // SEL_BLOCK

    pages = jnp.take_along_axis(page_table[:, None, :], idx // per_page, axis=-1)

    def blk_spec(h, k):
        return pl.BlockSpec((None, None, _KV_ROWS, PAGE_SIZE), lambda b, pg, ix: (l, pg[b, h, k], 0, 0))

    wb = win_t.shape[3]
    return pl.pallas_call(
        functools.partial(_sattn_kernel, n_blk=n_blk, past=past),
        grid_spec=pltpu.PrefetchScalarGridSpec(
            num_scalar_prefetch=2,
            grid=(ns,),
            in_specs=[blk_spec(h, k) for h in range(B_KV_HEADS) for k in range(n_blk)] + [
                pl.BlockSpec((1, SUBLANES, B_HEAD_DIM), lambda b, pt, ix: (b, 0, 0)),
                pl.BlockSpec((1, SUBLANES, LANES), lambda b, pt, ix: (b, 0, 0)),
                pl.BlockSpec((1, 1, 6 * B_KV_WIDTH), lambda b, pt, ix: (b, 0, 0)),
                pl.BlockSpec((1, SUBLANES, B_HEAD_DIM), lambda b, pt, ix: (b, 0, 0)),
                pl.BlockSpec((None, 1, _KV_ROWS, wb), lambda b, pt, ix: (l, b, 0, 0)),
                pl.BlockSpec((SUBLANES, REL_BUCKETS), lambda b, pt, ix: (0, 0))],
            out_specs=pl.BlockSpec((1, SUBLANES, B_HEAD_DIM), lambda b, pt, ix: (b, 0, 0))),
        out_shape=jax.ShapeDtypeStruct((ns, SUBLANES, B_HEAD_DIM), _F32),
        compiler_params=_cparams(("arbitrary",)),
        name="nsa_sample_attn",
    )(pages, idx, *([cache_t] * (B_KV_HEADS * n_blk)), q8, gate8, kv_new, oc, win_t, rbt)


def _smix_kernel(uv_ref, ng_ref, wd_ref, b0_ref, xbc_ref, st_ref, cw_ref, cb_ref, dt_ref, hp_ref,
                 dsk_ref, ex_ref, oa_ref, v_ref, xdt_ref, ea_ref, y1_ref, bc_ref):
    u, v = _gelu_ln(uv_ref[...], ng_ref[...])
    v_ref[...] = v
    oa_ref[...] = u * (v * wd_ref[...] + b0_ref[...])
    conv = cb_ref[...] + xbc_ref[...] * cw_ref[C_CONV - 1:C_CONV, :]
    for k in range(C_CONV - 1):
        conv = conv + st_ref[k] * cw_ref[k:k + 1, :]
    xc = _silu(conv)
    xs = xc[:, :C_WIDTH]
    bm = xc[:, C_WIDTH:C_WIDTH + C_GROUPS * C_STATE]
    cm = xc[:, C_WIDTH + C_GROUPS * C_STATE:]
    bc_ref[...] = xc[:, C_WIDTH:]
    dt = jax.nn.softplus(dt_ref[...] + hp_ref[0:1, :])
    acum = dt * (-jnp.exp(hp_ref[1:2, :]))
    dt_rep = _dot_exact(dt, ex_ref[...])
    ea_ref[...] = jnp.exp(_dot_exact(acum, ex_ref[...]))
    xdt = dt_rep * xs
    xdt_ref[...] = xdt
    per_g = C_WIDTH // C_GROUPS
    cb = [jnp.sum(cm[:, g * C_STATE:(g + 1) * C_STATE] * bm[:, g * C_STATE:(g + 1) * C_STATE],
                  axis=-1, keepdims=True) for g in range(C_GROUPS)]
    lane = lax.broadcasted_iota(jnp.int32, xs.shape, 1)
    y1_ref[...] = jnp.where(lane < per_g, cb[0], cb[1]) * xdt + dsk_ref[...] * xs


def _sample_mix(uv, ng_row, wd_row, b0_row, xbc, st, conv_w_l, conv_b_row, dt, hp, dsk_row, expand):
    ns = uv.shape[0]
    f = lambda w: jax.ShapeDtypeStruct((ns, w), _F32)
    return pl.pallas_call(
        _smix_kernel,
        out_shape=[f(A_WIDTH), f(A_WIDTH), f(C_WIDTH), f(C_WIDTH), f(C_WIDTH), f(2 * C_GROUPS * C_STATE)],
        name="sample_gmlp_conv",
    )(uv, ng_row, wd_row, b0_row, xbc, st, conv_w_l, conv_b_row, dt, hp, dsk_row, expand)


def _sssm_kernel(h0_ref, xdt_ref, ea_ref, y1_ref, bc_ref, z_ref, ng_ref, o_ref, hout_ref):
    per_g = C_WIDTH // C_GROUPS
    h0 = h0_ref[0]
    bc = bc_ref[0]
    bm = bc[:, :C_GROUPS * C_STATE]
    cm = bc[:, C_GROUPS * C_STATE:]
    c8 = [jnp.broadcast_to(cm[:, g * C_STATE:(g + 1) * C_STATE], (SUBLANES, C_STATE)) for g in range(C_GROUPS)]
    ch = jnp.concatenate([_dot_nt(c8[g], h0[g * per_g:(g + 1) * per_g, :])[0:1] for g in range(C_GROUPS)], axis=-1)
    y = y1_ref[0] + ch * ea_ref[0]
    o_ref[0] = _rms(y * _silu(z_ref[0]), ng_ref[...])
    row = lax.broadcasted_iota(jnp.int32, (LANES, C_WIDTH), 0)
    cols = jnp.where(row == 0, xdt_ref[0], jnp.where(row == 1, ea_ref[0], 0.0)).T
    rsel = lax.broadcasted_iota(jnp.int32, (C_WIDTH, C_STATE), 0) < per_g
    b_full = jnp.where(rsel, bm[:, :C_STATE], bm[:, C_STATE:])
    hout_ref[0] = h0 * cols[:, 1:2] + cols[:, 0:1] * b_full


def _sample_ssm(h0, xdt, ea, y1, bc, z, ng_row, l):
    ns = xdt.shape[0]
    r3 = lambda a: a.reshape(ns, 1, a.shape[-1])
    row_spec = lambda w: pl.BlockSpec((1, 1, w), lambda b: (b, 0, 0))
    out, hout = pl.pallas_call(
        _sssm_kernel,
        grid=(ns,),
        in_specs=[pl.BlockSpec((None, 1, C_WIDTH, C_STATE), lambda b: (l, b, 0, 0)),
                  row_spec(C_WIDTH), row_spec(C_WIDTH), row_spec(C_WIDTH), row_spec(2 * C_GROUPS * C_STATE),
                  row_spec(C_WIDTH), pl.BlockSpec((1, C_WIDTH), lambda b: (0, 0))],
        out_specs=[row_spec(C_WIDTH), pl.BlockSpec((1, C_WIDTH, C_STATE), lambda b: (b, 0, 0))],
        out_shape=[jax.ShapeDtypeStruct((ns, 1, C_WIDTH), _F32),
                   jax.ShapeDtypeStruct((ns, C_WIDTH, C_STATE), _F32)],
        compiler_params=_cparams(("parallel",)),
        name="sample_ssm",
    )(h0, r3(xdt), r3(ea), r3(y1), r3(bc), r3(z), ng_row)
    return out.reshape(ns, C_WIDTH), hout


def _pad_cols(a, width):
    return jnp.pad(a, ((0, 0),) * (a.ndim - 1) + ((0, width - a.shape[-1]),))


def _pack_w_in(w_in):
    o = np.cumsum((0, A_WIDTH, A_WIDTH, B_WIDTH, 6 * B_KV_WIDTH, _GATE_COLS, C_WIDTH, C_CONV_DIM, C_HEADS))
    u0, q0, kv0, gate0, z0, xbc0, dt0, end = o[0], o[2], o[3], o[4], o[5], o[6], o[7], o[8]
    parts = [w_in[..., u0:kv0], w_in[..., z0:xbc0], w_in[..., xbc0:dt0],
             _pad_cols(w_in[..., gate0:z0], LANES), _pad_cols(w_in[..., dt0:end], LANES)]
    w_rows = jnp.concatenate(parts, axis=-1).astype(_MXU)
    w_kv_t = jnp.swapaxes(w_in[..., kv0:gate0], -1, -2).astype(_MXU)
    return w_rows, w_kv_t


def _head_rows(dt_bias, a_log):
    hp = jnp.zeros((DEPTH, SUBLANES, LANES), _F32)
    hp = hp.at[:, 0, :C_HEADS].set(dt_bias)
    return hp.at[:, 1, :C_HEADS].set(a_log)


def kernel(x_prompt, x_sample, cache_cmp_kv, cache_slc_kv, page_table, state_win_kv, state_conv, state_ssm,
           norm_g, ffn_w_gate, ffn_w_up, ffn_w_down, w_in, w_out, gmlp_norm_g, gmlp_w_s, gmlp_b_s,
           nsa_pe_cmp, nsa_w_cmp, rel_bias, conv_w, conv_b, dt_bias, a_log, d_skip, ssm_norm_g):
    bp, t = x_prompt.shape[:2]
    ns = x_sample.shape[0]
    n_pages = page_table.shape[1]
    n_phys = cache_cmp_kv.shape[1]
    tm_p, tm_s = 512, ns
    assert t % _KEY_CHUNK == 0 and t % tm_p == 0

    wg, wu, wd = (w.astype(_MXU) for w in (ffn_w_gate, ffn_w_up, ffn_w_down))
    w_in_p, w_kv_t = _pack_w_in(w_in)
    w_out_b = w_out.astype(_MXU)
    pe_t = jnp.swapaxes(nsa_pe_cmp, -1, -2)
    wc_t = jnp.swapaxes(nsa_w_cmp, -1, -2)
    bs_t = jnp.swapaxes(gmlp_b_s, 1, 2)
    wdiag = jnp.repeat(gmlp_w_s[:, :, 0, 0], A_HEAD_DIM, axis=-1)
    b0 = jnp.repeat(gmlp_b_s[:, :, 0], A_HEAD_DIM, axis=-1)
    hp = _head_rows(dt_bias, a_log)
    dsk = jnp.repeat(d_skip, C_HEAD_DIM, axis=-1)
    expand = (jnp.arange(LANES)[:, None] == jnp.arange(C_WIDTH)[None, :] // C_HEAD_DIM).astype(_F32)
    rb_flat = rel_bias.reshape(-1)
    rbt = _pad_cols(rel_bias, SUBLANES).T
    def chan_major(a):
        lead, tok = a.shape[:-4], a.shape[-4]
        perm = tuple(range(len(lead))) + tuple(len(lead) + k for k in (1, 2, 3, 0))
        return a.transpose(perm).reshape(*lead, _KV_ROWS, tok)

    def token_major(a_t):
        lead, tok = a_t.shape[:-2], a_t.shape[-1]
        a5 = a_t.reshape(*lead, 2, B_KV_HEADS, B_HEAD_DIM, tok)
        perm = tuple(range(len(lead))) + tuple(len(lead) + k for k in (3, 0, 1, 2))
        return a5.transpose(perm)

    cache_c = chan_major(cache_cmp_kv)
    cache_s = chan_major(cache_slc_kv)
    win_state = chan_major(state_win_kv)
    ssm_state = state_ssm.reshape(DEPTH, ns, C_WIDTH, C_STATE)

    xp = x_prompt.reshape(bp * t, D_MODEL)
    xs = x_sample.reshape(ns, D_MODEL)
    outs = [[] for _ in range(11)]
    for l in range(DEPTH):
        g = norm_g[l]
        row = lambda a: a.reshape(1, -1)
        xp = _ffn(xp, g[0:2], wg, wu, wd, l, 0, tm_p)
        xs = _ffn(xs, g[0:2], wg, wu, wd, l, 0, tm_s)

        uv, q, z, xbc, gate, dt, kvc_t, kvs_t, kvw_t = _inproj(xp.reshape(bp, t, D_MODEL), row(g[2]), w_in_p,
                                                               w_kv_t, l, tm_p)
        oa = _gmlp_prompt(uv.reshape(bp * t, 2 * A_WIDTH), row(gmlp_norm_g[l]), gmlp_w_s, bs_t, l)
        kvcmp_t = _compress_prompt(kvc_t, pe_t, wc_t, l)
        ob = _nsa_prompt(q, gate, *_nsa_prompt_inputs(kvcmp_t, kvs_t, kvw_t), rb_flat)
        oc, h_p = _mamba_prompt(z, xbc, dt, conv_w, row(conv_b[l]), hp[l], row(dsk[l]), row(ssm_norm_g[l]), l)
        xp = _outproj(xp, oa, ob.reshape(bp * t, B_WIDTH), oc.reshape(bp * t, C_WIDTH), row(g[3]), w_out_b, l, tm_p)
        wkeep = min(WINDOW, t)
        outs[0].append(token_major(kvc_t))
        outs[1].append(token_major(kvs_t))
        outs[2].append(token_major(kvw_t[:, :, t - wkeep:]))
        outs[3].append(xbc[:, t - (C_CONV - 1):])
        outs[4].append(h_p)

        uv, q, z, xbc, gate, dt, kvc_t, kvs_t, kvw_t = (
            a[0] for a in _inproj(xs.reshape(1, ns, D_MODEL), row(g[2]), w_in_p, w_kv_t, l, tm_s))
        q8 = jnp.pad(q.reshape(ns, B_HEADS, B_HEAD_DIM), ((0, 0), (0, SUBLANES - B_HEADS), (0, 0)))
        gate8 = jnp.pad(gate[:, :_GATE_COLS].reshape(ns, B_HEADS, N_BRANCH),
                        ((0, 0), (0, SUBLANES - B_HEADS), (0, LANES - N_BRANCH)))
        o_cmp, imp = _sample_cmp(page_table, cache_c, q8, pe_t, wc_t, rbt, l)
        n_sel_past = imp.shape[-1]
        idx = _sample_topk(imp.reshape(ns * SUBLANES, n_sel_past), SEL_TOPK - 1)
        idx = idx.reshape(ns, SUBLANES, n_sel_past)[:, :B_KV_HEADS, :SEL_TOPK - 1]
        kvc, kvs, kvwin = kvc_t.T, kvs_t.T, kvw_t.T
        kv_new = jnp.concatenate([kvc, kvs, kvwin], axis=-1).reshape(ns, 1, 6 * B_KV_WIDTH)
        ob8 = _sample_attn(page_table, idx, cache_s, q8, gate8, kv_new, o_cmp, win_state, rbt, l)
        ob = ob8[:, :B_HEADS].reshape(ns, B_WIDTH)
        st = jnp.swapaxes(state_conv[l], 0, 1)
        oa, v_rows, xdt, ea, y1, bc = _sample_mix(uv, row(gmlp_norm_g[l]), row(wdiag[l]), row(b0[l]), xbc, st,
                                                  conv_w[l], row(conv_b[l]), dt, hp[l], row(dsk[l]), expand)
        oc, h_s = _sample_ssm(ssm_state, xdt, ea, y1, bc, z, row(ssm_norm_g[l]), l)
        xs = _outproj(xs, oa, ob, oc, row(g[3]), w_out_b, l, tm_s)
        outs[5].append(kvc.reshape(ns, 1, 2, B_KV_HEADS, B_HEAD_DIM))
        outs[6].append(kvs.reshape(ns, 1, 2, B_KV_HEADS, B_HEAD_DIM))
        outs[7].append(token_major(jnp.concatenate([win_state[l][:, :, 1:], kvwin[:, :, None]], axis=-1)))
        outs[8].append(jnp.concatenate([state_conv[l][:, 1:], xbc[:, None]], axis=1))
        outs[9].append(h_s.reshape(ns, C_HEADS, C_HEAD_DIM, C_STATE))
        outs[10].append(v_rows.reshape(ns, 1, A_WIDTH))

        xp = _ffn(xp, g[4:6], wg, wu, wd, l, 1, tm_p)
        xs = _ffn(xs, g[4:6], wg, wu, wd, l, 1, tm_s)
    stacked = [jnp.stack(o) for o in outs]
    return (xp.reshape(bp, t, D_MODEL), xs.reshape(ns, 1, D_MODEL), *stacked)
```

```python
import functools
import math

import numpy as np
import jax
import jax.numpy as jnp
from jax import lax
from jax.experimental import pallas as pl
from jax.experimental.pallas import tpu as pltpu

D_MODEL = 1024
DEPTH = 4
PAGE_SIZE = 128
A_HEADS, A_HEAD_DIM, A_CHUNK = 4, 64, 128
A_WIDTH = A_HEADS * A_HEAD_DIM
B_HEADS, B_KV_HEADS, B_HEAD_DIM = 6, 2, 64
B_GROUP = B_HEADS // B_KV_HEADS
B_WIDTH = B_HEADS * B_HEAD_DIM
B_KV_WIDTH = B_KV_HEADS * B_HEAD_DIM
N_BRANCH = 3
CMP_BLOCK, SEL_BLOCK, SEL_TOPK, WINDOW, Q_BLOCK = 32, 64, 16, 512, 128
FORCE_SCORE = 1e4
C_HEADS, C_HEAD_DIM, C_GROUPS, C_STATE, C_CONV = 6, 64, 2, 64, 4
C_WIDTH = C_HEADS * C_HEAD_DIM
C_CONV_DIM = C_WIDTH + 2 * C_GROUPS * C_STATE
SSD_CHUNK = 128
D_FF = 2816
REL_BUCKETS, REL_MAX_EXACT, REL_MAX_DIST = 32, 16, 128
EPS = 1e-6
NEG_INF = -1e30
SOFTMAX_FLOOR = -1e20

LANES = 128
SUBLANES = 8
VMEM_LIMIT_BYTES = 56 * 1024 * 1024

_MXU = jnp.bfloat16
_F32 = jnp.float32

_GATE_COLS = N_BRANCH * B_HEADS
_SEG = (("uv", 2 * A_WIDTH), ("q", B_WIDTH), ("z", C_WIDTH), ("xbc", C_CONV_DIM), ("gate", LANES), ("dt", LANES))
_D_IN_PAD = sum(w for _, w in _SEG)
_KV_ROWS = 2 * B_KV_WIDTH
_N_KV = 3


def _dot(a, b):
    return jnp.dot(a.astype(_MXU), b.astype(_MXU), preferred_element_type=_F32)


def _dot_nt(a, b):
    return lax.dot_general(a.astype(_MXU), b.astype(_MXU), (((1,), (1,)), ((), ())),
                           preferred_element_type=_F32)


def _dot_tn(a, b):
    return lax.dot_general(a.astype(_MXU), b.astype(_MXU), (((0,), (0,)), ((), ())),
                           preferred_element_type=_F32)


def _dot_exact(a, b):
    return jnp.dot(a, b, preferred_element_type=_F32, precision=lax.Precision.HIGHEST)


def _rms(x, g):
    return x * lax.rsqrt(jnp.mean(x * x, axis=-1, keepdims=True) + EPS) * g


def _silu(x):
    return x * jax.nn.sigmoid(x)


def _cparams(sem):
    return pltpu.CompilerParams(dimension_semantics=sem, vmem_limit_bytes=VMEM_LIMIT_BYTES)


def _bucket_np(dist):
    n = np.maximum(dist, 0)
    nf = np.maximum(n, 1).astype(np.float32)
    large = REL_MAX_EXACT + (np.log(nf / np.float32(REL_MAX_EXACT))
                             / np.float32(math.log(REL_MAX_DIST / REL_MAX_EXACT))
                             * np.float32(REL_BUCKETS - REL_MAX_EXACT)).astype(np.int32)
    large = np.minimum(large, REL_BUCKETS - 1)
    return np.where(n < REL_MAX_EXACT, n, large).astype(np.int32)


def _ffn_kernel(x_ref, g_ref, wg_ref, wu_ref, wd_ref, o_ref, h_ref, acc_ref):
    f = pl.program_id(1)

    @pl.when(f == 0)
    def _():
        h_ref[...] = _rms(x_ref[...], g_ref[0:1, :]).astype(h_ref.dtype)
        acc_ref[...] = jnp.zeros_like(acc_ref)

    h = h_ref[...]
    a = _silu(_dot(h, wg_ref[...])) * _dot(h, wu_ref[...])
    acc_ref[...] += _dot(a, wd_ref[...])

    @pl.when(f == pl.num_programs(1) - 1)
    def _():
        o_ref[...] = x_ref[...] + 0.5 * _rms(acc_ref[...], g_ref[1:2, :])


def _ffn(x, g2, wg, wu, wd, l, j, tm, tf=256):
    rows = x.shape[0]
    return pl.pallas_call(
        _ffn_kernel,
        grid=(rows // tm, D_FF // tf),
        in_specs=[pl.BlockSpec((tm, D_MODEL), lambda r, f: (r, 0)),
                  pl.BlockSpec((2, D_MODEL), lambda r, f: (0, 0)),
                  pl.BlockSpec((None, None, D_MODEL, tf), lambda r, f: (l, j, 0, f)),
                  pl.BlockSpec((None, None, D_MODEL, tf), lambda r, f: (l, j, 0, f)),
                  pl.BlockSpec((None, None, tf, D_MODEL), lambda r, f: (l, j, f, 0))],
        out_specs=pl.BlockSpec((tm, D_MODEL), lambda r, f: (r, 0)),
        out_shape=jax.ShapeDtypeStruct((rows, D_MODEL), _F32),
        scratch_shapes=[pltpu.VMEM((tm, D_MODEL), _MXU), pltpu.VMEM((tm, D_MODEL), _F32)],
        compiler_params=_cparams(("parallel", "arbitrary")),
        name="half_ffn",
    )(x, g2, wg, wu, wd)


def _inproj_kernel(x_ref, g_ref, w_ref, wkv_ref, *o_refs):
    h = _rms(x_ref[0], g_ref[...]).astype(_MXU)
    off = 0
    for (_, width), o_ref in zip(_SEG, o_refs):
        o_ref[0] = _dot(h, w_ref[:, off:off + width])
        off += width
    for k, o_ref in enumerate(o_refs[len(_SEG):]):
        o_ref[0] = _dot_nt(wkv_ref[k * _KV_ROWS:(k + 1) * _KV_ROWS, :], h)


def _inproj(x, g_row, w_in_p, w_kv_t, l, tm):
    nb, t = x.shape[:2]
    return pl.pallas_call(
        _inproj_kernel,
        grid=(nb, t // tm),
        in_specs=[pl.BlockSpec((1, tm, D_MODEL), lambda b, r: (b, r, 0)),
                  pl.BlockSpec((1, D_MODEL), lambda b, r: (0, 0)),
                  pl.BlockSpec((None, D_MODEL, _D_IN_PAD), lambda b, r: (l, 0, 0)),
                  pl.BlockSpec((None, _N_KV * _KV_ROWS, D_MODEL), lambda b, r: (l, 0, 0))],
        out_specs=[pl.BlockSpec((1, tm, w), lambda b, r: (b, r, 0)) for _, w in _SEG]
        + [pl.BlockSpec((1, _KV_ROWS, tm), lambda b, r: (b, 0, r))] * _N_KV,
        out_shape=[jax.ShapeDtypeStruct((nb, t, w), _F32) for _, w in _SEG]
        + [jax.ShapeDtypeStruct((nb, _KV_ROWS, t), _F32)] * _N_KV,
        compiler_params=_cparams(("parallel", "parallel")),
        name="in_proj",
    )(x, g_row, w_in_p, w_kv_t)


def _outproj_kernel(x_ref, oa_ref, ob_ref, oc_ref, g_ref, w_ref, o_ref):
    y = (_dot(oa_ref[...], w_ref[0:A_WIDTH, :])
         + _dot(ob_ref[...], w_ref[A_WIDTH:A_WIDTH + B_WIDTH, :])
         + _dot(oc_ref[...], w_ref[A_WIDTH + B_WIDTH:, :]))
    o_ref[...] = x_ref[...] + _rms(y, g_ref[...])


def _outproj(x, oa, ob, oc, g_row, w_out, l, tm):
    rows = x.shape[0]
    return pl.pallas_call(
        _outproj_kernel,
        grid=(rows // tm,),
        in_specs=[pl.BlockSpec((tm, D_MODEL), lambda r: (r, 0)),
                  pl.BlockSpec((tm, A_WIDTH), lambda r: (r, 0)),
                  pl.BlockSpec((tm, B_WIDTH), lambda r: (r, 0)),
                  pl.BlockSpec((tm, C_WIDTH), lambda r: (r, 0)),
                  pl.BlockSpec((1, D_MODEL), lambda r: (0, 0)),
                  pl.BlockSpec((None, D_MODEL, D_MODEL), lambda r: (l, 0, 0))],
        out_specs=pl.BlockSpec((tm, D_MODEL), lambda r: (r, 0)),
        out_shape=jax.ShapeDtypeStruct((rows, D_MODEL), _F32),
        compiler_params=_cparams(("parallel",)),
        name="out_proj",
    )(x, oa, ob, oc, g_row, w_out)


def _gelu_ln(uv, ng):
    u = jax.nn.gelu(uv[:, :A_WIDTH])
    v = jax.nn.gelu(uv[:, A_WIDTH:])
    mu = jnp.mean(v, axis=-1, keepdims=True)
    var = jnp.mean(jnp.square(v - mu), axis=-1, keepdims=True)
    return u, (v - mu) * lax.rsqrt(var + EPS) * ng


def _gmlp_kernel(uv_ref, ng_ref, ws_ref, bs_ref, o_ref, *, chunks):
    row = lax.broadcasted_iota(jnp.int32, (A_CHUNK, A_CHUNK), 0)
    col = lax.broadcasted_iota(jnp.int32, (A_CHUNK, A_CHUNK), 1)
    ws = [jnp.where(col <= row, ws_ref[h], 0.0).astype(_MXU) for h in range(A_HEADS)]
    for c in range(chunks):
        u, v = _gelu_ln(uv_ref[c * A_CHUNK:(c + 1) * A_CHUNK, :], ng_ref[...])
        sg = [_dot(ws[h], v[:, h * A_HEAD_DIM:(h + 1) * A_HEAD_DIM]) + bs_ref[:, h:h + 1]
              for h in range(A_HEADS)]
        o_ref[c * A_CHUNK:(c + 1) * A_CHUNK, :] = u * jnp.concatenate(sg, axis=-1)


def _gmlp_prompt(uv, ng_row, ws, bs_t, l, chunks=4):
    rows = uv.shape[0]
    tm = chunks * A_CHUNK
    return pl.pallas_call(
        functools.partial(_gmlp_kernel, chunks=chunks),
        grid=(rows // tm,),
        in_specs=[pl.BlockSpec((tm, 2 * A_WIDTH), lambda r: (r, 0)),
                  pl.BlockSpec((1, A_WIDTH), lambda r: (0, 0)),
                  pl.BlockSpec((None, A_HEADS, A_CHUNK, A_CHUNK), lambda r: (l, 0, 0, 0)),
                  pl.BlockSpec((None, A_CHUNK, A_HEADS), lambda r: (l, 0, 0))],
        out_specs=pl.BlockSpec((tm, A_WIDTH), lambda r: (r, 0)),
        out_shape=jax.ShapeDtypeStruct((rows, A_WIDTH), _F32),
        compiler_params=_cparams(("parallel",)),
        name="gmlp_prompt",
    )(uv, ng_row, ws, bs_t)


def _mamba_kernel(z_ref, xbc_ref, dt_ref, cw_ref, cb_ref, hp_ref, dsk_ref, ng_ref,
                  o_ref, hout_ref, xp_ref, hs_ref):
    t = pl.program_id(1)
    L = SSD_CHUNK
    hist = SUBLANES

    @pl.when(t == 0)
    def _():
        xp_ref[0:hist, :] = jnp.zeros((hist, C_CONV_DIM), _F32)
        hs_ref[...] = jnp.zeros_like(hs_ref)

    xp_ref[hist:hist + L, :] = xbc_ref[0]
    conv = cb_ref[...]
    for k in range(C_CONV):
        conv = conv + xp_ref[pl.ds(hist - (C_CONV - 1) + k, L), :] * cw_ref[k:k + 1, :]
    xp_ref[0:hist, :] = xp_ref[L:L + hist, :]
    xc = _silu(conv)
    xs = xc[:, :C_WIDTH]
    bm = xc[:, C_WIDTH:C_WIDTH + C_GROUPS * C_STATE]
    cm = xc[:, C_WIDTH + C_GROUPS * C_STATE:]

    dt = jax.nn.softplus(dt_ref[0] + hp_ref[0:1, :])
    a_row = -jnp.exp(hp_ref[1:2, :])
    row = lax.broadcasted_iota(jnp.int32, (L, L), 0)
    col = lax.broadcasted_iota(jnp.int32, (L, L), 1)
    causal = col <= row
    acum = _dot_exact(jnp.where(causal, 1.0, 0.0), dt * a_row)
    acum_t = acum.T
    dt_t = dt.T
    cb = [_dot_nt(cm[:, g * C_STATE:(g + 1) * C_STATE], bm[:, g * C_STATE:(g + 1) * C_STATE])
          for g in range(C_GROUPS)]
    ys = []
    for h in range(C_HEADS):
        g = h // (C_HEADS // C_GROUPS)
        ac_col = acum[:, h:h + 1]
        seg = ac_col - acum_t[h:h + 1, :]
        decay = jnp.where(causal, jnp.exp(jnp.where(causal, seg, 0.0)), 0.0)
        scores = cb[g] * decay * dt_t[h:h + 1, :]
        x_h = xs[:, h * C_HEAD_DIM:(h + 1) * C_HEAD_DIM]
        b_g = bm[:, g * C_STATE:(g + 1) * C_STATE]
        c_g = cm[:, g * C_STATE:(g + 1) * C_STATE]
        hs = hs_ref[h]
        ys.append(_dot(scores, x_h) + _dot_nt(c_g, hs) * jnp.exp(ac_col))
        ac_last = acum[L - 1:L, h:h + 1]
        w_end = jnp.exp(ac_last - ac_col) * dt[:, h:h + 1]
        hs_ref[h] = hs * jnp.exp(ac_last) + _dot_tn(x_h * w_end, b_g)
    y = jnp.concatenate(ys, axis=-1) + dsk_ref[...] * xs
    o_ref[0] = _rms(y * _silu(z_ref[0]), ng_ref[...])

    @pl.when(t == pl.num_programs(1) - 1)
    def _():
        hout_ref[0] = hs_ref[...]


def _mamba_prompt(z, xbc, dt, conv_w, conv_b_row, hp, dsk_row, ng_row, l):
    nb, t = z.shape[:2]
    L = SSD_CHUNK
    return pl.pallas_call(
        _mamba_kernel,
        grid=(nb, t // L),
        in_specs=[pl.BlockSpec((1, L, C_WIDTH), lambda b, c: (b, c, 0)),
                  pl.BlockSpec((1, L, C_CONV_DIM), lambda b, c: (b, c, 0)),
                  pl.BlockSpec((1, L, LANES), lambda b, c: (b, c, 0)),
                  pl.BlockSpec((None, C_CONV, C_CONV_DIM), lambda b, c: (l, 0, 0)),
                  pl.BlockSpec((1, C_CONV_DIM), lambda b, c: (0, 0)),
                  pl.BlockSpec((SUBLANES, LANES), lambda b, c: (0, 0)),
                  pl.BlockSpec((1, C_WIDTH), lambda b, c: (0, 0)),
                  pl.BlockSpec((1, C_WIDTH), lambda b, c: (0, 0))],
        out_specs=[pl.BlockSpec((1, L, C_WIDTH), lambda b, c: (b, c, 0)),
                   pl.BlockSpec((1, C_HEADS, C_HEAD_DIM, C_STATE), lambda b, c: (b, 0, 0, 0))],
        out_shape=[jax.ShapeDtypeStruct((nb, t, C_WIDTH), _F32),
                   jax.ShapeDtypeStruct((nb, C_HEADS, C_HEAD_DIM, C_STATE), _F32)],
        scratch_shapes=[pltpu.VMEM((L + 2 * SUBLANES, C_CONV_DIM), _F32),
                        pltpu.VMEM((C_HEADS, C_HEAD_DIM, C_STATE), _F32)],
        compiler_params=_cparams(("arbitrary", "arbitrary")),
        name="mamba_prompt",
    )(z, xbc, dt, conv_w, conv_b_row, hp, dsk_row, ng_row)


def _split_dot(x, p):
    hi = x.astype(_MXU)
    lo = (x - hi.astype(_F32)).astype(_MXU)
    return jnp.dot(hi, p, preferred_element_type=_F32) + jnp.dot(lo, p, preferred_element_type=_F32)


def _pool_matrix(n_tok, per, n_out, first=0):
    r = lax.broadcasted_iota(jnp.int32, (n_tok, n_out), 0)
    c = lax.broadcasted_iota(jnp.int32, (n_tok, n_out), 1)
    return jnp.where(c == first + r // per, 1.0, 0.0).astype(_MXU)


def _pe_sums_t(pet_ref):
    pk = jnp.sum(pet_ref[0], axis=-1, keepdims=True)
    pv = jnp.sum(pet_ref[1], axis=-1, keepdims=True)
    return jnp.concatenate([pk, pk, pv, pv], axis=0)


def _compress_cols(m, wt_ref):
    hd = B_HEAD_DIM
    return jnp.concatenate([_dot(wt_ref[0], m[0:hd]), _dot(wt_ref[0], m[hd:2 * hd]),
                            _dot(wt_ref[1], m[2 * hd:3 * hd]), _dot(wt_ref[1], m[3 * hd:])], axis=0)


def _compress_kernel(kv_ref, pet_ref, wt_ref, o_ref, *, chunk):
    t = kv_ref.shape[-1]
    pool = _pool_matrix(chunk, CMP_BLOCK, chunk // CMP_BLOCK)
    sums = jnp.concatenate([_split_dot(kv_ref[0, :, c * chunk:(c + 1) * chunk], pool) for c in range(t // chunk)],
                           axis=-1)
    o_ref[0] = _compress_cols((sums + _pe_sums_t(pet_ref)) * (1.0 / CMP_BLOCK), wt_ref)


def _compress_prompt(kvc_t, pe_t, w_t, l, chunk=2048):
    nb, _, t = kvc_t.shape
    return pl.pallas_call(
        functools.partial(_compress_kernel, chunk=min(chunk, t)),
        grid=(nb,),
        in_specs=[pl.BlockSpec((1, _KV_ROWS, t), lambda b: (b, 0, 0)),
                  pl.BlockSpec((None, 2, B_HEAD_DIM, CMP_BLOCK), lambda b: (l, 0, 0, 0)),
                  pl.BlockSpec((None, 2, B_HEAD_DIM, B_HEAD_DIM), lambda b: (l, 0, 0, 0))],
        out_specs=pl.BlockSpec((1, _KV_ROWS, t // CMP_BLOCK), lambda b: (b, 0, 0)),
        out_shape=jax.ShapeDtypeStruct((nb, _KV_ROWS, t // CMP_BLOCK), _F32),
        compiler_params=_cparams(("parallel",)),
        name="nsa_compress_prompt",
    )(kvc_t, pe_t, w_t)


_KEY_CHUNK = 512
_FRONT_PAD = Q_BLOCK
_NEAR = 2 * Q_BLOCK


def _nsa_tables(n_sel):
    r = np.arange(Q_BLOCK)[:, None]
    cmp_idx = np.full((2, Q_BLOCK, n_sel), REL_BUCKETS - 1, np.int32)
    for par in range(2):
        for u in (-2, -1, 0, 1):
            dist = r[:, 0] - (CMP_BLOCK - 1) - CMP_BLOCK * (2 * u + par)
            cmp_idx[par, :, u % n_sel] = _bucket_np(dist)
    c = np.arange(_NEAR)[None, :]
    dist = Q_BLOCK + r - c
    near_idx = np.where(dist >= 0, _bucket_np(dist), -1).astype(np.int32)
    c = np.arange(WINDOW + Q_BLOCK)[None, :]
    dist = r + WINDOW - c
    win_idx = np.where((dist >= 0) & (dist < WINDOW), _bucket_np(dist), -1).astype(np.int32)
    return cmp_idx, near_idx, win_idx


def _fill_bias(idx, rb_ref, hg, rel_to_last):
    base = rb_ref[(REL_BUCKETS - 1) * B_HEADS + hg] if rel_to_last else 0.0
    tile = jnp.where(idx < 0, NEG_INF, 0.0)
    for b in range(REL_BUCKETS):
        tile = jnp.where(idx == b, rb_ref[b * B_HEADS + hg] - base, tile)
    return tile


def _topk_mask(score, k):
    n = score.shape[-1]
    lane = lax.broadcasted_iota(jnp.int32, score.shape, 1)
    s = score
    for _ in range(k):
        m = jnp.max(s, axis=-1, keepdims=True)
        first = jnp.min(jnp.where(s == m, lane, n), axis=-1, keepdims=True)
        s = jnp.where(lane == first, -jnp.inf, s)
    return s == -jnp.inf


def _masked_softmax(l):
    m = jnp.maximum(jnp.max(l, axis=-1, keepdims=True), SOFTMAX_FLOOR)
    e = jnp.exp(l - m)
    return e / jnp.maximum(jnp.sum(e, axis=-1, keepdims=True), 1e-20)


def _online_step(s, vt, carry):
    m_run, l_run, acc = carry
    m_new = jnp.maximum(m_run, jnp.max(s, axis=-1, keepdims=True))
    alpha = jnp.exp(m_run - m_new)
    p = jnp.exp(s - m_new)
    return (m_new, alpha * l_run + jnp.sum(p, axis=-1, keepdims=True), alpha * acc + _dot_nt(p, vt))


def _key_tiles(ref, first, n, rows=slice(None)):
    return jnp.concatenate([ref[0, first + j, rows, :] for j in range(n)], axis=-1)


def _nsa_kernel(q_ref, gt_ref, kc_ref, vc_ref, ksa_ref, vs_ref, kwa_ref, vwa_ref,
                cidx_ref, nidx_ref, widx_ref, rb_ref, o_ref, bc_ref, bn_ref, bw_ref, *, n_sel):
    i = pl.program_id(1)
    hd, G = B_HEAD_DIM, B_GROUP
    QB = Q_BLOCK

    @pl.when((pl.program_id(0) == 0) & (i == 0))
    def _():
        for hg in range(B_HEADS):
            for par in range(2):
                bc_ref[hg, par] = _fill_bias(cidx_ref[par], rb_ref, hg, False)
            bn_ref[hg] = _fill_bias(nidx_ref[...], rb_ref, hg, True)
            bw_ref[hg] = _fill_bias(widx_ref[...], rb_ref, hg, False)

    q = q_ref[0] * (hd ** -0.5)
    gate = jax.nn.sigmoid(gt_ref[0])
    r_col = lax.broadcasted_iota(jnp.int32, (QB, 1), 0)
    qpos = i * QB + r_col
    n_lane = lax.broadcasted_iota(jnp.int32, (QB, n_sel), 1)
    cur = qpos // SEL_BLOCK
    vis = [CMP_BLOCK * (2 * n_lane + par) + (CMP_BLOCK - 1) <= qpos for par in range(2)]
    vis3 = jnp.concatenate([jnp.concatenate(vis, axis=-1)] * G, axis=0)
    c_near = lax.broadcasted_iota(jnp.int32, (QB, _NEAR), 1)
    near_ok = jnp.concatenate([c_near >= _FRONT_PAD - i * QB] * G, axis=0)
    c_win = lax.broadcasted_iota(jnp.int32, (QB, WINDOW + QB), 1)
    win_ok = jnp.concatenate([c_win >= WINDOW - i * QB] * G, axis=0)
    n_main = (jnp.maximum(i - 1, 0) * QB + _KEY_CHUNK - 1) // _KEY_CHUNK
    zeros_h = jnp.zeros((G * QB, hd), _F32)

    heads = range(B_KV_HEADS)
    hrows = [slice(h * hd, (h + 1) * hd) for h in heads]
    q3 = [jnp.concatenate([q[:, (h * G + g) * hd:(h * G + g + 1) * hd] for g in range(G)], axis=0) for h in heads]
    q3h = [jnp.concatenate([q3[h], zeros_h] if h == 0 else [zeros_h, q3[h]], axis=-1) for h in heads]
    t0 = i * (QB // LANES)

    forced = (n_lane == cur) | (n_lane == 0)
    started = n_lane <= cur
    o_c, imp = [], []
    for h in heads:
        lc = _dot(q3[h], kc_ref[0, hrows[h], :])
        bias_c = jnp.concatenate(
            [jnp.concatenate([pltpu.roll(bc_ref[h * G + g, par], 2 * i, 1) for par in range(2)], axis=-1)
             for g in range(G)], axis=0)
        p_c = _masked_softmax(jnp.where(vis3, lc + bias_c, NEG_INF))
        o_c.append(_dot_nt(p_c, vc_ref[0, hrows[h], :]))
        imp_h = sum(p_c[g * QB:(g + 1) * QB, :n_sel] + p_c[g * QB:(g + 1) * QB, n_sel:] for g in range(G))
        imp.append(jnp.where(forced, FORCE_SCORE, jnp.where(started, imp_h, -1.0)))

    chosen = _topk_mask(jnp.concatenate(imp, axis=0), min(SEL_TOPK, n_sel))
    n_win = (WINDOW + QB) // LANES
    kw = _key_tiles(kwa_ref, t0, n_win)
    o_w = []
    for h in heads:
        bias_w = jnp.concatenate([bw_ref[h * G + g] for g in range(G)], axis=0)
        p_w = _masked_softmax(jnp.where(win_ok, _dot(q3h[h], kw) + bias_w, NEG_INF))
        o_w.append(_dot_nt(p_w, _key_tiles(vwa_ref, t0, n_win, hrows[h])))

    qa_main, qa_near = [], []
    for h in heads:
        allowed = chosen[h * QB:(h + 1) * QB] & started
        m_main = jnp.where(allowed & (n_lane < 2 * i - 2), 0.0, NEG_INF)
        m_near = jnp.where(allowed, 0.0, NEG_INF)
        qa_main.append(jnp.concatenate([q3h[h], jnp.concatenate([m_main] * G, axis=0)], axis=-1).astype(_MXU))
        qa_near.append(jnp.concatenate([q3h[h], jnp.concatenate([m_near] * G, axis=0)], axis=-1).astype(_MXU))

    per_chunk = _KEY_CHUNK // LANES

    def main_body(c, carry):
        tc = _FRONT_PAD // LANES + c * per_chunk
        ks = _key_tiles(ksa_ref, tc, per_chunk)
        return tuple(_online_step(_dot(qa_main[h], ks), _key_tiles(vs_ref, tc, per_chunk, hrows[h]), carry[h])
                     for h in heads)

    init = (jnp.full((G * QB, 1), SOFTMAX_FLOOR, _F32), jnp.zeros((G * QB, 1), _F32), zeros_h)
    carry = lax.fori_loop(0, n_main, main_body, (init,) * B_KV_HEADS)
    ks = _key_tiles(ksa_ref, t0, _NEAR // LANES)
    outs = []
    for h in heads:
        corr = jnp.concatenate([bn_ref[h * G + g] for g in range(G)], axis=0)
        s = jnp.where(near_ok, _dot(qa_near[h], ks) + corr, NEG_INF)
        _, l_s, acc_s = _online_step(s, _key_tiles(vs_ref, t0, _NEAR // LANES, hrows[h]), carry[h])
        o_s = acc_s / l_s
        for g in range(G):
            k0 = (h * G + g) * N_BRANCH
            rows = slice(g * QB, (g + 1) * QB)
            outs.append(gate[:, k0:k0 + 1] * o_c[h][rows] + gate[:, k0 + 1:k0 + 2] * o_s[rows]
                        + gate[:, k0 + 2:k0 + 3] * o_w[h][rows])
    o_ref[0] = jnp.concatenate(outs, axis=-1)


def _nsa_prompt(q, gate, kcp, vcp, ksa, vs, kwa, vwa, rb_flat):
    nb, t = q.shape[:2]
    n_sel = t // SEL_BLOCK
    cidx, nidx, widx = _nsa_tables(n_sel)
    tp, tw = ksa.shape[1], kwa.shape[1]
    full = lambda shape: pl.BlockSpec(shape, lambda b, i: (0,) * len(shape))
    return pl.pallas_call(
        functools.partial(_nsa_kernel, n_sel=n_sel),
        grid=(nb, t // Q_BLOCK),
        in_specs=[pl.BlockSpec((1, Q_BLOCK, B_WIDTH), lambda b, i: (b, i, 0)),
                  pl.BlockSpec((1, Q_BLOCK, LANES), lambda b, i: (b, i, 0)),
                  pl.BlockSpec((1, B_KV_WIDTH, 2 * n_sel), lambda b, i: (b, 0, 0)),
                  pl.BlockSpec((1, B_KV_WIDTH, 2 * n_sel), lambda b, i: (b, 0, 0)),
                  pl.BlockSpec((1, tp, B_KV_WIDTH + n_sel, LANES), lambda b, i: (b, 0, 0, 0)),
                  pl.BlockSpec((1, tp, B_KV_WIDTH, LANES), lambda b, i: (b, 0, 0, 0)),
                  pl.BlockSpec((1, tw, B_KV_WIDTH, LANES), lambda b, i: (b, 0, 0, 0)),
                  pl.BlockSpec((1, tw, B_KV_WIDTH, LANES), lambda b, i: (b, 0, 0, 0)),
                  full(cidx.shape), full(nidx.shape), full(widx.shape),
                  pl.BlockSpec(memory_space=pltpu.SMEM)],
        out_specs=pl.BlockSpec((1, Q_BLOCK, B_WIDTH), lambda b, i: (b, i, 0)),
        out_shape=jax.ShapeDtypeStruct((nb, t, B_WIDTH), _F32),
        scratch_shapes=[pltpu.VMEM((B_HEADS, 2, Q_BLOCK, n_sel), _F32),
                        pltpu.VMEM((B_HEADS, Q_BLOCK, _NEAR), _F32),
                        pltpu.VMEM((B_HEADS, Q_BLOCK, WINDOW + Q_BLOCK), _F32)],
        compiler_params=_cparams(("arbitrary", "arbitrary")),
        name="nsa_prompt",
    )(q, gate, kcp, vcp, ksa, vs, kwa, vwa, jnp.asarray(cidx), jnp.asarray(nidx), jnp.asarray(widx), rb_flat)


def _nsa_prompt_inputs(kvcmp_t, kvs_t, kvw_t):
    nb, _, t = kvs_t.shape
    n_sel = t // SEL_BLOCK
    kvp = kvcmp_t.reshape(nb, _KV_ROWS, n_sel, 2).transpose(0, 1, 3, 2).reshape(nb, _KV_ROWS, 2 * n_sel)
    kcp = kvp[:, :B_KV_WIDTH].astype(_MXU)
    vcp = kvp[:, B_KV_WIDTH:].astype(_MXU)
    blk = (jnp.arange(n_sel)[:, None] == jnp.arange(t)[None, :] // SEL_BLOCK).astype(_MXU)
    ksa = jnp.concatenate([kvs_t[:, :B_KV_WIDTH].astype(_MXU), jnp.broadcast_to(blk, (nb, n_sel, t))], axis=1)
    pad_s = ((0, 0), (0, 0), (_FRONT_PAD, _KEY_CHUNK - _FRONT_PAD))
    pad_w = ((0, 0), (0, 0), (WINDOW, 0))
    def tiles(a, pad):
        a = jnp.pad(a, pad)
        return a.reshape(nb, a.shape[1], a.shape[2] // LANES, LANES).transpose(0, 2, 1, 3)

    vs = tiles(kvs_t[:, B_KV_WIDTH:].astype(_MXU), pad_s)
    kwa = tiles(kvw_t[:, :B_KV_WIDTH].astype(_MXU), pad_w)
    vwa = tiles(kvw_t[:, B_KV_WIDTH:].astype(_MXU), pad_w)
    return kcp, vcp, tiles(ksa, pad_s), vs, kwa, vwa


def _bias_rows(dist, rbt):
    n = jnp.maximum(dist, 0)
    nf = jnp.maximum(n, 1).astype(_F32)
    large = REL_MAX_EXACT + (jnp.log(nf / REL_MAX_EXACT) / math.log(REL_MAX_DIST / REL_MAX_EXACT)
                             * (REL_BUCKETS - REL_MAX_EXACT)).astype(jnp.int32)
    bucket = jnp.where(n < REL_MAX_EXACT, n, jnp.minimum(large, REL_BUCKETS - 1))
    out = jnp.zeros((SUBLANES, dist.shape[-1]), _F32)
    for b in range(REL_BUCKETS):
        out = jnp.where(bucket == b, rbt[:, b:b + 1], out)
    return out


def _scmp_kernel(pt_ref, *refs, n_pages, group):
    pages = refs[:group]
    q_ref, pet_ref, wt_ref, rbt_ref, oc_ref, imp_ref, kvm_ref = refs[group:]
    pg = pl.program_id(1)
    n_cmp = n_pages * (PAGE_SIZE // CMP_BLOCK)
    per = PAGE_SIZE // CMP_BLOCK * group

    @pl.when(pg == 0)
    def _():
        kvm_ref[...] = jnp.zeros_like(kvm_ref)

    x = jnp.concatenate([p[...] for p in pages], axis=-1)
    kvm_ref[...] += _split_dot(x, _pool_matrix(group * PAGE_SIZE, CMP_BLOCK, n_cmp, first=pg * per))

    @pl.when(pg == pl.num_programs(1) - 1)
    def _():
        hd = B_HEAD_DIM
        n_sel = n_cmp // 2
        past = n_pages * PAGE_SIZE
        kv = _compress_cols((kvm_ref[...] + _pe_sums_t(pet_ref)) * (1.0 / CMP_BLOCK), wt_ref)
        q8 = q_ref[0] * (hd ** -0.5)
        row = lax.broadcasted_iota(jnp.int32, (SUBLANES, 1), 0)
        head0 = row < B_GROUP
        lc = jnp.where(head0, _dot(q8, kv[0:hd]), _dot(q8, kv[hd:2 * hd]))
        blk = lax.broadcasted_iota(jnp.int32, (1, n_cmp), 1)
        dist = past - (blk * CMP_BLOCK + CMP_BLOCK - 1)
        p = _masked_softmax(jnp.where(dist >= 0, lc + _bias_rows(dist, rbt_ref[...]), NEG_INF))
        oc_ref[0] = jnp.where(head0, _dot_nt(p, kv[2 * hd:3 * hd]), _dot_nt(p, kv[3 * hd:]))
        pool = (lax.broadcasted_iota(jnp.int32, (n_cmp, n_sel), 0) // (SEL_BLOCK // CMP_BLOCK)
                == lax.broadcasted_iota(jnp.int32, (n_cmp, n_sel), 1))
        pp = _dot_exact(p, jnp.where(pool, 1.0, 0.0))
        imp0 = jnp.sum(pp[0:B_GROUP], axis=0, keepdims=True)
        imp1 = jnp.sum(pp[B_GROUP:2 * B_GROUP], axis=0, keepdims=True)
        imp_ref[0] = jnp.where(row == 0, imp0, jnp.where(row == 1, imp1, 0.0))


def _sample_cmp(page_table, cache_t, q8, pe_t, w_t, rbt, l, group=8):
    ns, n_pages = page_table.shape
    n_cmp = n_pages * (PAGE_SIZE // CMP_BLOCK)
    page_spec = lambda k: pl.BlockSpec((None, None, _KV_ROWS, PAGE_SIZE),
                                       lambda b, g, pt: (l, pt[b, g * group + k], 0, 0))
    return pl.pallas_call(
        functools.partial(_scmp_kernel, n_pages=n_pages, group=group),
        grid_spec=pltpu.PrefetchScalarGridSpec(
            num_scalar_prefetch=1,
            grid=(ns, n_pages // group),
            in_specs=[page_spec(k) for k in range(group)] + [
                pl.BlockSpec((1, SUBLANES, B_HEAD_DIM), lambda b, g, pt: (b, 0, 0)),
                pl.BlockSpec((None, 2, B_HEAD_DIM, CMP_BLOCK), lambda b, g, pt: (l, 0, 0, 0)),
                pl.BlockSpec((None, 2, B_HEAD_DIM, B_HEAD_DIM), lambda b, g, pt: (l, 0, 0, 0)),
                pl.BlockSpec((SUBLANES, REL_BUCKETS), lambda b, g, pt: (0, 0))],
            out_specs=[pl.BlockSpec((1, SUBLANES, B_HEAD_DIM), lambda b, g, pt: (b, 0, 0)),
                       pl.BlockSpec((1, SUBLANES, n_cmp // 2), lambda b, g, pt: (b, 0, 0))],
            scratch_shapes=[pltpu.VMEM((_KV_ROWS, n_cmp), _F32)]),
        out_shape=[jax.ShapeDtypeStruct((ns, SUBLANES, B_HEAD_DIM), _F32),
                   jax.ShapeDtypeStruct((ns, SUBLANES, n_cmp // 2), _F32)],
        compiler_params=_cparams(("arbitrary", "arbitrary")),
        name="nsa_sample_cmp",
    )(page_table, *([cache_t] * group), q8, pe_t, w_t, rbt)


def _stopk_kernel(imp_ref, idx_ref, *, k):
    s = imp_ref[...]
    n = s.shape[-1]
    lane = lax.broadcasted_iota(jnp.int32, s.shape, 1)
    s = jnp.where(lane == 0, FORCE_SCORE, s)

    def body(it, carry):
        s, out = carry
        m = jnp.max(s, axis=-1, keepdims=True)
        first = jnp.min(jnp.where(s == m, lane, n), axis=-1, keepdims=True)
        return jnp.where(lane == first, -jnp.inf, s), jnp.where(lane == it, first, out)

    _, out = lax.fori_loop(0, k, body, (s, jnp.zeros(s.shape, jnp.int32)))
    idx_ref[...] = out


def _sample_topk(imp2d, k):
    return pl.pallas_call(
        functools.partial(_stopk_kernel, k=k),
        out_shape=jax.ShapeDtypeStruct(imp2d.shape, jnp.int32),
        name="nsa_sample_topk",
    )(imp2d)


def _sattn_kernel(pt_ref, idx_ref, *refs, n_blk, past):
    blocks = refs[:2 * n_blk]
    q_ref, gt_ref, new_ref, oc_ref, win_ref, rbt_ref, o_ref = refs[2 * n_blk:]
    b = pl.program_id(0)
    hd = B_HEAD_DIM
    q8 = q_ref[0] * (hd ** -0.5)
    rbt = rbt_ref[...]
    row = lax.broadcasted_iota(jnp.int32, (SUBLANES, 1), 0)
    head0 = row < B_GROUP
    new = new_ref[0]
    bias0 = _bias_rows(jnp.zeros((1, 1), jnp.int32), rbt)

    def attend(s, v_of_head, k_new, v_new):
        s_new = jnp.sum(q8 * k_new, axis=-1, keepdims=True) + bias0
        m = jnp.maximum(jnp.max(s, axis=-1, keepdims=True), s_new)
        p = jnp.exp(s - m)
        p_new = jnp.exp(s_new - m)
        den = jnp.sum(p, axis=-1, keepdims=True) + p_new
        num = jnp.where(head0, _dot_nt(p, v_of_head(0)), _dot_nt(p, v_of_head(1))) + p_new * v_new
        return num / den

    def per_head(a0, a1):
        return jnp.where(head0, a0, a1)

    per_page = PAGE_SIZE // SEL_BLOCK
    t_in = lax.broadcasted_iota(jnp.int32, (1, PAGE_SIZE), 1)
    s_h, vs = [], []
    for h in range(B_KV_HEADS):
        kt = jnp.concatenate([blocks[h * n_blk + k][h * hd:(h + 1) * hd, :] for k in range(n_blk)], axis=-1)
        vs.append(jnp.concatenate(
            [blocks[h * n_blk + k][B_KV_WIDTH + h * hd:B_KV_WIDTH + (h + 1) * hd, :] for k in range(n_blk)], axis=-1))
        dist = []
        for k in range(n_blk):
            blk = idx_ref[b, h, k]
            in_blk = t_in // SEL_BLOCK == blk % per_page
            dist.append(jnp.where(in_blk, past - ((blk // per_page) * PAGE_SIZE + t_in), -1))
        dist = jnp.concatenate(dist, axis=-1)
        s_h.append(jnp.where(dist >= 0, _dot(q8, kt) + _bias_rows(dist, rbt), NEG_INF))
    ksn = new[:, 2 * B_KV_WIDTH:3 * B_KV_WIDTH]
    vsn = new[:, 3 * B_KV_WIDTH:4 * B_KV_WIDTH]
    o_s = attend(per_head(s_h[0], s_h[1]), lambda h: vs[h],
                 per_head(ksn[:, :hd], ksn[:, hd:]), per_head(vsn[:, :hd], vsn[:, hd:]))

    win = win_ref[0]
    wb = win.shape[1]
    dist = wb - lax.broadcasted_iota(jnp.int32, (1, wb), 1)
    okw = (dist < WINDOW) & (past - dist >= 0)
    lw = per_head(_dot(q8, win[0:hd]), _dot(q8, win[hd:2 * hd]))
    sw = jnp.where(okw, lw + _bias_rows(dist, rbt), NEG_INF)
    kwn = new[:, 4 * B_KV_WIDTH:5 * B_KV_WIDTH]
    vwn = new[:, 5 * B_KV_WIDTH:6 * B_KV_WIDTH]
    o_w = attend(sw, lambda h: win[B_KV_WIDTH + h * hd:B_KV_WIDTH + (h + 1) * hd],
                 per_head(kwn[:, :hd], kwn[:, hd:]), per_head(vwn[:, :hd], vwn[:, hd:]))

    gate = jax.nn.sigmoid(gt_ref[0])
    o_ref[0] = gate[:, 0:1] * oc_ref[0] + gate[:, 1:2] * o_s + gate[:, 2:3] * o_w


def _sample_attn(page_table, idx, cache_t, q8, gate8, kv_new, oc, win_t, rbt, l):
    ns, n_pages = page_table.shape
    n_blk = idx.shape[-1]
    past = n_pages * PAGE_SIZE
    per_page = PAGE_SIZE // SEL_BLOCK

    pages = jnp.take_along_axis(page_table[:, None, :], idx // per_page, axis=-1)

    def blk_spec(h, k):
        return pl.BlockSpec((None, None, _KV_ROWS, PAGE_SIZE), lambda b, pg, ix: (l, pg[b, h, k], 0, 0))

    wb = win_t.shape[3]
    return pl.pallas_call(
        functools.partial(_sattn_kernel, n_blk=n_blk, past=past),
        grid_spec=pltpu.PrefetchScalarGridSpec(
            num_scalar_prefetch=2,
            grid=(ns,),
            in_specs=[blk_spec(h, k) for h in range(B_KV_HEADS) for k in range(n_blk)] + [
                pl.BlockSpec((1, SUBLANES, B_HEAD_DIM), lambda b, pt, ix: (b, 0, 0)),
                pl.BlockSpec((1, SUBLANES, LANES), lambda b, pt, ix: (b, 0, 0)),
                pl.BlockSpec((1, 1, 6 * B_KV_WIDTH), lambda b, pt, ix: (b, 0, 0)),
                pl.BlockSpec((1, SUBLANES, B_HEAD_DIM), lambda b, pt, ix: (b, 0, 0)),
                pl.BlockSpec((None, 1, _KV_ROWS, wb), lambda b, pt, ix: (l, b, 0, 0)),
                pl.BlockSpec((SUBLANES, REL_BUCKETS), lambda b, pt, ix: (0, 0))],
            out_specs=pl.BlockSpec((1, SUBLANES, B_HEAD_DIM), lambda b, pt, ix: (b, 0, 0))),
        out_shape=jax.ShapeDtypeStruct((ns, SUBLANES, B_HEAD_DIM), _F32),
        compiler_params=_cparams(("arbitrary",)),
        name="nsa_sample_attn",
    )(pages, idx, *([cache_t] * (B_KV_HEADS * n_blk)), q8, gate8, kv_new, oc, win_t, rbt)


def _smix_kernel(uv_ref, ng_ref, wd_ref, b0_ref, xbc_ref, st_ref, cw_ref, cb_ref, dt_ref, hp_ref,
                 dsk_ref, ex_ref, oa_ref, v_ref, xdt_ref, ea_ref, y1_ref, bc_ref):
    u, v = _gelu_ln(uv_ref[...], ng_ref[...])
    v_ref[...] = v
    oa_ref[...] = u * (v * wd_ref[...] + b0_ref[...])
    conv = cb_ref[...] + xbc_ref[...] * cw_ref[C_CONV - 1:C_CONV, :]
    for k in range(C_CONV - 1):
        conv = conv + st_ref[k] * cw_ref[k:k + 1, :]
    xc = _silu(conv)
    xs = xc[:, :C_WIDTH]
    bm = xc[:, C_WIDTH:C_WIDTH + C_GROUPS * C_STATE]
    cm = xc[:, C_WIDTH + C_GROUPS * C_STATE:]
    bc_ref[...] = xc[:, C_WIDTH:]
    dt = jax.nn.softplus(dt_ref[...] + hp_ref[0:1, :])
    acum = dt * (-jnp.exp(hp_ref[1:2, :]))
    dt_rep = _dot_exact(dt, ex_ref[...])
    ea_ref[...] = jnp.exp(_dot_exact(acum, ex_ref[...]))
    xdt = dt_rep * xs
    xdt_ref[...] = xdt
    per_g = C_WIDTH // C_GROUPS
    cb = [jnp.sum(cm[:, g * C_STATE:(g + 1) * C_STATE] * bm[:, g * C_STATE:(g + 1) * C_STATE],
                  axis=-1, keepdims=True) for g in range(C_GROUPS)]
    lane = lax.broadcasted_iota(jnp.int32, xs.shape, 1)
    y1_ref[...] = jnp.where(lane < per_g, cb[0], cb[1]) * xdt + dsk_ref[...] * xs


def _sample_mix(uv, ng_row, wd_row, b0_row, xbc, st, conv_w_l, conv_b_row, dt, hp, dsk_row, expand):
    ns = uv.shape[0]
    f = lambda w: jax.ShapeDtypeStruct((ns, w), _F32)
    return pl.pallas_call(
        _smix_kernel,
        out_shape=[f(A_WIDTH), f(A_WIDTH), f(C_WIDTH), f(C_WIDTH), f(C_WIDTH), f(2 * C_GROUPS * C_STATE)],
        name="sample_gmlp_conv",
    )(uv, ng_row, wd_row, b0_row, xbc, st, conv_w_l, conv_b_row, dt, hp, dsk_row, expand)


def _sssm_kernel(h0_ref, xdt_ref, ea_ref, y1_ref, bc_ref, z_ref, ng_ref, o_ref, hout_ref):
    per_g = C_WIDTH // C_GROUPS
    h0 = h0_ref[0]
    bc = bc_ref[0]
    bm = bc[:, :C_GROUPS * C_STATE]
    cm = bc[:, C_GROUPS * C_STATE:]
    c8 = [jnp.broadcast_to(cm[:, g * C_STATE:(g + 1) * C_STATE], (SUBLANES, C_STATE)) for g in range(C_GROUPS)]
    ch = jnp.concatenate([_dot_nt(c8[g], h0[g * per_g:(g + 1) * per_g, :])[0:1] for g in range(C_GROUPS)], axis=-1)
    y = y1_ref[0] + ch * ea_ref[0]
    o_ref[0] = _rms(y * _silu(z_ref[0]), ng_ref[...])
    row = lax.broadcasted_iota(jnp.int32, (LANES, C_WIDTH), 0)
    cols = jnp.where(row == 0, xdt_ref[0], jnp.where(row == 1, ea_ref[0], 0.0)).T
    rsel = lax.broadcasted_iota(jnp.int32, (C_WIDTH, C_STATE), 0) < per_g
    b_full = jnp.where(rsel, bm[:, :C_STATE], bm[:, C_STATE:])
    hout_ref[0] = h0 * cols[:, 1:2] + cols[:, 0:1] * b_full


def _sample_ssm(h0, xdt, ea, y1, bc, z, ng_row, l):
    ns = xdt.shape[0]
    r3 = lambda a: a.reshape(ns, 1, a.shape[-1])
    row_spec = lambda w: pl.BlockSpec((1, 1, w), lambda b: (b, 0, 0))
    out, hout = pl.pallas_call(
        _sssm_kernel,
        grid=(ns,),
        in_specs=[pl.BlockSpec((None, 1, C_WIDTH, C_STATE), lambda b: (l, b, 0, 0)),
                  row_spec(C_WIDTH), row_spec(C_WIDTH), row_spec(C_WIDTH), row_spec(2 * C_GROUPS * C_STATE),
                  row_spec(C_WIDTH), pl.BlockSpec((1, C_WIDTH), lambda b: (0, 0))],
        out_specs=[row_spec(C_WIDTH), pl.BlockSpec((1, C_WIDTH, C_STATE), lambda b: (b, 0, 0))],
        out_shape=[jax.ShapeDtypeStruct((ns, 1, C_WIDTH), _F32),
                   jax.ShapeDtypeStruct((ns, C_WIDTH, C_STATE), _F32)],
        compiler_params=_cparams(("parallel",)),
        name="sample_ssm",
    )(h0, r3(xdt), r3(ea), r3(y1), r3(bc), r3(z), ng_row)
    return out.reshape(ns, C_WIDTH), hout


def _pad_cols(a, width):
    return jnp.pad(a, ((0, 0),) * (a.ndim - 1) + ((0, width - a.shape[-1]),))


def _pack_w_in(w_in):
    o = np.cumsum((0, A_WIDTH, A_WIDTH, B_WIDTH, 6 * B_KV_WIDTH, _GATE_COLS, C_WIDTH, C_CONV_DIM, C_HEADS))
    u0, q0, kv0, gate0, z0, xbc0, dt0, end = o[0], o[2], o[3], o[4], o[5], o[6], o[7], o[8]
    parts = [w_in[..., u0:kv0], w_in[..., z0:xbc0], w_in[..., xbc0:dt0],
             _pad_cols(w_in[..., gate0:z0], LANES), _pad_cols(w_in[..., dt0:end], LANES)]
    w_rows = jnp.concatenate(parts, axis=-1).astype(_MXU)
    w_kv_t = jnp.swapaxes(w_in[..., kv0:gate0], -1, -2).astype(_MXU)
    return w_rows, w_kv_t


def _head_rows(dt_bias, a_log):
    hp = jnp.zeros((DEPTH, SUBLANES, LANES), _F32)
    hp = hp.at[:, 0, :C_HEADS].set(dt_bias)
    return hp.at[:, 1, :C_HEADS].set(a_log)


def kernel(x_prompt, x_sample, cache_cmp_kv, cache_slc_kv, page_table, state_win_kv, state_conv, state_ssm,
           norm_g, ffn_w_gate, ffn_w_up, ffn_w_down, w_in, w_out, gmlp_norm_g, gmlp_w_s, gmlp_b_s,
           nsa_pe_cmp, nsa_w_cmp, rel_bias, conv_w, conv_b, dt_bias, a_log, d_skip, ssm_norm_g):
    bp, t = x_prompt.shape[:2]
    ns = x_sample.shape[0]
    n_pages = page_table.shape[1]
    n_phys = cache_cmp_kv.shape[1]
    tm_p, tm_s = 512, ns
    assert t % _KEY_CHUNK == 0 and t % tm_p == 0

    wg, wu, wd = (w.astype(_MXU) for w in (ffn_w_gate, ffn_w_up, ffn_w_down))
    w_in_p, w_kv_t = _pack_w_in(w_in)
    w_out_b = w_out.astype(_MXU)
    pe_t = jnp.swapaxes(nsa_pe_cmp, -1, -2)
    wc_t = jnp.swapaxes(nsa_w_cmp, -1, -2)
    bs_t = jnp.swapaxes(gmlp_b_s, 1, 2)
    wdiag = jnp.repeat(gmlp_w_s[:, :, 0, 0], A_HEAD_DIM, axis=-1)
    b0 = jnp.repeat(gmlp_b_s[:, :, 0], A_HEAD_DIM, axis=-1)
    hp = _head_rows(dt_bias, a_log)
    dsk = jnp.repeat(d_skip, C_HEAD_DIM, axis=-1)
    expand = (jnp.arange(LANES)[:, None] == jnp.arange(C_WIDTH)[None, :] // C_HEAD_DIM).astype(_F32)
    rb_flat = rel_bias.reshape(-1)
    rbt = _pad_cols(rel_bias, SUBLANES).T
    def chan_major(a):
        lead, tok = a.shape[:-4], a.shape[-4]
        perm = tuple(range(len(lead))) + tuple(len(lead) + k for k in (1, 2, 3, 0))
        return a.transpose(perm).reshape(*lead, _KV_ROWS, tok)

    def token_major(a_t):
        lead, tok = a_t.shape[:-2], a_t.shape[-1]
        a5 = a_t.reshape(*lead, 2, B_KV_HEADS, B_HEAD_DIM, tok)
        perm = tuple(range(len(lead))) + tuple(len(lead) + k for k in (3, 0, 1, 2))
        return a5.transpose(perm)

    cache_c = chan_major(cache_cmp_kv)
    cache_s = chan_major(cache_slc_kv)
    win_state = chan_major(state_win_kv)
    ssm_state = state_ssm.reshape(DEPTH, ns, C_WIDTH, C_STATE)

    xp = x_prompt.reshape(bp * t, D_MODEL)
    xs = x_sample.reshape(ns, D_MODEL)
    outs = [[] for _ in range(11)]
    for l in range(DEPTH):
        g = norm_g[l]
        row = lambda a: a.reshape(1, -1)
        xp = _ffn(xp, g[0:2], wg, wu, wd, l, 0, tm_p)
        xs = _ffn(xs, g[0:2], wg, wu, wd, l, 0, tm_s)

        uv, q, z, xbc, gate, dt, kvc_t, kvs_t, kvw_t = _inproj(xp.reshape(bp, t, D_MODEL), row(g[2]), w_in_p,
                                                               w_kv_t, l, tm_p)
        oa = _gmlp_prompt(uv.reshape(bp * t, 2 * A_WIDTH), row(gmlp_norm_g[l]), gmlp_w_s, bs_t, l)
        kvcmp_t = _compress_prompt(kvc_t, pe_t, wc_t, l)
        ob = _nsa_prompt(q, gate, *_nsa_prompt_inputs(kvcmp_t, kvs_t, kvw_t), rb_flat)
        oc, h_p = _mamba_prompt(z, xbc, dt, conv_w, row(conv_b[l]), hp[l], row(dsk[l]), row(ssm_norm_g[l]), l)
        xp = _outproj(xp, oa, ob.reshape(bp * t, B_WIDTH), oc.reshape(bp * t, C_WIDTH), row(g[3]), w_out_b, l, tm_p)
        wkeep = min(WINDOW, t)
        outs[0].append(token_major(kvc_t))
        outs[1].append(token_major(kvs_t))
        outs[2].append(token_major(kvw_t[:, :, t - wkeep:]))
        outs[3].append(xbc[:, t - (C_CONV - 1):])
        outs[4].append(h_p)

        uv, q, z, xbc, gate, dt, kvc_t, kvs_t, kvw_t = (
            a[0] for a in _inproj(xs.reshape(1, ns, D_MODEL), row(g[2]), w_in_p, w_kv_t, l, tm_s))
        q8 = jnp.pad(q.reshape(ns, B_HEADS, B_HEAD_DIM), ((0, 0), (0, SUBLANES - B_HEADS), (0, 0)))
        gate8 = jnp.pad(gate[:, :_GATE_COLS].reshape(ns, B_HEADS, N_BRANCH),
                        ((0, 0), (0, SUBLANES - B_HEADS), (0, LANES - N_BRANCH)))
        o_cmp, imp = _sample_cmp(page_table, cache_c, q8, pe_t, wc_t, rbt, l)
        n_sel_past = imp.shape[-1]
        idx = _sample_topk(imp.reshape(ns * SUBLANES, n_sel_past), SEL_TOPK - 1)
        idx = idx.reshape(ns, SUBLANES, n_sel_past)[:, :B_KV_HEADS, :SEL_TOPK - 1]
        kvc, kvs, kvwin = kvc_t.T, kvs_t.T, kvw_t.T
        kv_new = jnp.concatenate([kvc, kvs, kvwin], axis=-1).reshape(ns, 1, 6 * B_KV_WIDTH)
        ob8 = _sample_attn(page_table, idx, cache_s, q8, gate8, kv_new, o_cmp, win_state, rbt, l)
        ob = ob8[:, :B_HEADS].reshape(ns, B_WIDTH)
        st = jnp.swapaxes(state_conv[l], 0, 1)
        oa, v_rows, xdt, ea, y1, bc = _sample_mix(uv, row(gmlp_norm_g[l]), row(wdiag[l]), row(b0[l]), xbc, st,
                                                  conv_w[l], row(conv_b[l]), dt, hp[l], row(dsk[l]), expand)
        oc, h_s = _sample_ssm(ssm_state, xdt, ea, y1, bc, z, row(ssm_norm_g[l]), l)
        xs = _outproj(xs, oa, ob, oc, row(g[3]), w_out_b, l, tm_s)
        outs[5].append(kvc.reshape(ns, 1, 2, B_KV_HEADS, B_HEAD_DIM))
        outs[6].append(kvs.reshape(ns, 1, 2, B_KV_HEADS, B_HEAD_DIM))
        outs[7].append(token_major(jnp.concatenate([win_state[l][:, :, 1:], kvwin[:, :, None]], axis=-1)))
        outs[8].append(jnp.concatenate([state_conv[l][:, 1:], xbc[:, None]], axis=1))
        outs[9].append(h_s.reshape(ns, C_HEADS, C_HEAD_DIM, C_STATE))
        outs[10].append(v_rows.reshape(ns, 1, A_WIDTH))

        xp = _ffn(xp, g[4:6], wg, wu, wd, l, 1, tm_p)
        xs = _ffn(xs, g[4:6], wg, wu, wd, l, 1, tm_s)
    stacked = [jnp.stack(o) for o in outs]
    return (xp.reshape(bp, t, D_MODEL), xs.reshape(ns, 1, D_MODEL), *stacked)
```

```python
import functools
import math

import numpy as np
import jax
import jax.numpy as jnp
from jax import lax
from jax.experimental import pallas as pl
from jax.experimental.pallas import tpu as pltpu

D_MODEL = 1024
DEPTH = 4
PAGE_SIZE = 128
A_HEADS, A_HEAD_DIM, A_CHUNK = 4, 64, 128
A_WIDTH = A_HEADS * A_HEAD_DIM
B_HEADS, B_KV_HEADS, B_HEAD_DIM = 6, 2, 64
B_GROUP = B_HEADS // B_KV_HEADS
B_WIDTH = B_HEADS * B_HEAD_DIM
B_KV_WIDTH = B_KV_HEADS * B_HEAD_DIM
N_BRANCH = 3
CMP_BLOCK, SEL_BLOCK, SEL_TOPK, WINDOW, Q_BLOCK = 32, 64, 16, 512, 128
FORCE_SCORE = 1e4
C_HEADS, C_HEAD_DIM, C_GROUPS, C_STATE, C_CONV = 6, 64, 2, 64, 4
C_WIDTH = C_HEADS * C_HEAD_DIM
C_CONV_DIM = C_WIDTH + 2 * C_GROUPS * C_STATE
SSD_CHUNK = 128
D_FF = 2816
REL_BUCKETS, REL_MAX_EXACT, REL_MAX_DIST = 32, 16, 128
EPS = 1e-6
NEG_INF = -1e30
SOFTMAX_FLOOR = -1e20

LANES = 128
SUBLANES = 8
VMEM_LIMIT_BYTES = 56 * 1024 * 1024

_MXU = jnp.bfloat16
_F32 = jnp.float32

_GATE_COLS = N_BRANCH * B_HEADS
_SEG = (("uv", 2 * A_WIDTH), ("q", B_WIDTH), ("z", C_WIDTH), ("xbc", C_CONV_DIM), ("gate", LANES), ("dt", LANES))
_D_IN_PAD = sum(w for _, w in _SEG)
_KV_ROWS = 2 * B_KV_WIDTH
_N_KV = 3


def _dot(a, b):
    return jnp.dot(a.astype(_MXU), b.astype(_MXU), preferred_element_type=_F32)


def _dot_nt(a, b):
    return lax.dot_general(a.astype(_MXU), b.astype(_MXU), (((1,), (1,)), ((), ())),
                           preferred_element_type=_F32)


def _dot_tn(a, b):
    return lax.dot_general(a.astype(_MXU), b.astype(_MXU), (((0,), (0,)), ((), ())),
                           preferred_element_type=_F32)


def _dot_exact(a, b):
    return jnp.dot(a, b, preferred_element_type=_F32, precision=lax.Precision.HIGHEST)


def _rms(x, g):
    return x * lax.rsqrt(jnp.mean(x * x, axis=-1, keepdims=True) + EPS) * g


def _silu(x):
    return x * jax.nn.sigmoid(x)


def _cparams(sem):
    return pltpu.CompilerParams(dimension_semantics=sem, vmem_limit_bytes=VMEM_LIMIT_BYTES)


def _bucket_np(dist):
    n = np.maximum(dist, 0)
    nf = np.maximum(n, 1).astype(np.float32)
    large = REL_MAX_EXACT + (np.log(nf / np.float32(REL_MAX_EXACT))
                             / np.float32(math.log(REL_MAX_DIST / REL_MAX_EXACT))
                             * np.float32(REL_BUCKETS - REL_MAX_EXACT)).astype(np.int32)
    large = np.minimum(large, REL_BUCKETS - 1)
    return np.where(n < REL_MAX_EXACT, n, large).astype(np.int32)


def _ffn_kernel(x_ref, g_ref, wg_ref, wu_ref, wd_ref, o_ref, *, tf):
    x = x_ref[...]
    h = _rms(x, g_ref[0:1, :]).astype(_MXU)
    y = jnp.zeros(x.shape, _F32)
    for f in range(D_FF // tf):
        cols = slice(f * tf, (f + 1) * tf)
        a = _silu(_dot(h, wg_ref[:, cols])) * _dot(h, wu_ref[:, cols])
        y = y + _dot(a, wd_ref[cols, :])
    o_ref[...] = x + 0.5 * _rms(y, g_ref[1:2, :])


def _ffn(x, g2, wg, wu, wd, l, j, tm, tf=256):
    rows = x.shape[0]
    once = pl.Buffered(1)
    return pl.pallas_call(
        functools.partial(_ffn_kernel, tf=tf),
        grid=(rows // tm,),
        in_specs=[pl.BlockSpec((tm, D_MODEL), lambda r: (r, 0)),
                  pl.BlockSpec((2, D_MODEL), lambda r: (0, 0)),
                  pl.BlockSpec((None, None, D_MODEL, D_FF), lambda r: (l, j, 0, 0), pipeline_mode=once),
                  pl.BlockSpec((None, None, D_MODEL, D_FF), lambda r: (l, j, 0, 0), pipeline_mode=once),
                  pl.BlockSpec((None, None, D_FF, D_MODEL), lambda r: (l, j, 0, 0), pipeline_mode=once)],
        out_specs=pl.BlockSpec((tm, D_MODEL), lambda r: (r, 0)),
        out_shape=jax.ShapeDtypeStruct((rows, D_MODEL), _F32),
        compiler_params=_cparams(("parallel",)),
        name="half_ffn",
    )(x, g2, wg, wu, wd)


def _inproj_kernel(x_ref, g_ref, w_ref, wkv_ref, *o_refs):
    h = _rms(x_ref[0], g_ref[...]).astype(_MXU)
    off = 0
    for (_, width), o_ref in zip(_SEG, o_refs):
        o_ref[0] = _dot(h, w_ref[:, off:off + width])
        off += width
    for k, o_ref in enumerate(o_refs[len(_SEG):]):
        o_ref[0] = _dot_nt(wkv_ref[k * _KV_ROWS:(k + 1) * _KV_ROWS, :], h)


def _inproj(x, g_row, w_in_p, w_kv_t, l, tm):
    nb, t = x.shape[:2]
    return pl.pallas_call(
        _inproj_kernel,
        grid=(nb, t // tm),
        in_specs=[pl.BlockSpec((1, tm, D_MODEL), lambda b, r: (b, r, 0)),
                  pl.BlockSpec((1, D_MODEL), lambda b, r: (0, 0)),
                  pl.BlockSpec((None, D_MODEL, _D_IN_PAD), lambda b, r: (l, 0, 0)),
                  pl.BlockSpec((None, _N_KV * _KV_ROWS, D_MODEL), lambda b, r: (l, 0, 0))],
        out_specs=[pl.BlockSpec((1, tm, w), lambda b, r: (b, r, 0)) for _, w in _SEG]
        + [pl.BlockSpec((1, _KV_ROWS, tm), lambda b, r: (b, 0, r))] * _N_KV,
        out_shape=[jax.ShapeDtypeStruct((nb, t, w), _F32) for _, w in _SEG]
        + [jax.ShapeDtypeStruct((nb, _KV_ROWS, t), _F32)] * _N_KV,
        compiler_params=_cparams(("parallel", "parallel")),
        name="in_proj",
    )(x, g_row, w_in_p, w_kv_t)


def _outproj_kernel(x_ref, oa_ref, ob_ref, oc_ref, g_ref, w_ref, o_ref):
    y = (_dot(oa_ref[...], w_ref[0:A_WIDTH, :])
         + _dot(ob_ref[...], w_ref[A_WIDTH:A_WIDTH + B_WIDTH, :])
         + _dot(oc_ref[...], w_ref[A_WIDTH + B_WIDTH:, :]))
    o_ref[...] = x_ref[...] + _rms(y, g_ref[...])


def _outproj(x, oa, ob, oc, g_row, w_out, l, tm):
    rows = x.shape[0]
    return pl.pallas_call(
        _outproj_kernel,
        grid=(rows // tm,),
        in_specs=[pl.BlockSpec((tm, D_MODEL), lambda r: (r, 0)),
                  pl.BlockSpec((tm, A_WIDTH), lambda r: (r, 0)),
                  pl.BlockSpec((tm, B_WIDTH), lambda r: (r, 0)),
                  pl.BlockSpec((tm, C_WIDTH), lambda r: (r, 0)),
                  pl.BlockSpec((1, D_MODEL), lambda r: (0, 0)),
                  pl.BlockSpec((None, D_MODEL, D_MODEL), lambda r: (l, 0, 0))],
        out_specs=pl.BlockSpec((tm, D_MODEL), lambda r: (r, 0)),
        out_shape=jax.ShapeDtypeStruct((rows, D_MODEL), _F32),
        compiler_params=_cparams(("parallel",)),
        name="out_proj",
    )(x, oa, ob, oc, g_row, w_out)


def _gelu_ln(uv, ng):
    u = jax.nn.gelu(uv[:, :A_WIDTH])
    v = jax.nn.gelu(uv[:, A_WIDTH:])
    mu = jnp.mean(v, axis=-1, keepdims=True)
    var = jnp.mean(jnp.square(v - mu), axis=-1, keepdims=True)
    return u, (v - mu) * lax.rsqrt(var + EPS) * ng


def _gmlp_kernel(uv_ref, ng_ref, ws_ref, bs_ref, o_ref, *, chunks):
    row = lax.broadcasted_iota(jnp.int32, (A_CHUNK, A_CHUNK), 0)
    col = lax.broadcasted_iota(jnp.int32, (A_CHUNK, A_CHUNK), 1)
    ws = [jnp.where(col <= row, ws_ref[h], 0.0).astype(_MXU) for h in range(A_HEADS)]
    for c in range(chunks):
        u, v = _gelu_ln(uv_ref[c * A_CHUNK:(c + 1) * A_CHUNK, :], ng_ref[...])
        sg = [_dot(ws[h], v[:, h * A_HEAD_DIM:(h + 1) * A_HEAD_DIM]) + bs_ref[:, h:h + 1]
              for h in range(A_HEADS)]
        o_ref[c * A_CHUNK:(c + 1) * A_CHUNK, :] = u * jnp.concatenate(sg, axis=-1)


def _gmlp_prompt(uv, ng_row, ws, bs_t, l, chunks=4):
    rows = uv.shape[0]
    tm = chunks * A_CHUNK
    return pl.pallas_call(
        functools.partial(_gmlp_kernel, chunks=chunks),
        grid=(rows // tm,),
        in_specs=[pl.BlockSpec((tm, 2 * A_WIDTH), lambda r: (r, 0)),
                  pl.BlockSpec((1, A_WIDTH), lambda r: (0, 0)),
                  pl.BlockSpec((None, A_HEADS, A_CHUNK, A_CHUNK), lambda r: (l, 0, 0, 0)),
                  pl.BlockSpec((None, A_CHUNK, A_HEADS), lambda r: (l, 0, 0))],
        out_specs=pl.BlockSpec((tm, A_WIDTH), lambda r: (r, 0)),
        out_shape=jax.ShapeDtypeStruct((rows, A_WIDTH), _F32),
        compiler_params=_cparams(("parallel",)),
        name="gmlp_prompt",
    )(uv, ng_row, ws, bs_t)


def _mamba_kernel(z_ref, xbc_ref, dt_ref, cw_ref, cb_ref, hp_ref, dsk_ref, ng_ref,
                  o_ref, hout_ref, xp_ref, hs_ref):
    t = pl.program_id(1)
    L = SSD_CHUNK
    hist = SUBLANES

    @pl.when(t == 0)
    def _():
        xp_ref[0:hist, :] = jnp.zeros((hist, C_CONV_DIM), _F32)
        hs_ref[...] = jnp.zeros_like(hs_ref)

    xp_ref[hist:hist + L, :] = xbc_ref[0]
    conv = cb_ref[...]
    for k in range(C_CONV):
        conv = conv + xp_ref[pl.ds(hist - (C_CONV - 1) + k, L), :] * cw_ref[k:k + 1, :]
    xp_ref[0:hist, :] = xp_ref[L:L + hist, :]
    xc = _silu(conv)
    xs = xc[:, :C_WIDTH]
    bm = xc[:, C_WIDTH:C_WIDTH + C_GROUPS * C_STATE]
    cm = xc[:, C_WIDTH + C_GROUPS * C_STATE:]

    dt = jax.nn.softplus(dt_ref[0] + hp_ref[0:1, :])
    a_row = -jnp.exp(hp_ref[1:2, :])
    row = lax.broadcasted_iota(jnp.int32, (L, L), 0)
    col = lax.broadcasted_iota(jnp.int32, (L, L), 1)
    causal = col <= row
    acum = _dot_exact(jnp.where(causal, 1.0, 0.0), dt * a_row)
    acum_t = acum.T
    dt_t = dt.T
    cb = [_dot_nt(cm[:, g * C_STATE:(g + 1) * C_STATE], bm[:, g * C_STATE:(g + 1) * C_STATE])
          for g in range(C_GROUPS)]
    ys = []
    for h in range(C_HEADS):
        g = h // (C_HEADS // C_GROUPS)
        ac_col = acum[:, h:h + 1]
        seg = ac_col - acum_t[h:h + 1, :]
        decay = jnp.where(causal, jnp.exp(jnp.where(causal, seg, 0.0)), 0.0)
        scores = cb[g] * decay * dt_t[h:h + 1, :]
        x_h = xs[:, h * C_HEAD_DIM:(h + 1) * C_HEAD_DIM]
        b_g = bm[:, g * C_STATE:(g + 1) * C_STATE]
        c_g = cm[:, g * C_STATE:(g + 1) * C_STATE]
        hs = hs_ref[h]
        ys.append(_dot(scores, x_h) + _dot_nt(c_g, hs) * jnp.exp(ac_col))
        ac_last = acum[L - 1:L, h:h + 1]
        w_end = jnp.exp(ac_last - ac_col) * dt[:, h:h + 1]
        hs_ref[h] = hs * jnp.exp(ac_last) + _dot_tn(x_h * w_end, b_g)
    y = jnp.concatenate(ys, axis=-1) + dsk_ref[...] * xs
    o_ref[0] = _rms(y * _silu(z_ref[0]), ng_ref[...])

    @pl.when(t == pl.num_programs(1) - 1)
    def _():
        hout_ref[0] = hs_ref[...]


def _mamba_prompt(z, xbc, dt, conv_w, conv_b_row, hp, dsk_row, ng_row, l):
    nb, t = z.shape[:2]
    L = SSD_CHUNK
    return pl.pallas_call(
        _mamba_kernel,
        grid=(nb, t // L),
        in_specs=[pl.BlockSpec((1, L, C_WIDTH), lambda b, c: (b, c, 0)),
                  pl.BlockSpec((1, L, C_CONV_DIM), lambda b, c: (b, c, 0)),
                  pl.BlockSpec((1, L, LANES), lambda b, c: (b, c, 0)),
                  pl.BlockSpec((None, C_CONV, C_CONV_DIM), lambda b, c: (l, 0, 0)),
                  pl.BlockSpec((1, C_CONV_DIM), lambda b, c: (0, 0)),
                  pl.BlockSpec((SUBLANES, LANES), lambda b, c: (0, 0)),
                  pl.BlockSpec((1, C_WIDTH), lambda b, c: (0, 0)),
                  pl.BlockSpec((1, C_WIDTH), lambda b, c: (0, 0))],
        out_specs=[pl.BlockSpec((1, L, C_WIDTH), lambda b, c: (b, c, 0)),
                   pl.BlockSpec((1, C_HEADS, C_HEAD_DIM, C_STATE), lambda b, c: (b, 0, 0, 0))],
        out_shape=[jax.ShapeDtypeStruct((nb, t, C_WIDTH), _F32),
                   jax.ShapeDtypeStruct((nb, C_HEADS, C_HEAD_DIM, C_STATE), _F32)],
        scratch_shapes=[pltpu.VMEM((L + 2 * SUBLANES, C_CONV_DIM), _F32),
                        pltpu.VMEM((C_HEADS, C_HEAD_DIM, C_STATE), _F32)],
        compiler_params=_cparams(("arbitrary", "arbitrary")),
        name="mamba_prompt",
    )(z, xbc, dt, conv_w, conv_b_row, hp, dsk_row, ng_row)


def _split_dot(x, p):
    hi = x.astype(_MXU)
    lo = (x - hi.astype(_F32)).astype(_MXU)
    return jnp.dot(hi, p, preferred_element_type=_F32) + jnp.dot(lo, p, preferred_element_type=_F32)


def _pool_matrix(n_tok, per, n_out, first=0):
    r = lax.broadcasted_iota(jnp.int32, (n_tok, n_out), 0)
    c = lax.broadcasted_iota(jnp.int32, (n_tok, n_out), 1)
    return jnp.where(c == first + r // per, 1.0, 0.0).astype(_MXU)


def _pe_sums_t(pet_ref):
    pk = jnp.sum(pet_ref[0], axis=-1, keepdims=True)
    pv = jnp.sum(pet_ref[1], axis=-1, keepdims=True)
    return jnp.concatenate([pk, pk, pv, pv], axis=0)


def _compress_cols(m, wt_ref):
    hd = B_HEAD_DIM
    return jnp.concatenate([_dot(wt_ref[0], m[0:hd]), _dot(wt_ref[0], m[hd:2 * hd]),
                            _dot(wt_ref[1], m[2 * hd:3 * hd]), _dot(wt_ref[1], m[3 * hd:])], axis=0)


def _compress_kernel(kv_ref, pet_ref, wt_ref, o_ref, *, chunk):
    t = kv_ref.shape[-1]
    pool = _pool_matrix(chunk, CMP_BLOCK, chunk // CMP_BLOCK)
    sums = jnp.concatenate([_split_dot(kv_ref[0, :, c * chunk:(c + 1) * chunk], pool) for c in range(t // chunk)],
                           axis=-1)
    o_ref[0] = _compress_cols((sums + _pe_sums_t(pet_ref)) * (1.0 / CMP_BLOCK), wt_ref)


def _compress_prompt(kvc_t, pe_t, w_t, l, chunk=2048):
    nb, _, t = kvc_t.shape
    return pl.pallas_call(
        functools.partial(_compress_kernel, chunk=min(chunk, t)),
        grid=(nb,),
        in_specs=[pl.BlockSpec((1, _KV_ROWS, t), lambda b: (b, 0, 0)),
                  pl.BlockSpec((None, 2, B_HEAD_DIM, CMP_BLOCK), lambda b: (l, 0, 0, 0)),
                  pl.BlockSpec((None, 2, B_HEAD_DIM, B_HEAD_DIM), lambda b: (l, 0, 0, 0))],
        out_specs=pl.BlockSpec((1, _KV_ROWS, t // CMP_BLOCK), lambda b: (b, 0, 0)),
        out_shape=jax.ShapeDtypeStruct((nb, _KV_ROWS, t // CMP_BLOCK), _F32),
        compiler_params=_cparams(("parallel",)),
        name="nsa_compress_prompt",
    )(kvc_t, pe_t, w_t)


_KEY_CHUNK = 512
_FRONT_PAD = Q_BLOCK
_NEAR = 2 * Q_BLOCK


def _nsa_tables(n_sel):
    r = np.arange(Q_BLOCK)[:, None]
    cmp_idx = np.full((2, Q_BLOCK, n_sel), REL_BUCKETS - 1, np.int32)
    for par in range(2):
        for u in (-2, -1, 0, 1):
            dist = r[:, 0] - (CMP_BLOCK - 1) - CMP_BLOCK * (2 * u + par)
            cmp_idx[par, :, u % n_sel] = _bucket_np(dist)
    c = np.arange(_NEAR)[None, :]
    dist = Q_BLOCK + r - c
    near_idx = np.where(dist >= 0, _bucket_np(dist), -1).astype(np.int32)
    c = np.arange(WINDOW + Q_BLOCK)[None, :]
    dist = r + WINDOW - c
    win_idx = np.where((dist >= 0) & (dist < WINDOW), _bucket_np(dist), -1).astype(np.int32)
    return cmp_idx, near_idx, win_idx


def _fill_bias(idx, rb_ref, hg, rel_to_last):
    base = rb_ref[(REL_BUCKETS - 1) * B_HEADS + hg] if rel_to_last else 0.0
    tile = jnp.where(idx < 0, NEG_INF, 0.0)
    for b in range(REL_BUCKETS):
        tile = jnp.where(idx == b, rb_ref[b * B_HEADS + hg] - base, tile)
    return tile


def _topk_mask(score, k, taken):
    n = score.shape[-1]
    lane = lax.broadcasted_iota(jnp.int32, score.shape, 1)
    s = jnp.where(taken, -jnp.inf, score)
    for _ in range(k):
        m = jnp.max(s, axis=-1, keepdims=True)
        first = jnp.min(jnp.where(s == m, lane, n), axis=-1, keepdims=True)
        s = jnp.where(lane == first, -jnp.inf, s)
    return s == -jnp.inf


def _masked_softmax(l):
    m = jnp.maximum(jnp.max(l, axis=-1, keepdims=True), SOFTMAX_FLOOR)
    e = jnp.exp(l - m)
    return e / jnp.maximum(jnp.sum(e, axis=-1, keepdims=True), 1e-20)


def _online_step(s, vt, carry):
    m_run, l_run, acc = carry
    m_new = jnp.maximum(m_run, jnp.max(s, axis=-1, keepdims=True))
    alpha = jnp.exp(m_run - m_new)
    p = jnp.exp(s - m_new)
    return (m_new, alpha * l_run + jnp.sum(p, axis=-1, keepdims=True), alpha * acc + _dot_nt(p, vt))


def _key_tiles(ref, first, n, rows=slice(None)):
    return jnp.concatenate([ref[0, first + j, rows, :] for j in range(n)], axis=-1)


def _nsa_kernel(q_ref, gt_ref, kc_ref, vc_ref, ksa_ref, vs_ref, kwa_ref, vwa_ref,
                cidx_ref, nidx_ref, widx_ref, rb_ref, o_ref, bc_ref, bn_ref, bw_ref, *, n_sel):
    i = pl.program_id(1)
    hd, G = B_HEAD_DIM, B_GROUP
    QB = Q_BLOCK

    @pl.when((pl.program_id(0) == 0) & (i == 0))
    def _():
        for hg in range(B_HEADS):
            for par in range(2):
                bc_ref[hg, par] = _fill_bias(cidx_ref[par], rb_ref, hg, False)
            bn_ref[hg] = _fill_bias(nidx_ref[...], rb_ref, hg, True)
            bw_ref[hg] = _fill_bias(widx_ref[...], rb_ref, hg, False)

    q = q_ref[0] * (hd ** -0.5)
    gate = jax.nn.sigmoid(gt_ref[0])
    r_col = lax.broadcasted_iota(jnp.int32, (QB, 1), 0)
    qpos = i * QB + r_col
    n_lane = lax.broadcasted_iota(jnp.int32, (QB, n_sel), 1)
    cur = qpos // SEL_BLOCK
    vis = [CMP_BLOCK * (2 * n_lane + par) + (CMP_BLOCK - 1) <= qpos for par in range(2)]
    vis3 = jnp.concatenate([jnp.concatenate(vis, axis=-1)] * G, axis=0)
    c_near = lax.broadcasted_iota(jnp.int32, (QB, _NEAR), 1)
    near_ok = jnp.concatenate([c_near >= _FRONT_PAD - i * QB] * G, axis=0)
    c_win = lax.broadcasted_iota(jnp.int32, (QB, WINDOW + QB), 1)
    win_ok = jnp.concatenate([c_win >= WINDOW - i * QB] * G, axis=0)
    n_main = (jnp.maximum(i - 1, 0) * QB + _KEY_CHUNK - 1) // _KEY_CHUNK
    zeros_h = jnp.zeros((G * QB, hd), _F32)

    heads = range(B_KV_HEADS)
    hrows = [slice(h * hd, (h + 1) * hd) for h in heads]
    q3 = [jnp.concatenate([q[:, (h * G + g) * hd:(h * G + g + 1) * hd] for g in range(G)], axis=0) for h in heads]
    q3h = [jnp.concatenate([q3[h], zeros_h] if h == 0 else [zeros_h, q3[h]], axis=-1) for h in heads]
    t0 = i * (QB // LANES)

    forced = (n_lane == cur) | (n_lane == 0)
    started = n_lane <= cur
    o_c, imp = [], []
    for h in heads:
        lc = _dot(q3[h], kc_ref[0, hrows[h], :])
        bias_c = jnp.concatenate(
            [jnp.concatenate([pltpu.roll(bc_ref[h * G + g, par], 2 * i, 1) for par in range(2)], axis=-1)
             for g in range(G)], axis=0)
        p_c = _masked_softmax(jnp.where(vis3, lc + bias_c, NEG_INF))
        o_c.append(_dot_nt(p_c, vc_ref[0, hrows[h], :]))
        imp_h = sum(p_c[g * QB:(g + 1) * QB, :n_sel] + p_c[g * QB:(g + 1) * QB, n_sel:] for g in range(G))
        imp.append(jnp.where(started, imp_h, -1.0))

    chosen = _topk_mask(jnp.concatenate(imp, axis=0), min(SEL_TOPK, n_sel) - 2,
                        jnp.concatenate([forced] * B_KV_HEADS, axis=0))
    n_win = (WINDOW + QB) // LANES
    kw = _key_tiles(kwa_ref, t0, n_win)
    o_w = []
    for h in heads:
        bias_w = jnp.concatenate([bw_ref[h * G + g] for g in range(G)], axis=0)
        p_w = _masked_softmax(jnp.where(win_ok, _dot(q3h[h], kw) + bias_w, NEG_INF))
        o_w.append(_dot_nt(p_w, _key_tiles(vwa_ref, t0, n_win, hrows[h])))

    qa_main, qa_near = [], []
    for h in heads:
        allowed = chosen[h * QB:(h + 1) * QB] & started
        m_main = jnp.where(allowed & (n_lane < 2 * i - 2), 0.0, NEG_INF)
        m_near = jnp.where(allowed, 0.0, NEG_INF)
        qa_main.append(jnp.concatenate([q3h[h], jnp.concatenate([m_main] * G, axis=0)], axis=-1).astype(_MXU))
        qa_near.append(jnp.concatenate([q3h[h], jnp.concatenate([m_near] * G, axis=0)], axis=-1).astype(_MXU))

    per_chunk = _KEY_CHUNK // LANES

    def main_body(c, carry):
        tc = _FRONT_PAD // LANES + c * per_chunk
        ks = _key_tiles(ksa_ref, tc, per_chunk)
        return tuple(_online_step(_dot(qa_main[h], ks), _key_tiles(vs_ref, tc, per_chunk, hrows[h]), carry[h])
                     for h in heads)

    init = (jnp.full((G * QB, 1), SOFTMAX_FLOOR, _F32), jnp.zeros((G * QB, 1), _F32), zeros_h)
    carry = lax.fori_loop(0, n_main, main_body, (init,) * B_KV_HEADS)
    ks = _key_tiles(ksa_ref, t0, _NEAR // LANES)
    outs = []
    for h in heads:
        corr = jnp.concatenate([bn_ref[h * G + g] for g in range(G)], axis=0)
        s = jnp.where(near_ok, _dot(qa_near[h], ks) + corr, NEG_INF)
        _, l_s, acc_s = _online_step(s, _key_tiles(vs_ref, t0, _NEAR // LANES, hrows[h]), carry[h])
        o_s = acc_s / l_s
        for g in range(G):
            k0 = (h * G + g) * N_BRANCH
            rows = slice(g * QB, (g + 1) * QB)
            outs.append(gate[:, k0:k0 + 1] * o_c[h][rows] + gate[:, k0 + 1:k0 + 2] * o_s[rows]
                        + gate[:, k0 + 2:k0 + 3] * o_w[h][rows])
    o_ref[0] = jnp.concatenate(outs, axis=-1)


def _nsa_prompt(q, gate, kcp, vcp, ksa, vs, kwa, vwa, rb_flat):
    nb, t = q.shape[:2]
    n_sel = t // SEL_BLOCK
    cidx, nidx, widx = _nsa_tables(n_sel)
    tp, tw = ksa.shape[1], kwa.shape[1]
    full = lambda shape: pl.BlockSpec(shape, lambda b, i: (0,) * len(shape))
    return pl.pallas_call(
        functools.partial(_nsa_kernel, n_sel=n_sel),
        grid=(nb, t // Q_BLOCK),
        in_specs=[pl.BlockSpec((1, Q_BLOCK, B_WIDTH), lambda b, i: (b, i, 0)),
                  pl.BlockSpec((1, Q_BLOCK, LANES), lambda b, i: (b, i, 0)),
                  pl.BlockSpec((1, B_KV_WIDTH, 2 * n_sel), lambda b, i: (b, 0, 0)),
                  pl.BlockSpec((1, B_KV_WIDTH, 2 * n_sel), lambda b, i: (b, 0, 0)),
                  pl.BlockSpec((1, tp, B_KV_WIDTH + n_sel, LANES), lambda b, i: (b, 0, 0, 0)),
                  pl.BlockSpec((1, tp, B_KV_WIDTH, LANES), lambda b, i: (b, 0, 0, 0)),
                  pl.BlockSpec((1, tw, B_KV_WIDTH, LANES), lambda b, i: (b, 0, 0, 0)),
                  pl.BlockSpec((1, tw, B_KV_WIDTH, LANES), lambda b, i: (b, 0, 0, 0)),
                  full(cidx.shape), full(nidx.shape), full(widx.shape),
                  pl.BlockSpec(memory_space=pltpu.SMEM)],
        out_specs=pl.BlockSpec((1, Q_BLOCK, B_WIDTH), lambda b, i: (b, i, 0)),
        out_shape=jax.ShapeDtypeStruct((nb, t, B_WIDTH), _F32),
        scratch_shapes=[pltpu.VMEM((B_HEADS, 2, Q_BLOCK, n_sel), _F32),
                        pltpu.VMEM((B_HEADS, Q_BLOCK, _NEAR), _F32),
                        pltpu.VMEM((B_HEADS, Q_BLOCK, WINDOW + Q_BLOCK), _F32)],
        compiler_params=_cparams(("arbitrary", "arbitrary")),
        name="nsa_prompt",
    )(q, gate, kcp, vcp, ksa, vs, kwa, vwa, jnp.asarray(cidx), jnp.asarray(nidx), jnp.asarray(widx), rb_flat)


def _nsa_prompt_inputs(kvcmp_t, kvs_t, kvw_t):
    nb, _, t = kvs_t.shape
    n_sel = t // SEL_BLOCK
    kvp = kvcmp_t.reshape(nb, _KV_ROWS, n_sel, 2).transpose(0, 1, 3, 2).reshape(nb, _KV_ROWS, 2 * n_sel)
    kcp = kvp[:, :B_KV_WIDTH].astype(_MXU)
    vcp = kvp[:, B_KV_WIDTH:].astype(_MXU)
    blk = (jnp.arange(n_sel)[:, None] == jnp.arange(t)[None, :] // SEL_BLOCK).astype(_MXU)
    ksa = jnp.concatenate([kvs_t[:, :B_KV_WIDTH].astype(_MXU), jnp.broadcast_to(blk, (nb, n_sel, t))], axis=1)
    pad_s = ((0, 0), (0, 0), (_FRONT_PAD, _KEY_CHUNK - _FRONT_PAD))
    pad_w = ((0, 0), (0, 0), (WINDOW, 0))
    def tiles(a, pad):
        a = jnp.pad(a, pad)
        return a.reshape(nb, a.shape[1], a.shape[2] // LANES, LANES).transpose(0, 2, 1, 3)

    vs = tiles(kvs_t[:, B_KV_WIDTH:].astype(_MXU), pad_s)
    kwa = tiles(kvw_t[:, :B_KV_WIDTH].astype(_MXU), pad_w)
    vwa = tiles(kvw_t[:, B_KV_WIDTH:].astype(_MXU), pad_w)
    return kcp, vcp, tiles(ksa, pad_s), vs, kwa, vwa


def _bias_rows(dist, rbt):
    n = jnp.maximum(dist, 0)
    nf = jnp.maximum(n, 1).astype(_F32)
    large = REL_MAX_EXACT + (jnp.log(nf / REL_MAX_EXACT) / math.log(REL_MAX_DIST / REL_MAX_EXACT)
                             * (REL_BUCKETS - REL_MAX_EXACT)).astype(jnp.int32)
    bucket = jnp.where(n < REL_MAX_EXACT, n, jnp.minimum(large, REL_BUCKETS - 1))
    out = jnp.zeros((SUBLANES, dist.shape[-1]), _F32)
    for b in range(REL_BUCKETS):
        out = jnp.where(bucket == b, rbt[:, b:b + 1], out)
    return out


def _scmp_kernel(pt_ref, *refs, n_pages, group):
    pages = refs[:group]
    q_ref, pet_ref, wt_ref, rbt_ref, oc_ref, imp_ref, kvm_ref = refs[group:]
    pg = pl.program_id(1)
    n_cmp = n_pages * (PAGE_SIZE // CMP_BLOCK)
    per = PAGE_SIZE // CMP_BLOCK * group

    x = jnp.concatenate([p[...] for p in pages], axis=-1)
    kvm_ref[pg] = _split_dot(x, _pool_matrix(group * PAGE_SIZE, CMP_BLOCK, per))

    @pl.when(pg == pl.num_programs(1) - 1)
    def _():
        hd = B_HEAD_DIM
        n_sel = n_cmp // 2
        past = n_pages * PAGE_SIZE
        sums = jnp.concatenate([kvm_ref[s] for s in range(n_pages // group)], axis=-1)
        kv = _compress_cols((sums + _pe_sums_t(pet_ref)) * (1.0 / CMP_BLOCK), wt_ref)
        q8 = q_ref[0] * (hd ** -0.5)
        row = lax.broadcasted_iota(jnp.int32, (SUBLANES, 1), 0)
        head0 = row < B_GROUP
        lc = jnp.where(head0, _dot(q8, kv[0:hd]), _dot(q8, kv[hd:2 * hd]))
        blk = lax.broadcasted_iota(jnp.int32, (1, n_cmp), 1)
        dist = past - (blk * CMP_BLOCK + CMP_BLOCK - 1)
        p = _masked_softmax(jnp.where(dist >= 0, lc + _bias_rows(dist, rbt_ref[...]), NEG_INF))
        oc_ref[0] = jnp.where(head0, _dot_nt(p, kv[2 * hd:3 * hd]), _dot_nt(p, kv[3 * hd:]))
        pool = (lax.broadcasted_iota(jnp.int32, (n_cmp, n_sel), 0) // (SEL_BLOCK // CMP_BLOCK)
                == lax.broadcasted_iota(jnp.int32, (n_cmp, n_sel), 1))
        pp = _dot_exact(p, jnp.where(pool, 1.0, 0.0))
        imp0 = jnp.sum(pp[0:B_GROUP], axis=0, keepdims=True)
        imp1 = jnp.sum(pp[B_GROUP:2 * B_GROUP], axis=0, keepdims=True)
        imp_ref[0] = jnp.where(row == 0, imp0, jnp.where(row == 1, imp1, 0.0))


def _sample_cmp(page_table, cache_t, q8, pe_t, w_t, rbt, l, group=16):
    ns, n_pages = page_table.shape
    n_cmp = n_pages * (PAGE_SIZE // CMP_BLOCK)
    page_spec = lambda k: pl.BlockSpec((None, None, _KV_ROWS, PAGE_SIZE),
                                       lambda b, g, pt: (l, pt[b, g * group + k], 0, 0))
    return pl.pallas_call(
        functools.partial(_scmp_kernel, n_pages=n_pages, group=group),
        grid_spec=pltpu.PrefetchScalarGridSpec(
            num_scalar_prefetch=1,
            grid=(ns, n_pages // group),
            in_specs=[page_spec(k) for k in range(group)] + [
                pl.BlockSpec((1, SUBLANES, B_HEAD_DIM), lambda b, g, pt: (b, 0, 0)),
                pl.BlockSpec((None, 2, B_HEAD_DIM, CMP_BLOCK), lambda b, g, pt: (l, 0, 0, 0)),
                pl.BlockSpec((None, 2, B_HEAD_DIM, B_HEAD_DIM), lambda b, g, pt: (l, 0, 0, 0)),
                pl.BlockSpec((SUBLANES, REL_BUCKETS), lambda b, g, pt: (0, 0))],
            out_specs=[pl.BlockSpec((1, SUBLANES, B_HEAD_DIM), lambda b, g, pt: (b, 0, 0)),
                       pl.BlockSpec((1, SUBLANES, n_cmp // 2), lambda b, g, pt: (b, 0, 0))],
            scratch_shapes=[pltpu.VMEM((n_pages // group, _KV_ROWS, n_cmp * group // n_pages), _F32)]),
        out_shape=[jax.ShapeDtypeStruct((ns, SUBLANES, B_HEAD_DIM), _F32),
                   jax.ShapeDtypeStruct((ns, SUBLANES, n_cmp // 2), _F32)],
        compiler_params=_cparams(("arbitrary", "arbitrary")),
        name="nsa_sample_cmp",
    )(page_table, *([cache_t] * group), q8, pe_t, w_t, rbt)


def _stopk_kernel(imp_ref, idx_ref, *, k):
    s = imp_ref[...]
    n = s.shape[-1]
    lane = lax.broadcasted_iota(jnp.int32, s.shape, 1)
    s = jnp.where(lane == 0, FORCE_SCORE, s)

    def body(it, carry):
        s, out = carry
        m = jnp.max(s, axis=-1, keepdims=True)
        first = jnp.min(jnp.where(s == m, lane, n), axis=-1, keepdims=True)
        return jnp.where(lane == first, -jnp.inf, s), jnp.where(lane == it, first, out)

    _, out = lax.fori_loop(0, k, body, (s, jnp.zeros(s.shape, jnp.int32)))
    idx_ref[...] = out


def _sample_topk(imp2d, k):
    return pl.pallas_call(
        functools.partial(_stopk_kernel, k=k),
        out_shape=jax.ShapeDtypeStruct(imp2d.shape, jnp.int32),
        name="nsa_sample_topk",
    )(imp2d)


def _sattn_kernel(pt_ref, idx_ref, *refs, n_blk, past):
    blocks = refs[:2 * n_blk]
    q_ref, gt_ref, new_ref, oc_ref, win_ref, rbt_ref, o_ref = refs[2 * n_blk:]
    b = pl.program_id(0)
    hd = B_HEAD_DIM
    q8 = q_ref[0] * (hd ** -0.5)
    rbt = rbt_ref[...]
    row = lax.broadcasted_iota(jnp.int32, (SUBLANES, 1), 0)
    head0 = row < B_GROUP
    new = new_ref[0]
    bias0 = _bias_rows(jnp.zeros((1, 1), jnp.int32), rbt)

    def attend(s, v_of_head, k_new, v_new):
        s_new = jnp.sum(q8 * k_new, axis=-1, keepdims=True) + bias0
        m = jnp.maximum(jnp.max(s, axis=-1, keepdims=True), s_new)
        p = jnp.exp(s - m)
        p_new = jnp.exp(s_new - m)
        den = jnp.sum(p, axis=-1, keepdims=True) + p_new
        num = jnp.where(head0, _dot_nt(p, v_of_head(0)), _dot_nt(p, v_of_head(1))) + p_new * v_new
        return num / den

    def per_head(a0, a1):
        return jnp.where(head0, a0, a1)

    per_page = PAGE_SIZE // SEL_BLOCK
    t_in = lax.broadcasted_iota(jnp.int32, (1, PAGE_SIZE), 1)
    s_h, vs = [], []
    for h in range(B_KV_HEADS):
        kt = jnp.concatenate([blocks[h * n_blk + k][h * hd:(h + 1) * hd, :] for k in range(n_blk)], axis=-1)
        vs.append(jnp.concatenate(
            [blocks[h * n_blk + k][B_KV_WIDTH + h * hd:B_KV_WIDTH + (h + 1) * hd, :] for k in range(n_blk)], axis=-1))
        dist = []
        for k in range(n_blk):
            blk = idx_ref[b, h, k]
            in_blk = t_in // SEL_BLOCK == blk % per_page
            dist.append(jnp.where(in_blk, past - ((blk // per_page) * PAGE_SIZE + t_in), -1))
        dist = jnp.concatenate(dist, axis=-1)
        s_h.append(jnp.where(dist >= 0, _dot(q8, kt) + _bias_rows(dist, rbt), NEG_INF))
    ksn = new[:, 2 * B_KV_WIDTH:3 * B_KV_WIDTH]
    vsn = new[:, 3 * B_KV_WIDTH:4 * B_KV_WIDTH]
    o_s = attend(per_head(s_h[0], s_h[1]), lambda h: vs[h],
                 per_head(ksn[:, :hd], ksn[:, hd:]), per_head(vsn[:, :hd], vsn[:, hd:]))

    win = win_ref[0]
    wb = win.shape[1]
    dist = wb - lax.broadcasted_iota(jnp.int32, (1, wb), 1)
    okw = (dist < WINDOW) & (past - dist >= 0)
    lw = per_head(_dot(q8, win[0:hd]), _dot(q8, win[hd:2 * hd]))
    sw = jnp.where(okw, lw + _bias_rows(dist, rbt), NEG_INF)
    kwn = new[:, 4 * B_KV_WIDTH:5 * B_KV_WIDTH]
    vwn = new[:, 5 * B_KV_WIDTH:6 * B_KV_WIDTH]
    o_w = attend(sw, lambda h: win[B_KV_WIDTH + h * hd:B_KV_WIDTH + (h + 1) * hd],
                 per_head(kwn[:, :hd], kwn[:, hd:]), per_head(vwn[:, :hd], vwn[:, hd:]))

    gate = jax.nn.sigmoid(gt_ref[0])
    o_ref[0] = gate[:, 0:1] * oc_ref[0] + gate[:, 1:2] * o_s + gate[:, 2:3] * o_w


def _sample_attn(page_table, idx, cache_t, q8, gate8, kv_new, oc, win_t, rbt, l):
    ns, n_pages = page_table.shape
    n_blk = idx.shape[-1]
    past = n_pages * PAGE_SIZE
    per_page = PAGE_SIZE // SEL_BLOCK

    pages = jnp.take_along_axis(page_table[:, None, :], idx // per_page, axis=-1)

    def blk_spec(h, k):
        return pl.BlockSpec((None, None, _KV_ROWS, PAGE_SIZE), lambda b, pg, ix: (l, pg[b, h, k], 0, 0))

    wb = win_t.shape[3]
    return pl.pallas_call(
        functools.partial(_sattn_kernel, n_blk=n_blk, past=past),
        grid_spec=pltpu.PrefetchScalarGridSpec(
            num_scalar_prefetch=2,
            grid=(ns,),
            in_specs=[blk_spec(h, k) for h in range(B_KV_HEADS) for k in range(n_blk)] + [
                pl.BlockSpec((1, SUBLANES, B_HEAD_DIM), lambda b, pt, ix: (b, 0, 0)),
                pl.BlockSpec((1, SUBLANES, LANES), lambda b, pt, ix: (b, 0, 0)),
                pl.BlockSpec((1, 1, 6 * B_KV_WIDTH), lambda b, pt, ix: (b, 0, 0)),
                pl.BlockSpec((1, SUBLANES, B_HEAD_DIM), lambda b, pt, ix: (b, 0, 0)),
                pl.BlockSpec((None, 1, _KV_ROWS, wb), lambda b, pt, ix: (l, b, 0, 0)),
                pl.BlockSpec((SUBLANES, REL_BUCKETS), lambda b, pt, ix: (0, 0))],
            out_specs=pl.BlockSpec((1, SUBLANES, B_HEAD_DIM), lambda b, pt, ix: (b, 0, 0))),
        out_shape=jax.ShapeDtypeStruct((ns, SUBLANES, B_HEAD_DIM), _F32),
        compiler_params=_cparams(("arbitrary",)),
        name="nsa_sample_attn",
    )(pages, idx, *([cache_t] * (B_KV_HEADS * n_blk)), q8, gate8, kv_new, oc, win_t, rbt)


def _smix_kernel(uv_ref, ng_ref, wd_ref, b0_ref, xbc_ref, st_ref, cw_ref, cb_ref, dt_ref, hp_ref,
                 dsk_ref, ex_ref, oa_ref, v_ref, xdt_ref, ea_ref, y1_ref, bc_ref):
    u, v = _gelu_ln(uv_ref[...], ng_ref[...])
    v_ref[...] = v
    oa_ref[...] = u * (v * wd_ref[...] + b0_ref[...])
    conv = cb_ref[...] + xbc_ref[...] * cw_ref[C_CONV - 1:C_CONV, :]
    for k in range(C_CONV - 1):
        conv = conv + st_ref[k] * cw_ref[k:k + 1, :]
    xc = _silu(conv)
    xs = xc[:, :C_WIDTH]
    bm = xc[:, C_WIDTH:C_WIDTH + C_GROUPS * C_STATE]
    cm = xc[:, C_WIDTH + C_GROUPS * C_STATE:]
    bc_ref[...] = xc[:, C_WIDTH:]
    dt = jax.nn.softplus(dt_ref[...] + hp_ref[0:1, :])
    acum = dt * (-jnp.exp(hp_ref[1:2, :]))
    dt_rep = _dot_exact(dt, ex_ref[...])
    ea_ref[...] = jnp.exp(_dot_exact(acum, ex_ref[...]))
    xdt = dt_rep * xs
    xdt_ref[...] = xdt
    per_g = C_WIDTH // C_GROUPS
    cb = [jnp.sum(cm[:, g * C_STATE:(g + 1) * C_STATE] * bm[:, g * C_STATE:(g + 1) * C_STATE],
                  axis=-1, keepdims=True) for g in range(C_GROUPS)]
    lane = lax.broadcasted_iota(jnp.int32, xs.shape, 1)
    y1_ref[...] = jnp.where(lane < per_g, cb[0], cb[1]) * xdt + dsk_ref[...] * xs


def _sample_mix(uv, ng_row, wd_row, b0_row, xbc, st, conv_w_l, conv_b_row, dt, hp, dsk_row, expand):
    ns = uv.shape[0]
    f = lambda w: jax.ShapeDtypeStruct((ns, w), _F32)
    return pl.pallas_call(
        _smix_kernel,
        out_shape=[f(A_WIDTH), f(A_WIDTH), f(C_WIDTH), f(C_WIDTH), f(C_WIDTH), f(2 * C_GROUPS * C_STATE)],
        name="sample_gmlp_conv",
    )(uv, ng_row, wd_row, b0_row, xbc, st, conv_w_l, conv_b_row, dt, hp, dsk_row, expand)


def _sssm_kernel(h0_ref, xdt_ref, ea_ref, y1_ref, bc_ref, z_ref, ng_ref, o_ref, hout_ref):
    per_g = C_WIDTH // C_GROUPS
    h0 = h0_ref[0]
    bc = bc_ref[0]
    bm = bc[:, :C_GROUPS * C_STATE]
    cm = bc[:, C_GROUPS * C_STATE:]
    c8 = [jnp.broadcast_to(cm[:, g * C_STATE:(g + 1) * C_STATE], (SUBLANES, C_STATE)) for g in range(C_GROUPS)]
    ch = jnp.concatenate([_dot_nt(c8[g], h0[g * per_g:(g + 1) * per_g, :])[0:1] for g in range(C_GROUPS)], axis=-1)
    y = y1_ref[0] + ch * ea_ref[0]
    o_ref[0] = _rms(y * _silu(z_ref[0]), ng_ref[...])
    row = lax.broadcasted_iota(jnp.int32, (LANES, C_WIDTH), 0)
    cols = jnp.where(row == 0, xdt_ref[0], jnp.where(row == 1, ea_ref[0], 0.0)).T
    rsel = lax.broadcasted_iota(jnp.int32, (C_WIDTH, C_STATE), 0) < per_g
    b_full = jnp.where(rsel, bm[:, :C_STATE], bm[:, C_STATE:])
    hout_ref[0] = h0 * cols[:, 1:2] + cols[:, 0:1] * b_full


def _sample_ssm(h0, xdt, ea, y1, bc, z, ng_row, l):
    ns = xdt.shape[0]
    r3 = lambda a: a.reshape(ns, 1, a.shape[-1])
    row_spec = lambda w: pl.BlockSpec((1, 1, w), lambda b: (b, 0, 0))
    out, hout = pl.pallas_call(
        _sssm_kernel,
        grid=(ns,),
        in_specs=[pl.BlockSpec((None, 1, C_WIDTH, C_STATE), lambda b: (l, b, 0, 0)),
                  row_spec(C_WIDTH), row_spec(C_WIDTH), row_spec(C_WIDTH), row_spec(2 * C_GROUPS * C_STATE),
                  row_spec(C_WIDTH), pl.BlockSpec((1, C_WIDTH), lambda b: (0, 0))],
        out_specs=[row_spec(C_WIDTH), pl.BlockSpec((1, C_WIDTH, C_STATE), lambda b: (b, 0, 0))],
        out_shape=[jax.ShapeDtypeStruct((ns, 1, C_WIDTH), _F32),
                   jax.ShapeDtypeStruct((ns, C_WIDTH, C_STATE), _F32)],
        compiler_params=_cparams(("parallel",)),
        name="sample_ssm",
    )(h0, r3(xdt), r3(ea), r3(y1), r3(bc), r3(z), ng_row)
    return out.reshape(ns, C_WIDTH), hout


def _pad_cols(a, width):
    return jnp.pad(a, ((0, 0),) * (a.ndim - 1) + ((0, width - a.shape[-1]),))


def _pack_w_in(w_in):
    o = np.cumsum((0, A_WIDTH, A_WIDTH, B_WIDTH, 6 * B_KV_WIDTH, _GATE_COLS, C_WIDTH, C_CONV_DIM, C_HEADS))
    u0, q0, kv0, gate0, z0, xbc0, dt0, end = o[0], o[2], o[3], o[4], o[5], o[6], o[7], o[8]
    parts = [w_in[..., u0:kv0], w_in[..., z0:xbc0], w_in[..., xbc0:dt0],
             _pad_cols(w_in[..., gate0:z0], LANES), _pad_cols(w_in[..., dt0:end], LANES)]
    w_rows = jnp.concatenate(parts, axis=-1).astype(_MXU)
    w_kv_t = jnp.swapaxes(w_in[..., kv0:gate0], -1, -2).astype(_MXU)
    return w_rows, w_kv_t


def _head_rows(dt_bias, a_log):
    hp = jnp.zeros((DEPTH, SUBLANES, LANES), _F32)
    hp = hp.at[:, 0, :C_HEADS].set(dt_bias)
    return hp.at[:, 1, :C_HEADS].set(a_log)


def kernel(x_prompt, x_sample, cache_cmp_kv, cache_slc_kv, page_table, state_win_kv, state_conv, state_ssm,
           norm_g, ffn_w_gate, ffn_w_up, ffn_w_down, w_in, w_out, gmlp_norm_g, gmlp_w_s, gmlp_b_s,
           nsa_pe_cmp, nsa_w_cmp, rel_bias, conv_w, conv_b, dt_bias, a_log, d_skip, ssm_norm_g):
    bp, t = x_prompt.shape[:2]
    ns = x_sample.shape[0]
    n_pages = page_table.shape[1]
    n_phys = cache_cmp_kv.shape[1]
    tm_p, tm_s = 512, ns
    assert t % _KEY_CHUNK == 0 and t % tm_p == 0

    wg, wu, wd = (w.astype(_MXU) for w in (ffn_w_gate, ffn_w_up, ffn_w_down))
    w_in_p, w_kv_t = _pack_w_in(w_in)
    w_out_b = w_out.astype(_MXU)
    pe_t = jnp.swapaxes(nsa_pe_cmp, -1, -2)
    wc_t = jnp.swapaxes(nsa_w_cmp, -1, -2)
    bs_t = jnp.swapaxes(gmlp_b_s, 1, 2)
    wdiag = jnp.repeat(gmlp_w_s[:, :, 0, 0], A_HEAD_DIM, axis=-1)
    b0 = jnp.repeat(gmlp_b_s[:, :, 0], A_HEAD_DIM, axis=-1)
    hp = _head_rows(dt_bias, a_log)
    dsk = jnp.repeat(d_skip, C_HEAD_DIM, axis=-1)
    expand = (jnp.arange(LANES)[:, None] == jnp.arange(C_WIDTH)[None, :] // C_HEAD_DIM).astype(_F32)
    rb_flat = rel_bias.reshape(-1)
    rbt = _pad_cols(rel_bias, SUBLANES).T
    def chan_major(a):
        lead, tok = a.shape[:-4], a.shape[-4]
        perm = tuple(range(len(lead))) + tuple(len(lead) + k for k in (1, 2, 3, 0))
        return a.transpose(perm).reshape(*lead, _KV_ROWS, tok)

    def token_major(a_t):
        lead, tok = a_t.shape[:-2], a_t.shape[-1]
        a5 = a_t.reshape(*lead, 2, B_KV_HEADS, B_HEAD_DIM, tok)
        perm = tuple(range(len(lead))) + tuple(len(lead) + k for k in (3, 0, 1, 2))
        return a5.transpose(perm)

    cache_c = chan_major(cache_cmp_kv)
    cache_s = chan_major(cache_slc_kv)
    win_state = chan_major(state_win_kv)
    ssm_state = state_ssm.reshape(DEPTH, ns, C_WIDTH, C_STATE)

    xp = x_prompt.reshape(bp * t, D_MODEL)
    xs = x_sample.reshape(ns, D_MODEL)
    outs = [[] for _ in range(11)]
    for l in range(DEPTH):
        g = norm_g[l]
        row = lambda a: a.reshape(1, -1)
        xp = _ffn(xp, g[0:2], wg, wu, wd, l, 0, tm_p)
        xs = _ffn(xs, g[0:2], wg, wu, wd, l, 0, tm_s)

        uv, q, z, xbc, gate, dt, kvc_t, kvs_t, kvw_t = _inproj(xp.reshape(bp, t, D_MODEL), row(g[2]), w_in_p,
                                                               w_kv_t, l, tm_p)
        oa = _gmlp_prompt(uv.reshape(bp * t, 2 * A_WIDTH), row(gmlp_norm_g[l]), gmlp_w_s, bs_t, l)
        kvcmp_t = _compress_prompt(kvc_t, pe_t, wc_t, l)
        ob = _nsa_prompt(q, gate, *_nsa_prompt_inputs(kvcmp_t, kvs_t, kvw_t), rb_flat)
        oc, h_p = _mamba_prompt(z, xbc, dt, conv_w, row(conv_b[l]), hp[l], row(dsk[l]), row(ssm_norm_g[l]), l)
        xp = _outproj(xp, oa, ob.reshape(bp * t, B_WIDTH), oc.reshape(bp * t, C_WIDTH), row(g[3]), w_out_b, l, tm_p)
        wkeep = min(WINDOW, t)
        outs[0].append(token_major(kvc_t))
        outs[1].append(token_major(kvs_t))
        outs[2].append(token_major(kvw_t[:, :, t - wkeep:]))
        outs[3].append(xbc[:, t - (C_CONV - 1):])
        outs[4].append(h_p)

        uv, q, z, xbc, gate, dt, kvc_t, kvs_t, kvw_t = (
            a[0] for a in _inproj(xs.reshape(1, ns, D_MODEL), row(g[2]), w_in_p, w_kv_t, l, tm_s))
        q8 = jnp.pad(q.reshape(ns, B_HEADS, B_HEAD_DIM), ((0, 0), (0, SUBLANES - B_HEADS), (0, 0)))
        gate8 = jnp.pad(gate[:, :_GATE_COLS].reshape(ns, B_HEADS, N_BRANCH),
                        ((0, 0), (0, SUBLANES - B_HEADS), (0, LANES - N_BRANCH)))
        o_cmp, imp = _sample_cmp(page_table, cache_c, q8, pe_t, wc_t, rbt, l)
        n_sel_past = imp.shape[-1]
        idx = _sample_topk(imp.reshape(ns * SUBLANES, n_sel_past), SEL_TOPK - 1)
        idx = idx.reshape(ns, SUBLANES, n_sel_past)[:, :B_KV_HEADS, :SEL_TOPK - 1]
        kvc, kvs, kvwin = kvc_t.T, kvs_t.T, kvw_t.T
        kv_new = jnp.concatenate([kvc, kvs, kvwin], axis=-1).reshape(ns, 1, 6 * B_KV_WIDTH)
        ob8 = _sample_attn(page_table, idx, cache_s, q8, gate8, kv_new, o_cmp, win_state, rbt, l)
        ob = ob8[:, :B_HEADS].reshape(ns, B_WIDTH)
        st = jnp.swapaxes(state_conv[l], 0, 1)
        oa, v_rows, xdt, ea, y1, bc = _sample_mix(uv, row(gmlp_norm_g[l]), row(wdiag[l]), row(b0[l]), xbc, st,
                                                  conv_w[l], row(conv_b[l]), dt, hp[l], row(dsk[l]), expand)
        oc, h_s = _sample_ssm(ssm_state, xdt, ea, y1, bc, z, row(ssm_norm_g[l]), l)
        xs = _outproj(xs, oa, ob, oc, row(g[3]), w_out_b, l, tm_s)
        outs[5].append(kvc.reshape(ns, 1, 2, B_KV_HEADS, B_HEAD_DIM))
        outs[6].append(kvs.reshape(ns, 1, 2, B_KV_HEADS, B_HEAD_DIM))
        outs[7].append(token_major(jnp.concatenate([win_state[l][:, :, 1:], kvwin[:, :, None]], axis=-1)))
        outs[8].append(jnp.concatenate([state_conv[l][:, 1:], xbc[:, None]], axis=1))
        outs[9].append(h_s.reshape(ns, C_HEADS, C_HEAD_DIM, C_STATE))
        outs[10].append(v_rows.reshape(ns, 1, A_WIDTH))

        xp = _ffn(xp, g[4:6], wg, wu, wd, l, 1, tm_p)
        xs = _ffn(xs, g[4:6], wg, wu, wd, l, 1, tm_s)
    stacked = [jnp.stack(o) for o in outs]
    return (xp.reshape(bp, t, D_MODEL), xs.reshape(ns, 1, D_MODEL), *stacked)
```

```python
import functools
import math

import numpy as np
import jax
import jax.numpy as jnp
from jax import lax
from jax.experimental import pallas as pl
from jax.experimental.pallas import tpu as pltpu

D_MODEL = 1024
DEPTH = 4
PAGE_SIZE = 128
A_HEADS, A_HEAD_DIM, A_CHUNK = 4, 64, 128
A_WIDTH = A_HEADS * A_HEAD_DIM
B_HEADS, B_KV_HEADS, B_HEAD_DIM = 6, 2, 64
B_GROUP = B_HEADS // B_KV_HEADS
B_WIDTH = B_HEADS * B_HEAD_DIM
B_KV_WIDTH = B_KV_HEADS * B_HEAD_DIM
N_BRANCH = 3
CMP_BLOCK, SEL_BLOCK, SEL_TOPK, WINDOW, Q_BLOCK = 32, 64, 16, 512, 128
FORCE_SCORE = 1e4
C_HEADS, C_HEAD_DIM, C_GROUPS, C_STATE, C_CONV = 6, 64, 2, 64, 4
C_WIDTH = C_HEADS * C_HEAD_DIM
C_CONV_DIM = C_WIDTH + 2 * C_GROUPS * C_STATE
SSD_CHUNK = 128
D_FF = 2816
REL_BUCKETS, REL_MAX_EXACT, REL_MAX_DIST = 32, 16, 128
EPS = 1e-6
NEG_INF = -1e30
SOFTMAX_FLOOR = -1e20

LANES = 128
SUBLANES = 8
VMEM_LIMIT_BYTES = 56 * 1024 * 1024

_MXU = jnp.bfloat16
_F32 = jnp.float32

_GATE_COLS = N_BRANCH * B_HEADS
_SEG = (("uv", 2 * A_WIDTH), ("q", B_WIDTH), ("z", C_WIDTH), ("xbc", C_CONV_DIM), ("gate", LANES), ("dt", LANES))
_D_IN_PAD = sum(w for _, w in _SEG)
_KV_ROWS = 2 * B_KV_WIDTH
_N_KV = 3


def _dot(a, b):
    return jnp.dot(a.astype(_MXU), b.astype(_MXU), preferred_element_type=_F32)


def _dot_nt(a, b):
    return lax.dot_general(a.astype(_MXU), b.astype(_MXU), (((1,), (1,)), ((), ())),
                           preferred_element_type=_F32)


def _dot_tn(a, b):
    return lax.dot_general(a.astype(_MXU), b.astype(_MXU), (((0,), (0,)), ((), ())),
                           preferred_element_type=_F32)


def _dot_exact(a, b):
    return jnp.dot(a, b, preferred_element_type=_F32, precision=lax.Precision.HIGHEST)


def _rms(x, g):
    return x * lax.rsqrt(jnp.mean(x * x, axis=-1, keepdims=True) + EPS) * g


def _silu(x):
    return x * jax.nn.sigmoid(x)


def _cparams(sem):
    return pltpu.CompilerParams(dimension_semantics=sem, vmem_limit_bytes=VMEM_LIMIT_BYTES)


def _bucket_np(dist):
    n = np.maximum(dist, 0)
    nf = np.maximum(n, 1).astype(np.float32)
    large = REL_MAX_EXACT + (np.log(nf / np.float32(REL_MAX_EXACT))
                             / np.float32(math.log(REL_MAX_DIST / REL_MAX_EXACT))
                             * np.float32(REL_BUCKETS - REL_MAX_EXACT)).astype(np.int32)
    large = np.minimum(large, REL_BUCKETS - 1)
    return np.where(n < REL_MAX_EXACT, n, large).astype(np.int32)


def _ffn_kernel(x_ref, g_ref, wg_ref, wu_ref, wd_ref, o_ref, *, tf):
    x = x_ref[...]
    h = _rms(x, g_ref[0:1, :]).astype(_MXU)
    y = jnp.zeros(x.shape, _F32)
    for f in range(D_FF // tf):
        cols = slice(f * tf, (f + 1) * tf)
        a = _silu(_dot(h, wg_ref[:, cols])) * _dot(h, wu_ref[:, cols])
        y = y + _dot(a, wd_ref[cols, :])
    o_ref[...] = x + 0.5 * _rms(y, g_ref[1:2, :])


def _ffn(x, g2, wg, wu, wd, l, j, tm, tf=256):
    rows = x.shape[0]
    once = pl.Buffered(1)
    return pl.pallas_call(
        functools.partial(_ffn_kernel, tf=tf),
        grid=(rows // tm,),
        in_specs=[pl.BlockSpec((tm, D_MODEL), lambda r: (r, 0)),
                  pl.BlockSpec((2, D_MODEL), lambda r: (0, 0)),
                  pl.BlockSpec((None, None, D_MODEL, D_FF), lambda r: (l, j, 0, 0), pipeline_mode=once),
                  pl.BlockSpec((None, None, D_MODEL, D_FF), lambda r: (l, j, 0, 0), pipeline_mode=once),
                  pl.BlockSpec((None, None, D_FF, D_MODEL), lambda r: (l, j, 0, 0), pipeline_mode=once)],
        out_specs=pl.BlockSpec((tm, D_MODEL), lambda r: (r, 0)),
        out_shape=jax.ShapeDtypeStruct((rows, D_MODEL), _F32),
        compiler_params=_cparams(("parallel",)),
        name="half_ffn",
    )(x, g2, wg, wu, wd)


def _inproj_kernel(x_ref, g_ref, w_ref, wkv_ref, *o_refs):
    h = _rms(x_ref[0], g_ref[...]).astype(_MXU)
    off = 0
    for (_, width), o_ref in zip(_SEG, o_refs):
        o_ref[0] = _dot(h, w_ref[:, off:off + width])
        off += width
    for k, o_ref in enumerate(o_refs[len(_SEG):]):
        o_ref[0] = _dot_nt(wkv_ref[k * _KV_ROWS:(k + 1) * _KV_ROWS, :], h)


def _inproj(x, g_row, w_in_p, w_kv_t, l, tm):
    nb, t = x.shape[:2]
    return pl.pallas_call(
        _inproj_kernel,
        grid=(nb, t // tm),
        in_specs=[pl.BlockSpec((1, tm, D_MODEL), lambda b, r: (b, r, 0)),
                  pl.BlockSpec((1, D_MODEL), lambda b, r: (0, 0)),
                  pl.BlockSpec((None, D_MODEL, _D_IN_PAD), lambda b, r: (l, 0, 0)),
                  pl.BlockSpec((None, _N_KV * _KV_ROWS, D_MODEL), lambda b, r: (l, 0, 0))],
        out_specs=[pl.BlockSpec((1, tm, w), lambda b, r: (b, r, 0)) for _, w in _SEG]
        + [pl.BlockSpec((1, _KV_ROWS, tm), lambda b, r: (b, 0, r))] * _N_KV,
        out_shape=[jax.ShapeDtypeStruct((nb, t, w), _F32) for _, w in _SEG]
        + [jax.ShapeDtypeStruct((nb, _KV_ROWS, t), _F32)] * _N_KV,
        compiler_params=_cparams(("parallel", "parallel")),
        name="in_proj",
    )(x, g_row, w_in_p, w_kv_t)


def _outproj_kernel(x_ref, oa_ref, ob_ref, oc_ref, g_ref, w_ref, o_ref):
    y = (_dot(oa_ref[...], w_ref[0:A_WIDTH, :])
         + _dot(ob_ref[...], w_ref[A_WIDTH:A_WIDTH + B_WIDTH, :])
         + _dot(oc_ref[...], w_ref[A_WIDTH + B_WIDTH:, :]))
    o_ref[...] = x_ref[...] + _rms(y, g_ref[...])


def _outproj(x, oa, ob, oc, g_row, w_out, l, tm):
    rows = x.shape[0]
    return pl.pallas_call(
        _outproj_kernel,
        grid=(rows // tm,),
        in_specs=[pl.BlockSpec((tm, D_MODEL), lambda r: (r, 0)),
                  pl.BlockSpec((tm, A_WIDTH), lambda r: (r, 0)),
                  pl.BlockSpec((tm, B_WIDTH), lambda r: (r, 0)),
                  pl.BlockSpec((tm, C_WIDTH), lambda r: (r, 0)),
                  pl.BlockSpec((1, D_MODEL), lambda r: (0, 0)),
                  pl.BlockSpec((None, D_MODEL, D_MODEL), lambda r: (l, 0, 0))],
        out_specs=pl.BlockSpec((tm, D_MODEL), lambda r: (r, 0)),
        out_shape=jax.ShapeDtypeStruct((rows, D_MODEL), _F32),
        compiler_params=_cparams(("parallel",)),
        name="out_proj",
    )(x, oa, ob, oc, g_row, w_out)


def _gelu_ln(uv, ng):
    u = jax.nn.gelu(uv[:, :A_WIDTH])
    v = jax.nn.gelu(uv[:, A_WIDTH:])
    mu = jnp.mean(v, axis=-1, keepdims=True)
    var = jnp.mean(jnp.square(v - mu), axis=-1, keepdims=True)
    return u, (v - mu) * lax.rsqrt(var + EPS) * ng


def _gmlp_kernel(uv_ref, ng_ref, ws_ref, bs_ref, o_ref, *, chunks):
    row = lax.broadcasted_iota(jnp.int32, (A_CHUNK, A_CHUNK), 0)
    col = lax.broadcasted_iota(jnp.int32, (A_CHUNK, A_CHUNK), 1)
    ws = [jnp.where(col <= row, ws_ref[h], 0.0).astype(_MXU) for h in range(A_HEADS)]
    for c in range(chunks):
        u, v = _gelu_ln(uv_ref[c * A_CHUNK:(c + 1) * A_CHUNK, :], ng_ref[...])
        sg = [_dot(ws[h], v[:, h * A_HEAD_DIM:(h + 1) * A_HEAD_DIM]) + bs_ref[:, h:h + 1]
              for h in range(A_HEADS)]
        o_ref[c * A_CHUNK:(c + 1) * A_CHUNK, :] = u * jnp.concatenate(sg, axis=-1)


def _gmlp_prompt(uv, ng_row, ws, bs_t, l, chunks=4):
    rows = uv.shape[0]
    tm = chunks * A_CHUNK
    return pl.pallas_call(
        functools.partial(_gmlp_kernel, chunks=chunks),
        grid=(rows // tm,),
        in_specs=[pl.BlockSpec((tm, 2 * A_WIDTH), lambda r: (r, 0)),
                  pl.BlockSpec((1, A_WIDTH), lambda r: (0, 0)),
                  pl.BlockSpec((None, A_HEADS, A_CHUNK, A_CHUNK), lambda r: (l, 0, 0, 0)),
                  pl.BlockSpec((None, A_CHUNK, A_HEADS), lambda r: (l, 0, 0))],
        out_specs=pl.BlockSpec((tm, A_WIDTH), lambda r: (r, 0)),
        out_shape=jax.ShapeDtypeStruct((rows, A_WIDTH), _F32),
        compiler_params=_cparams(("parallel",)),
        name="gmlp_prompt",
    )(uv, ng_row, ws, bs_t)


def _mamba_kernel(z_ref, xbc_ref, dt_ref, cw_ref, cb_ref, hp_ref, dsk_ref, ng_ref,
                  o_ref, hout_ref, xp_ref, hs_ref):
    t = pl.program_id(0)
    L = SSD_CHUNK
    hist = SUBLANES

    @pl.when(t == 0)
    def _():
        xp_ref[:, 0:hist, :] = jnp.zeros((xp_ref.shape[0], hist, C_CONV_DIM), _F32)
        hs_ref[...] = jnp.zeros_like(hs_ref)

    a_row = -jnp.exp(hp_ref[1:2, :])
    row = lax.broadcasted_iota(jnp.int32, (L, L), 0)
    col = lax.broadcasted_iota(jnp.int32, (L, L), 1)
    causal = col <= row
    tril = jnp.where(causal, 1.0, 0.0)
    for b in range(z_ref.shape[0]):
        xp_ref[b, hist:hist + L, :] = xbc_ref[b]
        conv = cb_ref[...]
        for k in range(C_CONV):
            conv = conv + xp_ref[b, pl.ds(hist - (C_CONV - 1) + k, L), :] * cw_ref[k:k + 1, :]
        xp_ref[b, 0:hist, :] = xp_ref[b, L:L + hist, :]
        xc = _silu(conv)
        xs = xc[:, :C_WIDTH]
        bm = xc[:, C_WIDTH:C_WIDTH + C_GROUPS * C_STATE]
        cm = xc[:, C_WIDTH + C_GROUPS * C_STATE:]

        dt = jax.nn.softplus(dt_ref[b] + hp_ref[0:1, :])
        acum = _dot_exact(tril, dt * a_row)
        acum_t = acum.T
        dt_t = dt.T
        cb = [_dot_nt(cm[:, g * C_STATE:(g + 1) * C_STATE], bm[:, g * C_STATE:(g + 1) * C_STATE])
              for g in range(C_GROUPS)]
        ys = []
        for h in range(C_HEADS):
            g = h // (C_HEADS // C_GROUPS)
            ac_col = acum[:, h:h + 1]
            seg = ac_col - acum_t[h:h + 1, :]
            decay = jnp.where(causal, jnp.exp(jnp.where(causal, seg, 0.0)), 0.0)
            scores = cb[g] * decay * dt_t[h:h + 1, :]
            x_h = xs[:, h * C_HEAD_DIM:(h + 1) * C_HEAD_DIM]
            b_g = bm[:, g * C_STATE:(g + 1) * C_STATE]
            c_g = cm[:, g * C_STATE:(g + 1) * C_STATE]
            hs = hs_ref[b, h]
            ys.append(_dot(scores, x_h) + _dot_nt(c_g, hs) * jnp.exp(ac_col))
            ac_last = acum[L - 1:L, h:h + 1]
            w_end = jnp.exp(ac_last - ac_col) * dt[:, h:h + 1]
            hs_ref[b, h] = hs * jnp.exp(ac_last) + _dot_tn(x_h * w_end, b_g)
        y = jnp.concatenate(ys, axis=-1) + dsk_ref[...] * xs
        o_ref[b] = _rms(y * _silu(z_ref[b]), ng_ref[...])

    @pl.when(t == pl.num_programs(0) - 1)
    def _():
        hout_ref[...] = hs_ref[...]


def _mamba_prompt(z, xbc, dt, conv_w, conv_b_row, hp, dsk_row, ng_row, l):
    nb, t = z.shape[:2]
    L = SSD_CHUNK
    return pl.pallas_call(
        _mamba_kernel,
        grid=(t // L,),
        in_specs=[pl.BlockSpec((nb, L, C_WIDTH), lambda c: (0, c, 0)),
                  pl.BlockSpec((nb, L, C_CONV_DIM), lambda c: (0, c, 0)),
                  pl.BlockSpec((nb, L, LANES), lambda c: (0, c, 0)),
                  pl.BlockSpec((None, C_CONV, C_CONV_DIM), lambda c: (l, 0, 0)),
                  pl.BlockSpec((1, C_CONV_DIM), lambda c: (0, 0)),
                  pl.BlockSpec((SUBLANES, LANES), lambda c: (0, 0)),
                  pl.BlockSpec((1, C_WIDTH), lambda c: (0, 0)),
                  pl.BlockSpec((1, C_WIDTH), lambda c: (0, 0))],
        out_specs=[pl.BlockSpec((nb, L, C_WIDTH), lambda c: (0, c, 0)),
                   pl.BlockSpec((nb, C_HEADS, C_HEAD_DIM, C_STATE), lambda c: (0, 0, 0, 0))],
        out_shape=[jax.ShapeDtypeStruct((nb, t, C_WIDTH), _F32),
                   jax.ShapeDtypeStruct((nb, C_HEADS, C_HEAD_DIM, C_STATE), _F32)],
        scratch_shapes=[pltpu.VMEM((nb, L + 2 * SUBLANES, C_CONV_DIM), _F32),
                        pltpu.VMEM((nb, C_HEADS, C_HEAD_DIM, C_STATE), _F32)],
        compiler_params=_cparams(("arbitrary",)),
        name="mamba_prompt",
    )(z, xbc, dt, conv_w, conv_b_row, hp, dsk_row, ng_row)


def _split_dot(x, p):
    hi = x.astype(_MXU)
    lo = (x - hi.astype(_F32)).astype(_MXU)
    return jnp.dot(hi, p, preferred_element_type=_F32) + jnp.dot(lo, p, preferred_element_type=_F32)


def _pool_matrix(n_tok, per, n_out, first=0):
    r = lax.broadcasted_iota(jnp.int32, (n_tok, n_out), 0)
    c = lax.broadcasted_iota(jnp.int32, (n_tok, n_out), 1)
    return jnp.where(c == first + r // per, 1.0, 0.0).astype(_MXU)


def _pe_sums_t(pet_ref):
    pk = jnp.sum(pet_ref[0], axis=-1, keepdims=True)
    pv = jnp.sum(pet_ref[1], axis=-1, keepdims=True)
    return jnp.concatenate([pk, pk, pv, pv], axis=0)


def _compress_cols(m, wt_ref):
    hd = B_HEAD_DIM
    return jnp.concatenate([_dot(wt_ref[0], m[0:hd]), _dot(wt_ref[0], m[hd:2 * hd]),
                            _dot(wt_ref[1], m[2 * hd:3 * hd]), _dot(wt_ref[1], m[3 * hd:])], axis=0)


def _compress_kernel(kv_ref, pet_ref, wt_ref, o_ref, *, chunk):
    t = kv_ref.shape[-1]
    pool = _pool_matrix(chunk, CMP_BLOCK, chunk // CMP_BLOCK)
    sums = jnp.concatenate([_split_dot(kv_ref[0, :, c * chunk:(c + 1) * chunk], pool) for c in range(t // chunk)],
                           axis=-1)
    o_ref[0] = _compress_cols((sums + _pe_sums_t(pet_ref)) * (1.0 / CMP_BLOCK), wt_ref)


def _compress_prompt(kvc_t, pe_t, w_t, l, chunk=2048):
    nb, _, t = kvc_t.shape
    return pl.pallas_call(
        functools.partial(_compress_kernel, chunk=min(chunk, t)),
        grid=(nb,),
        in_specs=[pl.BlockSpec((1, _KV_ROWS, t), lambda b: (b, 0, 0)),
                  pl.BlockSpec((None, 2, B_HEAD_DIM, CMP_BLOCK), lambda b: (l, 0, 0, 0)),
                  pl.BlockSpec((None, 2, B_HEAD_DIM, B_HEAD_DIM), lambda b: (l, 0, 0, 0))],
        out_specs=pl.BlockSpec((1, _KV_ROWS, t // CMP_BLOCK), lambda b: (b, 0, 0)),
        out_shape=jax.ShapeDtypeStruct((nb, _KV_ROWS, t // CMP_BLOCK), _F32),
        compiler_params=_cparams(("parallel",)),
        name="nsa_compress_prompt",
    )(kvc_t, pe_t, w_t)


_KEY_CHUNK = 512
_FRONT_PAD = Q_BLOCK
_NEAR = 2 * Q_BLOCK
_MAIN_UNROLL = 2


def _nsa_tables(n_sel):
    r = np.arange(Q_BLOCK)[:, None]
    cmp_idx = np.full((2, Q_BLOCK, n_sel), REL_BUCKETS - 1, np.int32)
    for par in range(2):
        for u in (-2, -1, 0, 1):
            dist = r[:, 0] - (CMP_BLOCK - 1) - CMP_BLOCK * (2 * u + par)
            cmp_idx[par, :, u % n_sel] = _bucket_np(dist)
    c = np.arange(_NEAR)[None, :]
    dist = Q_BLOCK + r - c
    near_idx = np.where(dist >= 0, _bucket_np(dist), -1).astype(np.int32)
    c = np.arange(WINDOW + Q_BLOCK)[None, :]
    dist = r + WINDOW - c
    win_idx = np.where((dist >= 0) & (dist < WINDOW), _bucket_np(dist), -1).astype(np.int32)
    return cmp_idx, near_idx, win_idx


def _fill_bias(idx, rb_ref, hg, rel_to_last):
    base = rb_ref[(REL_BUCKETS - 1) * B_HEADS + hg] if rel_to_last else 0.0
    tile = jnp.where(idx < 0, NEG_INF, 0.0)
    for b in range(REL_BUCKETS):
        tile = jnp.where(idx == b, rb_ref[b * B_HEADS + hg] - base, tile)
    return tile


def _topk_mask(score, k, taken):
    n = float(score.shape[-1])
    lane = lax.broadcasted_iota(jnp.int32, score.shape, 1).astype(_F32)
    s = jnp.where(taken, -jnp.inf, score)
    for _ in range(k):
        m = jnp.max(s, axis=-1, keepdims=True)
        first = jnp.min(jnp.where(s == m, lane, n), axis=-1, keepdims=True)
        s = jnp.where(lane == first, -jnp.inf, s)
    return s == -jnp.inf


def _masked_softmax(l):
    m = jnp.maximum(jnp.max(l, axis=-1, keepdims=True), SOFTMAX_FLOOR)
    e = jnp.exp(l - m)
    return e / jnp.maximum(jnp.sum(e, axis=-1, keepdims=True), 1e-20)


def _online_step(s, vt, carry):
    m_run, l_run, acc = carry
    m_new = jnp.maximum(m_run, jnp.max(s, axis=-1, keepdims=True))
    alpha = jnp.exp(m_run - m_new)
    p = jnp.exp(s - m_new)
    return (m_new, alpha * l_run + jnp.sum(p, axis=-1, keepdims=True), alpha * acc + _dot_nt(p, vt))


def _key_tiles(ref, first, n, rows=slice(None)):
    return jnp.concatenate([ref[0, first + j, rows, :] for j in range(n)], axis=-1)


def _nsa_kernel(q_ref, gt_ref, kc_ref, vc_ref, ksa_ref, vs_ref, kwa_ref, vwa_ref,
                cidx_ref, nidx_ref, widx_ref, rb_ref, o_ref, bc_ref, bn_ref, bw_ref, *, n_sel):
    i = pl.program_id(1)
    hd, G = B_HEAD_DIM, B_GROUP
    QB = Q_BLOCK

    @pl.when((pl.program_id(0) == 0) & (i == 0))
    def _():
        for hg in range(B_HEADS):
            for par in range(2):
                bc_ref[hg, par] = _fill_bias(cidx_ref[par], rb_ref, hg, False)
            bn_ref[hg] = _fill_bias(nidx_ref[...], rb_ref, hg, True)
            bw_ref[hg] = _fill_bias(widx_ref[...], rb_ref, hg, False)

    q = q_ref[0] * (hd ** -0.5)
    gate = jax.nn.sigmoid(gt_ref[0])
    r_col = lax.broadcasted_iota(jnp.int32, (QB, 1), 0)
    qpos = i * QB + r_col
    n_lane = lax.broadcasted_iota(jnp.int32, (QB, n_sel), 1)
    cur = qpos // SEL_BLOCK
    vis = [CMP_BLOCK * (2 * n_lane + par) + (CMP_BLOCK - 1) <= qpos for par in range(2)]
    vis3 = jnp.concatenate([jnp.concatenate(vis, axis=-1)] * G, axis=0)
    c_near = lax.broadcasted_iota(jnp.int32, (QB, _NEAR), 1)
    near_ok = jnp.concatenate([c_near >= _FRONT_PAD - i * QB] * G, axis=0)
    c_win = lax.broadcasted_iota(jnp.int32, (QB, WINDOW + QB), 1)
    win_ok = jnp.concatenate([c_win >= WINDOW - i * QB] * G, axis=0)
    n_main = (jnp.maximum(i - 1, 0) * QB + _KEY_CHUNK - 1) // _KEY_CHUNK
    zeros_h = jnp.zeros((G * QB, hd), _F32)

    heads = range(B_KV_HEADS)
    hrows = [slice(h * hd, (h + 1) * hd) for h in heads]
    q3 = [jnp.concatenate([q[:, (h * G + g) * hd:(h * G + g + 1) * hd] for g in range(G)], axis=0) for h in heads]
    q3h = [jnp.concatenate([q3[h], zeros_h] if h == 0 else [zeros_h, q3[h]], axis=-1) for h in heads]
    t0 = i * (QB // LANES)

    forced = (n_lane == cur) | (n_lane == 0)
    started = n_lane <= cur
    o_c, imp = [], []
    for h in heads:
        lc = _dot(q3[h], kc_ref[0, hrows[h], :])
        bias_c = jnp.concatenate(
            [jnp.concatenate([pltpu.roll(bc_ref[h * G + g, par], 2 * i, 1) for par in range(2)], axis=-1)
             for g in range(G)], axis=0)
        p_c = _masked_softmax(jnp.where(vis3, lc + bias_c, NEG_INF))
        o_c.append(_dot_nt(p_c, vc_ref[0, hrows[h], :]))
        imp_h = sum(p_c[g * QB:(g + 1) * QB, :n_sel] + p_c[g * QB:(g + 1) * QB, n_sel:] for g in range(G))
        imp.append(jnp.where(started, imp_h, -1.0))

    chosen = _topk_mask(jnp.concatenate(imp, axis=0), min(SEL_TOPK, n_sel) - 2,
                        jnp.concatenate([forced] * B_KV_HEADS, axis=0))
    n_win = (WINDOW + QB) // LANES
    kw = _key_tiles(kwa_ref, t0, n_win)
    o_w = []
    for h in heads:
        bias_w = jnp.concatenate([bw_ref[h * G + g] for g in range(G)], axis=0)
        p_w = _masked_softmax(jnp.where(win_ok, _dot(q3h[h], kw) + bias_w, NEG_INF))
        o_w.append(_dot_nt(p_w, _key_tiles(vwa_ref, t0, n_win, hrows[h])))

    qa_main, qa_near = [], []
    for h in heads:
        allowed = chosen[h * QB:(h + 1) * QB] & started
        m_main = jnp.where(allowed & (n_lane < 2 * i - 2), 0.0, NEG_INF)
        m_near = jnp.where(allowed, 0.0, NEG_INF)
        qa_main.append(jnp.concatenate([q3h[h], jnp.concatenate([m_main] * G, axis=0)], axis=-1).astype(_MXU))
        qa_near.append(jnp.concatenate([q3h[h], jnp.concatenate([m_near] * G, axis=0)], axis=-1).astype(_MXU))

    per_chunk = _KEY_CHUNK // LANES

    def main_body(c, carry):
        for u in range(_MAIN_UNROLL):
            tc = _FRONT_PAD // LANES + (c * _MAIN_UNROLL + u) * per_chunk
            ks = _key_tiles(ksa_ref, tc, per_chunk)
            carry = tuple(_online_step(_dot(qa_main[h], ks), _key_tiles(vs_ref, tc, per_chunk, hrows[h]), carry[h])
                          for h in heads)
        return carry

    init = (jnp.full((G * QB, 1), SOFTMAX_FLOOR, _F32), jnp.zeros((G * QB, 1), _F32), zeros_h)
    carry = lax.fori_loop(0, (n_main + _MAIN_UNROLL - 1) // _MAIN_UNROLL, main_body, (init,) * B_KV_HEADS)
    ks = _key_tiles(ksa_ref, t0, _NEAR // LANES)
    outs = []
    for h in heads:
        corr = jnp.concatenate([bn_ref[h * G + g] for g in range(G)], axis=0)
        s = jnp.where(near_ok, _dot(qa_near[h], ks) + corr, NEG_INF)
        _, l_s, acc_s = _online_step(s, _key_tiles(vs_ref, t0, _NEAR // LANES, hrows[h]), carry[h])
        o_s = acc_s / l_s
        for g in range(G):
            k0 = (h * G + g) * N_BRANCH
            rows = slice(g * QB, (g + 1) * QB)
            outs.append(gate[:, k0:k0 + 1] * o_c[h][rows] + gate[:, k0 + 1:k0 + 2] * o_s[rows]
                        + gate[:, k0 + 2:k0 + 3] * o_w[h][rows])
    o_ref[0] = jnp.concatenate(outs, axis=-1)


def _nsa_prompt(q, gate, kcp, vcp, ksa, vs, kwa, vwa, rb_flat):
    nb, t = q.shape[:2]
    n_sel = t // SEL_BLOCK
    cidx, nidx, widx = _nsa_tables(n_sel)
    tp, tw = ksa.shape[1], kwa.shape[1]
    full = lambda shape: pl.BlockSpec(shape, lambda b, i: (0,) * len(shape))
    return pl.pallas_call(
        functools.partial(_nsa_kernel, n_sel=n_sel),
        grid=(nb, t // Q_BLOCK),
        in_specs=[pl.BlockSpec((1, Q_BLOCK, B_WIDTH), lambda b, i: (b, i, 0)),
                  pl.BlockSpec((1, Q_BLOCK, LANES), lambda b, i: (b, i, 0)),
                  pl.BlockSpec((1, B_KV_WIDTH, 2 * n_sel), lambda b, i: (b, 0, 0)),
                  pl.BlockSpec((1, B_KV_WIDTH, 2 * n_sel), lambda b, i: (b, 0, 0)),
                  pl.BlockSpec((1, tp, B_KV_WIDTH + n_sel, LANES), lambda b, i: (b, 0, 0, 0)),
                  pl.BlockSpec((1, tp, B_KV_WIDTH, LANES), lambda b, i: (b, 0, 0, 0)),
                  pl.BlockSpec((1, tw, B_KV_WIDTH, LANES), lambda b, i: (b, 0, 0, 0)),
                  pl.BlockSpec((1, tw, B_KV_WIDTH, LANES), lambda b, i: (b, 0, 0, 0)),
                  full(cidx.shape), full(nidx.shape), full(widx.shape),
                  pl.BlockSpec(memory_space=pltpu.SMEM)],
        out_specs=pl.BlockSpec((1, Q_BLOCK, B_WIDTH), lambda b, i: (b, i, 0)),
        out_shape=jax.ShapeDtypeStruct((nb, t, B_WIDTH), _F32),
        scratch_shapes=[pltpu.VMEM((B_HEADS, 2, Q_BLOCK, n_sel), _F32),
                        pltpu.VMEM((B_HEADS, Q_BLOCK, _NEAR), _F32),
                        pltpu.VMEM((B_HEADS, Q_BLOCK, WINDOW + Q_BLOCK), _F32)],
        compiler_params=_cparams(("arbitrary", "arbitrary")),
        name="nsa_prompt",
    )(q, gate, kcp, vcp, ksa, vs, kwa, vwa, jnp.asarray(cidx), jnp.asarray(nidx), jnp.asarray(widx), rb_flat)


def _nsa_prompt_inputs(kvcmp_t, kvs_t, kvw_t):
    nb, _, t = kvs_t.shape
    n_sel = t // SEL_BLOCK
    kvp = kvcmp_t.reshape(nb, _KV_ROWS, n_sel, 2).transpose(0, 1, 3, 2).reshape(nb, _KV_ROWS, 2 * n_sel)
    kcp = kvp[:, :B_KV_WIDTH].astype(_MXU)
    vcp = kvp[:, B_KV_WIDTH:].astype(_MXU)
    blk = (jnp.arange(n_sel)[:, None] == jnp.arange(t)[None, :] // SEL_BLOCK).astype(_MXU)
    ksa = jnp.concatenate([kvs_t[:, :B_KV_WIDTH].astype(_MXU), jnp.broadcast_to(blk, (nb, n_sel, t))], axis=1)
    pad_s = ((0, 0), (0, 0), (_FRONT_PAD, _MAIN_UNROLL * _KEY_CHUNK - _FRONT_PAD))
    pad_w = ((0, 0), (0, 0), (WINDOW, 0))
    def tiles(a, pad):
        a = jnp.pad(a, pad)
        return a.reshape(nb, a.shape[1], a.shape[2] // LANES, LANES).transpose(0, 2, 1, 3)

    vs = tiles(kvs_t[:, B_KV_WIDTH:].astype(_MXU), pad_s)
    kwa = tiles(kvw_t[:, :B_KV_WIDTH].astype(_MXU), pad_w)
    vwa = tiles(kvw_t[:, B_KV_WIDTH:].astype(_MXU), pad_w)
    return kcp, vcp, tiles(ksa, pad_s), vs, kwa, vwa


def _bias_rows(dist, rbt):
    n = jnp.maximum(dist, 0)
    nf = jnp.maximum(n, 1).astype(_F32)
    large = REL_MAX_EXACT + (jnp.log(nf / REL_MAX_EXACT) / math.log(REL_MAX_DIST / REL_MAX_EXACT)
                             * (REL_BUCKETS - REL_MAX_EXACT)).astype(jnp.int32)
    bucket = jnp.where(n < REL_MAX_EXACT, n, jnp.minimum(large, REL_BUCKETS - 1))
    out = jnp.zeros((SUBLANES, dist.shape[-1]), _F32)
    for b in range(REL_BUCKETS):
        out = jnp.where(bucket == b, rbt[:, b:b + 1], out)
    return out


def _scmp_kernel(pt_ref, *refs, n_pages, group):
    pages = refs[:group]
    q_ref, pet_ref, wt_ref, rbt_ref, oc_ref, imp_ref, kvm_ref = refs[group:]
    pg = pl.program_id(1)
    n_cmp = n_pages * (PAGE_SIZE // CMP_BLOCK)
    per = PAGE_SIZE // CMP_BLOCK * group

    x = jnp.concatenate([p[...] for p in pages], axis=-1)
    kvm_ref[pg] = _split_dot(x, _pool_matrix(group * PAGE_SIZE, CMP_BLOCK, per))

    @pl.when(pg == pl.num_programs(1) - 1)
    def _():
        hd = B_HEAD_DIM
        n_sel = n_cmp // 2
        past = n_pages * PAGE_SIZE
        sums = jnp.concatenate([kvm_ref[s] for s in range(n_pages // group)], axis=-1)
        kv = _compress_cols((sums + _pe_sums_t(pet_ref)) * (1.0 / CMP_BLOCK), wt_ref)
        q8 = q_ref[0] * (hd ** -0.5)
        row = lax.broadcasted_iota(jnp.int32, (SUBLANES, 1), 0)
        head0 = row < B_GROUP
        lc = jnp.where(head0, _dot(q8, kv[0:hd]), _dot(q8, kv[hd:2 * hd]))
        blk = lax.broadcasted_iota(jnp.int32, (1, n_cmp), 1)
        dist = past - (blk * CMP_BLOCK + CMP_BLOCK - 1)
        p = _masked_softmax(jnp.where(dist >= 0, lc + _bias_rows(dist, rbt_ref[...]), NEG_INF))
        oc_ref[0] = jnp.where(head0, _dot_nt(p, kv[2 * hd:3 * hd]), _dot_nt(p, kv[3 * hd:]))
        pool = (lax.broadcasted_iota(jnp.int32, (n_cmp, n_sel), 0) // (SEL_BLOCK // CMP_BLOCK)
                == lax.broadcasted_iota(jnp.int32, (n_cmp, n_sel), 1))
        pp = _dot_exact(p, jnp.where(pool, 1.0, 0.0))
        imp0 = jnp.sum(pp[0:B_GROUP], axis=0, keepdims=True)
        imp1 = jnp.sum(pp[B_GROUP:2 * B_GROUP], axis=0, keepdims=True)
        imp_ref[0] = jnp.where(row == 0, imp0, jnp.where(row == 1, imp1, 0.0))


def _sample_cmp(page_table, cache_t, q8, pe_t, w_t, rbt, l, group=16):
    ns, n_pages = page_table.shape
    n_cmp = n_pages * (PAGE_SIZE // CMP_BLOCK)
    page_spec = lambda k: pl.BlockSpec((None, None, _KV_ROWS, PAGE_SIZE),
                                       lambda b, g, pt: (l, pt[b, g * group + k], 0, 0))
    return pl.pallas_call(
        functools.partial(_scmp_kernel, n_pages=n_pages, group=group),
        grid_spec=pltpu.PrefetchScalarGridSpec(
            num_scalar_prefetch=1,
            grid=(ns, n_pages // group),
            in_specs=[page_spec(k) for k in range(group)] + [
                pl.BlockSpec((1, SUBLANES, B_HEAD_DIM), lambda b, g, pt: (b, 0, 0)),
                pl.BlockSpec((None, 2, B_HEAD_DIM, CMP_BLOCK), lambda b, g, pt: (l, 0, 0, 0)),
                pl.BlockSpec((None, 2, B_HEAD_DIM, B_HEAD_DIM), lambda b, g, pt: (l, 0, 0, 0)),
                pl.BlockSpec((SUBLANES, REL_BUCKETS), lambda b, g, pt: (0, 0))],
            out_specs=[pl.BlockSpec((1, SUBLANES, B_HEAD_DIM), lambda b, g, pt: (b, 0, 0)),
                       pl.BlockSpec((1, SUBLANES, n_cmp // 2), lambda b, g, pt: (b, 0, 0))],
            scratch_shapes=[pltpu.VMEM((n_pages // group, _KV_ROWS, n_cmp * group // n_pages), _F32)]),
        out_shape=[jax.ShapeDtypeStruct((ns, SUBLANES, B_HEAD_DIM), _F32),
                   jax.ShapeDtypeStruct((ns, SUBLANES, n_cmp // 2), _F32)],
        compiler_params=_cparams(("arbitrary", "arbitrary")),
        name="nsa_sample_cmp",
    )(page_table, *([cache_t] * group), q8, pe_t, w_t, rbt)


def _stopk_kernel(imp_ref, idx_ref, *, k):
    s = imp_ref[...]
    n = s.shape[-1]
    lane = lax.broadcasted_iota(jnp.int32, s.shape, 1)
    s = jnp.where(lane == 0, FORCE_SCORE, s)

    def body(it, carry):
        s, out = carry
        m = jnp.max(s, axis=-1, keepdims=True)
        first = jnp.min(jnp.where(s == m, lane, n), axis=-1, keepdims=True)
        return jnp.where(lane == first, -jnp.inf, s), jnp.where(lane == it, first, out)

    _, out = lax.fori_loop(0, k, body, (s, jnp.zeros(s.shape, jnp.int32)))
    idx_ref[...] = out


def _sample_topk(imp2d, k):
    return pl.pallas_call(
        functools.partial(_stopk_kernel, k=k),
        out_shape=jax.ShapeDtypeStruct(imp2d.shape, jnp.int32),
        name="nsa_sample_topk",
    )(imp2d)


def _sattn_kernel(pt_ref, idx_ref, *refs, n_blk, past):
    blocks = refs[:2 * n_blk]
    q_ref, gt_ref, new_ref, oc_ref, win_ref, rbt_ref, o_ref = refs[2 * n_blk:]
    b = pl.program_id(0)
    hd = B_HEAD_DIM
    q8 = q_ref[0] * (hd ** -0.5)
    rbt = rbt_ref[...]
    row = lax.broadcasted_iota(jnp.int32, (SUBLANES, 1), 0)
    head0 = row < B_GROUP
    new = new_ref[0]
    bias0 = _bias_rows(jnp.zeros((1, 1), jnp.int32), rbt)

    def attend(s, v_of_head, k_new, v_new):
        s_new = jnp.sum(q8 * k_new, axis=-1, keepdims=True) + bias0
        m = jnp.maximum(jnp.max(s, axis=-1, keepdims=True), s_new)
        p = jnp.exp(s - m)
        p_new = jnp.exp(s_new - m)
        den = jnp.sum(p, axis=-1, keepdims=True) + p_new
        num = jnp.where(head0, _dot_nt(p, v_of_head(0)), _dot_nt(p, v_of_head(1))) + p_new * v_new
        return num / den

    def per_head(a0, a1):
        return jnp.where(head0, a0, a1)

    per_page = PAGE_SIZE // SEL_BLOCK
    t_in = lax.broadcasted_iota(jnp.int32, (1, PAGE_SIZE), 1)
    s_h, vs = [], []
    for h in range(B_KV_HEADS):
        kt = jnp.concatenate([blocks[h * n_blk + k][h * hd:(h + 1) * hd, :] for k in range(n_blk)], axis=-1)
        vs.append(jnp.concatenate(
            [blocks[h * n_blk + k][B_KV_WIDTH + h * hd:B_KV_WIDTH + (h + 1) * hd, :] for k in range(n_blk)], axis=-1))
        dist = []
        for k in range(n_blk):
            blk = idx_ref[b, h, k]
            in_blk = t_in // SEL_BLOCK == blk % per_page
            dist.append(jnp.where(in_blk, past - ((blk // per_page) * PAGE_SIZE + t_in), -1))
        dist = jnp.concatenate(dist, axis=-1)
        s_h.append(jnp.where(dist >= 0, _dot(q8, kt) + _bias_rows(dist, rbt), NEG_INF))
    ksn = new[:, 2 * B_KV_WIDTH:3 * B_KV_WIDTH]
    vsn = new[:, 3 * B_KV_WIDTH:4 * B_KV_WIDTH]
    o_s = attend(per_head(s_h[0], s_h[1]), lambda h: vs[h],
                 per_head(ksn[:, :hd], ksn[:, hd:]), per_head(vsn[:, :hd], vsn[:, hd:]))

    win = win_ref[0]
    wb = win.shape[1]
    dist = wb - lax.broadcasted_iota(jnp.int32, (1, wb), 1)
    okw = (dist < WINDOW) & (past - dist >= 0)
    lw = per_head(_dot(q8, win[0:hd]), _dot(q8, win[hd:2 * hd]))
    sw = jnp.where(okw, lw + _bias_rows(dist, rbt), NEG_INF)
    kwn = new[:, 4 * B_KV_WIDTH:5 * B_KV_WIDTH]
    vwn = new[:, 5 * B_KV_WIDTH:6 * B_KV_WIDTH]
    o_w = attend(sw, lambda h: win[B_KV_WIDTH + h * hd:B_KV_WIDTH + (h + 1) * hd],
                 per_head(kwn[:, :hd], kwn[:, hd:]), per_head(vwn[:, :hd], vwn[:, hd:]))

    gate = jax.nn.sigmoid(gt_ref[0])
    o_ref[0] = gate[:, 0:1] * oc_ref[0] + gate[:, 1:2] * o_s + gate[:, 2:3] * o_w


def _sample_attn(page_table, idx, cache_t, q8, gate8, kv_new, oc, win_t, rbt, l):
    ns, n_pages = page_table.shape
    n_blk = idx.shape[-1]
    past = n_pages * PAGE_SIZE
    per_page = PAGE_SIZE // SEL_BLOCK

    pages = jnp.take_along_axis(page_table[:, None, :], idx // per_page, axis=-1)

    def blk_spec(h, k):
        return pl.BlockSpec((None, None, _KV_ROWS, PAGE_SIZE), lambda b, pg, ix: (l, pg[b, h, k], 0, 0))

    wb = win_t.shape[3]
    return pl.pallas_call(
        functools.partial(_sattn_kernel, n_blk=n_blk, past=past),
        grid_spec=pltpu.PrefetchScalarGridSpec(
            num_scalar_prefetch=2,
            grid=(ns,),
            in_specs=[blk_spec(h, k) for h in range(B_KV_HEADS) for k in range(n_blk)] + [
                pl.BlockSpec((1, SUBLANES, B_HEAD_DIM), lambda b, pt, ix: (b, 0, 0)),
                pl.BlockSpec((1, SUBLANES, LANES), lambda b, pt, ix: (b, 0, 0)),
                pl.BlockSpec((1, 1, 6 * B_KV_WIDTH), lambda b, pt, ix: (b, 0, 0)),
                pl.BlockSpec((1, SUBLANES, B_HEAD_DIM), lambda b, pt, ix: (b, 0, 0)),
                pl.BlockSpec((None, 1, _KV_ROWS, wb), lambda b, pt, ix: (l, b, 0, 0)),
                pl.BlockSpec((SUBLANES, REL_BUCKETS), lambda b, pt, ix: (0, 0))],
            out_specs=pl.BlockSpec((1, SUBLANES, B_HEAD_DIM), lambda b, pt, ix: (b, 0, 0))),
        out_shape=jax.ShapeDtypeStruct((ns, SUBLANES, B_HEAD_DIM), _F32),
        compiler_params=_cparams(("arbitrary",)),
        name="nsa_sample_attn",
    )(pages, idx, *([cache_t] * (B_KV_HEADS * n_blk)), q8, gate8, kv_new, oc, win_t, rbt)


def _smix_kernel(uv_ref, ng_ref, wd_ref, b0_ref, xbc_ref, st_ref, cw_ref, cb_ref, dt_ref, hp_ref,
                 dsk_ref, ex_ref, oa_ref, v_ref, xdt_ref, ea_ref, y1_ref, bc_ref):
    u, v = _gelu_ln(uv_ref[...], ng_ref[...])
    v_ref[...] = v
    oa_ref[...] = u * (v * wd_ref[...] + b0_ref[...])
    conv = cb_ref[...] + xbc_ref[...] * cw_ref[C_CONV - 1:C_CONV, :]
    for k in range(C_CONV - 1):
        conv = conv + st_ref[k] * cw_ref[k:k + 1, :]
    xc = _silu(conv)
    xs = xc[:, :C_WIDTH]
    bm = xc[:, C_WIDTH:C_WIDTH + C_GROUPS * C_STATE]
    cm = xc[:, C_WIDTH + C_GROUPS * C_STATE:]
    bc_ref[...] = xc[:, C_WIDTH:]
    dt = jax.nn.softplus(dt_ref[...] + hp_ref[0:1, :])
    acum = dt * (-jnp.exp(hp_ref[1:2, :]))
    dt_rep = _dot_exact(dt, ex_ref[...])
    ea_ref[...] = jnp.exp(_dot_exact(acum, ex_ref[...]))
    xdt = dt_rep * xs
    xdt_ref[...] = xdt
    per_g = C_WIDTH // C_GROUPS
    cb = [jnp.sum(cm[:, g * C_STATE:(g + 1) * C_STATE] * bm[:, g * C_STATE:(g + 1) * C_STATE],
                  axis=-1, keepdims=True) for g in range(C_GROUPS)]
    lane = lax.broadcasted_iota(jnp.int32, xs.shape, 1)
    y1_ref[...] = jnp.where(lane < per_g, cb[0], cb[1]) * xdt + dsk_ref[...] * xs


def _sample_mix(uv, ng_row, wd_row, b0_row, xbc, st, conv_w_l, conv_b_row, dt, hp, dsk_row, expand):
    ns = uv.shape[0]
    f = lambda w: jax.ShapeDtypeStruct((ns, w), _F32)
    return pl.pallas_call(
        _smix_kernel,
        out_shape=[f(A_WIDTH), f(A_WIDTH), f(C_WIDTH), f(C_WIDTH), f(C_WIDTH), f(2 * C_GROUPS * C_STATE)],
        name="sample_gmlp_conv",
    )(uv, ng_row, wd_row, b0_row, xbc, st, conv_w_l, conv_b_row, dt, hp, dsk_row, expand)


def _sssm_kernel(h0_ref, xdt_ref, ea_ref, y1_ref, bc_ref, z_ref, ng_ref, o_ref, hout_ref):
    per_g = C_WIDTH // C_GROUPS
    h0 = h0_ref[0]
    bc = bc_ref[0]
    bm = bc[:, :C_GROUPS * C_STATE]
    cm = bc[:, C_GROUPS * C_STATE:]
    c8 = [jnp.broadcast_to(cm[:, g * C_STATE:(g + 1) * C_STATE], (SUBLANES, C_STATE)) for g in range(C_GROUPS)]
    ch = jnp.concatenate([_dot_nt(c8[g], h0[g * per_g:(g + 1) * per_g, :])[0:1] for g in range(C_GROUPS)], axis=-1)
    y = y1_ref[0] + ch * ea_ref[0]
    o_ref[0] = _rms(y * _silu(z_ref[0]), ng_ref[...])
    row = lax.broadcasted_iota(jnp.int32, (LANES, C_WIDTH), 0)
    cols = jnp.where(row == 0, xdt_ref[0], jnp.where(row == 1, ea_ref[0], 0.0)).T
    rsel = lax.broadcasted_iota(jnp.int32, (C_WIDTH, C_STATE), 0) < per_g
    b_full = jnp.where(rsel, bm[:, :C_STATE], bm[:, C_STATE:])
    hout_ref[0] = h0 * cols[:, 1:2] + cols[:, 0:1] * b_full


def _sample_ssm(h0, xdt, ea, y1, bc, z, ng_row, l):
    ns = xdt.shape[0]
    r3 = lambda a: a.reshape(ns, 1, a.shape[-1])
    row_spec = lambda w: pl.BlockSpec((1, 1, w), lambda b: (b, 0, 0))
    out, hout = pl.pallas_call(
        _sssm_kernel,
        grid=(ns,),
        in_specs=[pl.BlockSpec((None, 1, C_WIDTH, C_STATE), lambda b: (l, b, 0, 0)),
                  row_spec(C_WIDTH), row_spec(C_WIDTH), row_spec(C_WIDTH), row_spec(2 * C_GROUPS * C_STATE),
                  row_spec(C_WIDTH), pl.BlockSpec((1, C_WIDTH), lambda b: (0, 0))],
        out_specs=[row_spec(C_WIDTH), pl.BlockSpec((1, C_WIDTH, C_STATE), lambda b: (b, 0, 0))],
        out_shape=[jax.ShapeDtypeStruct((ns, 1, C_WIDTH), _F32),
                   jax.ShapeDtypeStruct((ns, C_WIDTH, C_STATE), _F32)],
        compiler_params=_cparams(("parallel",)),
        name="sample_ssm",
    )(h0, r3(xdt), r3(ea), r3(y1), r3(bc), r3(z), ng_row)
    return out.reshape(ns, C_WIDTH), hout


def _pad_cols(a, width):
    return jnp.pad(a, ((0, 0),) * (a.ndim - 1) + ((0, width - a.shape[-1]),))


def _pack_w_in(w_in):
    o = np.cumsum((0, A_WIDTH, A_WIDTH, B_WIDTH, 6 * B_KV_WIDTH, _GATE_COLS, C_WIDTH, C_CONV_DIM, C_HEADS))
    u0, q0, kv0, gate0, z0, xbc0, dt0, end = o[0], o[2], o[3], o[4], o[5], o[6], o[7], o[8]
    parts = [w_in[..., u0:kv0], w_in[..., z0:xbc0], w_in[..., xbc0:dt0],
             _pad_cols(w_in[..., gate0:z0], LANES), _pad_cols(w_in[..., dt0:end], LANES)]
    w_rows = jnp.concatenate(parts, axis=-1).astype(_MXU)
    w_kv_t = jnp.swapaxes(w_in[..., kv0:gate0], -1, -2).astype(_MXU)
    return w_rows, w_kv_t


def _head_rows(dt_bias, a_log):
    hp = jnp.zeros((DEPTH, SUBLANES, LANES), _F32)
    hp = hp.at[:, 0, :C_HEADS].set(dt_bias)
    return hp.at[:, 1, :C_HEADS].set(a_log)


def kernel(x_prompt, x_sample, cache_cmp_kv, cache_slc_kv, page_table, state_win_kv, state_conv, state_ssm,
           norm_g, ffn_w_gate, ffn_w_up, ffn_w_down, w_in, w_out, gmlp_norm_g, gmlp_w_s, gmlp_b_s,
           nsa_pe_cmp, nsa_w_cmp, rel_bias, conv_w, conv_b, dt_bias, a_log, d_skip, ssm_norm_g):
    bp, t = x_prompt.shape[:2]
    ns = x_sample.shape[0]
    n_pages = page_table.shape[1]
    n_phys = cache_cmp_kv.shape[1]
    tm_p, tm_s = 512, ns
    assert t % _KEY_CHUNK == 0 and t % tm_p == 0

    wg, wu, wd = (w.astype(_MXU) for w in (ffn_w_gate, ffn_w_up, ffn_w_down))
    w_in_p, w_kv_t = _pack_w_in(w_in)
    w_out_b = w_out.astype(_MXU)
    pe_t = jnp.swapaxes(nsa_pe_cmp, -1, -2)
    wc_t = jnp.swapaxes(nsa_w_cmp, -1, -2)
    bs_t = jnp.swapaxes(gmlp_b_s, 1, 2)
    wdiag = jnp.repeat(gmlp_w_s[:, :, 0, 0], A_HEAD_DIM, axis=-1)
    b0 = jnp.repeat(gmlp_b_s[:, :, 0], A_HEAD_DIM, axis=-1)
    hp = _head_rows(dt_bias, a_log)
    dsk = jnp.repeat(d_skip, C_HEAD_DIM, axis=-1)
    expand = (jnp.arange(LANES)[:, None] == jnp.arange(C_WIDTH)[None, :] // C_HEAD_DIM).astype(_F32)
    rb_flat = rel_bias.reshape(-1)
    rbt = _pad_cols(rel_bias, SUBLANES).T
    def chan_major(a):
        lead, tok = a.shape[:-4], a.shape[-4]
        perm = tuple(range(len(lead))) + tuple(len(lead) + k for k in (1, 2, 3, 0))
        return a.transpose(perm).reshape(*lead, _KV_ROWS, tok)

    def token_major(a_t):
        lead, tok = a_t.shape[:-2], a_t.shape[-1]
        a5 = a_t.reshape(*lead, 2, B_KV_HEADS, B_HEAD_DIM, tok)
        perm = tuple(range(len(lead))) + tuple(len(lead) + k for k in (3, 0, 1, 2))
        return a5.transpose(perm)

    cache_c = chan_major(cache_cmp_kv)
    cache_s = chan_major(cache_slc_kv)
    win_state = chan_major(state_win_kv)
    ssm_state = state_ssm.reshape(DEPTH, ns, C_WIDTH, C_STATE)

    xp = x_prompt.reshape(bp * t, D_MODEL)
    xs = x_sample.reshape(ns, D_MODEL)
    outs = [[] for _ in range(11)]
    for l in range(DEPTH):
        g = norm_g[l]
        row = lambda a: a.reshape(1, -1)
        xp = _ffn(xp, g[0:2], wg, wu, wd, l, 0, tm_p)
        xs = _ffn(xs, g[0:2], wg, wu, wd, l, 0, tm_s)

        uv, q, z, xbc, gate, dt, kvc_t, kvs_t, kvw_t = _inproj(xp.reshape(bp, t, D_MODEL), row(g[2]), w_in_p,
                                                               w_kv_t, l, tm_p)
        oa = _gmlp_prompt(uv.reshape(bp * t, 2 * A_WIDTH), row(gmlp_norm_g[l]), gmlp_w_s, bs_t, l)
        kvcmp_t = _compress_prompt(kvc_t, pe_t, wc_t, l)
        ob = _nsa_prompt(q, gate, *_nsa_prompt_inputs(kvcmp_t, kvs_t, kvw_t), rb_flat)
        oc, h_p = _mamba_prompt(z, xbc, dt, conv_w, row(conv_b[l]), hp[l], row(dsk[l]), row(ssm_norm_g[l]), l)
        xp = _outproj(xp, oa, ob.reshape(bp * t, B_WIDTH), oc.reshape(bp * t, C_WIDTH), row(g[3]), w_out_b, l, tm_p)
        wkeep = min(WINDOW, t)
        outs[0].append(token_major(kvc_t))
        outs[1].append(token_major(kvs_t))
        outs[2].append(token_major(kvw_t[:, :, t - wkeep:]))
        outs[3].append(xbc[:, t - (C_CONV - 1):])
        outs[4].append(h_p)

        uv, q, z, xbc, gate, dt, kvc_t, kvs_t, kvw_t = (
            a[0] for a in _inproj(xs.reshape(1, ns, D_MODEL), row(g[2]), w_in_p, w_kv_t, l, tm_s))
        q8 = jnp.pad(q.reshape(ns, B_HEADS, B_HEAD_DIM), ((0, 0), (0, SUBLANES - B_HEADS), (0, 0)))
        gate8 = jnp.pad(gate[:, :_GATE_COLS].reshape(ns, B_HEADS, N_BRANCH),
                        ((0, 0), (0, SUBLANES - B_HEADS), (0, LANES - N_BRANCH)))
        o_cmp, imp = _sample_cmp(page_table, cache_c, q8, pe_t, wc_t, rbt, l)
        n_sel_past = imp.shape[-1]
        idx = _sample_topk(imp.reshape(ns * SUBLANES, n_sel_past), SEL_TOPK - 1)
        idx = idx.reshape(ns, SUBLANES, n_sel_past)[:, :B_KV_HEADS, :SEL_TOPK - 1]
        kvc, kvs, kvwin = kvc_t.T, kvs_t.T, kvw_t.T
        kv_new = jnp.concatenate([kvc, kvs, kvwin], axis=-1).reshape(ns, 1, 6 * B_KV_WIDTH)
        ob8 = _sample_attn(page_table, idx, cache_s, q8, gate8, kv_new, o_cmp, win_state, rbt, l)
        ob = ob8[:, :B_HEADS].reshape(ns, B_WIDTH)
        st = jnp.swapaxes(state_conv[l], 0, 1)
        oa, v_rows, xdt, ea, y1, bc = _sample_mix(uv, row(gmlp_norm_g[l]), row(wdiag[l]), row(b0[l]), xbc, st,
                                                  conv_w[l], row(conv_b[l]), dt, hp[l], row(dsk[l]), expand)
        oc, h_s = _sample_ssm(ssm_state, xdt, ea, y1, bc, z, row(ssm_norm_g[l]), l)
        xs = _outproj(xs, oa, ob, oc, row(g[3]), w_out_b, l, tm_s)
        outs[5].append(kvc.reshape(ns, 1, 2, B_KV_HEADS, B_HEAD_DIM))
        outs[6].append(kvs.reshape(ns, 1, 2, B_KV_HEADS, B_HEAD_DIM))
        outs[7].append(token_major(jnp.concatenate([win_state[l][:, :, 1:], kvwin[:, :, None]], axis=-1)))
        outs[8].append(jnp.concatenate([state_conv[l][:, 1:], xbc[:, None]], axis=1))
        outs[9].append(h_s.reshape(ns, C_HEADS, C_HEAD_DIM, C_STATE))
        outs[10].append(v_rows.reshape(ns, 1, A_WIDTH))

        xp = _ffn(xp, g[4:6], wg, wu, wd, l, 1, tm_p)
        xs = _ffn(xs, g[4:6], wg, wu, wd, l, 1, tm_s)
    stacked = [jnp.stack(o) for o in outs]
    return (xp.reshape(bp, t, D_MODEL), xs.reshape(ns, 1, D_MODEL), *stacked)
```

```python
import functools
import math

import numpy as np
import jax
import jax.numpy as jnp
from jax import lax
from jax.experimental import pallas as pl
from jax.experimental.pallas import tpu as pltpu

D_MODEL = 1024
DEPTH = 4
PAGE_SIZE = 128
A_HEADS, A_HEAD_DIM, A_CHUNK = 4, 64, 128
A_WIDTH = A_HEADS * A_HEAD_DIM
B_HEADS, B_KV_HEADS, B_HEAD_DIM = 6, 2, 64
B_GROUP = B_HEADS // B_KV_HEADS
B_WIDTH = B_HEADS * B_HEAD_DIM
B_KV_WIDTH = B_KV_HEADS * B_HEAD_DIM
N_BRANCH = 3
CMP_BLOCK, SEL_BLOCK, SEL_TOPK, WINDOW, Q_BLOCK = 32, 64, 16, 512, 128
FORCE_SCORE = 1e4
C_HEADS, C_HEAD_DIM, C_GROUPS, C_STATE, C_CONV = 6, 64, 2, 64, 4
C_WIDTH = C_HEADS * C_HEAD_DIM
C_CONV_DIM = C_WIDTH + 2 * C_GROUPS * C_STATE
SSD_CHUNK = 128
D_FF = 2816
REL_BUCKETS, REL_MAX_EXACT, REL_MAX_DIST = 32, 16, 128
EPS = 1e-6
NEG_INF = -1e30
SOFTMAX_FLOOR = -1e20

LANES = 128
SUBLANES = 8
VMEM_LIMIT_BYTES = 56 * 1024 * 1024

_MXU = jnp.bfloat16
_F32 = jnp.float32

_GATE_COLS = N_BRANCH * B_HEADS
_SEG = (("uv", 2 * A_WIDTH), ("q", B_WIDTH), ("z", C_WIDTH), ("xbc", C_CONV_DIM), ("gate", LANES), ("dt", LANES))
_D_IN_PAD = sum(w for _, w in _SEG)
_KV_ROWS = 2 * B_KV_WIDTH
_N_KV = 3


def _dot(a, b):
    return jnp.dot(a.astype(_MXU), b.astype(_MXU), preferred_element_type=_F32)


def _dot_nt(a, b):
    return lax.dot_general(a.astype(_MXU), b.astype(_MXU), (((1,), (1,)), ((), ())),
                           preferred_element_type=_F32)


def _dot_tn(a, b):
    return lax.dot_general(a.astype(_MXU), b.astype(_MXU), (((0,), (0,)), ((), ())),
                           preferred_element_type=_F32)


def _dot_exact(a, b):
    return jnp.dot(a, b, preferred_element_type=_F32, precision=lax.Precision.HIGHEST)


def _rms(x, g):
    return x * lax.rsqrt(jnp.mean(x * x, axis=-1, keepdims=True) + EPS) * g


def _silu(x):
    return x * jax.nn.sigmoid(x)


def _cparams(sem):
    return pltpu.CompilerParams(dimension_semantics=sem, vmem_limit_bytes=VMEM_LIMIT_BYTES)


def _bucket_np(dist):
    n = np.maximum(dist, 0)
    nf = np.maximum(n, 1).astype(np.float32)
    large = REL_MAX_EXACT + (np.log(nf / np.float32(REL_MAX_EXACT))
                             / np.float32(math.log(REL_MAX_DIST / REL_MAX_EXACT))
                             * np.float32(REL_BUCKETS - REL_MAX_EXACT)).astype(np.int32)
    large = np.minimum(large, REL_BUCKETS - 1)
    return np.where(n < REL_MAX_EXACT, n, large).astype(np.int32)


def _ffn_kernel(x_ref, g_ref, wg_ref, wu_ref, wd_ref, o_ref, *, tf):
    x = x_ref[...]
    h = _rms(x, g_ref[0:1, :]).astype(_MXU)
    y = jnp.zeros(x.shape, _F32)
    for f in range(D_FF // tf):
        cols = slice(f * tf, (f + 1) * tf)
        a = _silu(_dot(h, wg_ref[:, cols])) * _dot(h, wu_ref[:, cols])
        y = y + _dot(a, wd_ref[cols, :])
    o_ref[...] = x + 0.5 * _rms(y, g_ref[1:2, :])


def _ffn(x, g2, wg, wu, wd, l, j, tm, tf=256):
    rows = x.shape[0]
    once = pl.Buffered(1)
    return pl.pallas_call(
        functools.partial(_ffn_kernel, tf=tf),
        grid=(rows // tm,),
        in_specs=[pl.BlockSpec((tm, D_MODEL), lambda r: (r, 0)),
                  pl.BlockSpec((2, D_MODEL), lambda r: (0, 0)),
                  pl.BlockSpec((None, None, D_MODEL, D_FF), lambda r: (l, j, 0, 0), pipeline_mode=once),
                  pl.BlockSpec((None, None, D_MODEL, D_FF), lambda r: (l, j, 0, 0), pipeline_mode=once),
                  pl.BlockSpec((None, None, D_FF, D_MODEL), lambda r: (l, j, 0, 0), pipeline_mode=once)],
        out_specs=pl.BlockSpec((tm, D_MODEL), lambda r: (r, 0)),
        out_shape=jax.ShapeDtypeStruct((rows, D_MODEL), _F32),
        compiler_params=_cparams(("parallel",)),
        name="half_ffn",
    )(x, g2, wg, wu, wd)


def _inproj_kernel(x_ref, g_ref, w_ref, wkv_ref, *o_refs):
    h = _rms(x_ref[0], g_ref[...]).astype(_MXU)
    off = 0
    for (_, width), o_ref in zip(_SEG, o_refs):
        o_ref[0] = _dot(h, w_ref[:, off:off + width])
        off += width
    for k, o_ref in enumerate(o_refs[len(_SEG):]):
        o_ref[0] = _dot_nt(wkv_ref[k * _KV_ROWS:(k + 1) * _KV_ROWS, :], h)


def _inproj(x, g_row, w_in_p, w_kv_t, l, tm):
    nb, t = x.shape[:2]
    return pl.pallas_call(
        _inproj_kernel,
        grid=(nb, t // tm),
        in_specs=[pl.BlockSpec((1, tm, D_MODEL), lambda b, r: (b, r, 0)),
                  pl.BlockSpec((1, D_MODEL), lambda b, r: (0, 0)),
                  pl.BlockSpec((None, D_MODEL, _D_IN_PAD), lambda b, r: (l, 0, 0)),
                  pl.BlockSpec((None, _N_KV * _KV_ROWS, D_MODEL), lambda b, r: (l, 0, 0))],
        out_specs=[pl.BlockSpec((1, tm, w), lambda b, r: (b, r, 0)) for _, w in _SEG]
        + [pl.BlockSpec((1, _KV_ROWS, tm), lambda b, r: (b, 0, r))] * _N_KV,
        out_shape=[jax.ShapeDtypeStruct((nb, t, w), _F32) for _, w in _SEG]
        + [jax.ShapeDtypeStruct((nb, _KV_ROWS, t), _F32)] * _N_KV,
        compiler_params=_cparams(("parallel", "parallel")),
        name="in_proj",
    )(x, g_row, w_in_p, w_kv_t)


def _outproj_kernel(x_ref, oa_ref, ob_ref, oc_ref, g_ref, w_ref, o_ref):
    y = (_dot(oa_ref[...], w_ref[0:A_WIDTH, :])
         + _dot(ob_ref[...], w_ref[A_WIDTH:A_WIDTH + B_WIDTH, :])
         + _dot(oc_ref[...], w_ref[A_WIDTH + B_WIDTH:, :]))
    o_ref[...] = x_ref[...] + _rms(y, g_ref[...])


def _outproj(x, oa, ob, oc, g_row, w_out, l, tm):
    rows = x.shape[0]
    return pl.pallas_call(
        _outproj_kernel,
        grid=(rows // tm,),
        in_specs=[pl.BlockSpec((tm, D_MODEL), lambda r: (r, 0)),
                  pl.BlockSpec((tm, A_WIDTH), lambda r: (r, 0)),
                  pl.BlockSpec((tm, B_WIDTH), lambda r: (r, 0)),
                  pl.BlockSpec((tm, C_WIDTH), lambda r: (r, 0)),
                  pl.BlockSpec((1, D_MODEL), lambda r: (0, 0)),
                  pl.BlockSpec((None, D_MODEL, D_MODEL), lambda r: (l, 0, 0))],
        out_specs=pl.BlockSpec((tm, D_MODEL), lambda r: (r, 0)),
        out_shape=jax.ShapeDtypeStruct((rows, D_MODEL), _F32),
        compiler_params=_cparams(("parallel",)),
        name="out_proj",
    )(x, oa, ob, oc, g_row, w_out)


def _gelu_ln(uv, ng):
    u = jax.nn.gelu(uv[:, :A_WIDTH])
    v = jax.nn.gelu(uv[:, A_WIDTH:])
    mu = jnp.mean(v, axis=-1, keepdims=True)
    var = jnp.mean(jnp.square(v - mu), axis=-1, keepdims=True)
    return u, (v - mu) * lax.rsqrt(var + EPS) * ng


def _gmlp_kernel(uv_ref, ng_ref, ws_ref, bs_ref, o_ref, *, chunks):
    row = lax.broadcasted_iota(jnp.int32, (A_CHUNK, A_CHUNK), 0)
    col = lax.broadcasted_iota(jnp.int32, (A_CHUNK, A_CHUNK), 1)
    ws = [jnp.where(col <= row, ws_ref[h], 0.0).astype(_MXU) for h in range(A_HEADS)]
    for c in range(chunks):
        u, v = _gelu_ln(uv_ref[c * A_CHUNK:(c + 1) * A_CHUNK, :], ng_ref[...])
        sg = [_dot(ws[h], v[:, h * A_HEAD_DIM:(h + 1) * A_HEAD_DIM]) + bs_ref[:, h:h + 1]
              for h in range(A_HEADS)]
        o_ref[c * A_CHUNK:(c + 1) * A_CHUNK, :] = u * jnp.concatenate(sg, axis=-1)


def _gmlp_prompt(uv, ng_row, ws, bs_t, l, chunks=4):
    rows = uv.shape[0]
    tm = chunks * A_CHUNK
    return pl.pallas_call(
        functools.partial(_gmlp_kernel, chunks=chunks),
        grid=(rows // tm,),
        in_specs=[pl.BlockSpec((tm, 2 * A_WIDTH), lambda r: (r, 0)),
                  pl.BlockSpec((1, A_WIDTH), lambda r: (0, 0)),
                  pl.BlockSpec((None, A_HEADS, A_CHUNK, A_CHUNK), lambda r: (l, 0, 0, 0)),
                  pl.BlockSpec((None, A_CHUNK, A_HEADS), lambda r: (l, 0, 0))],
        out_specs=pl.BlockSpec((tm, A_WIDTH), lambda r: (r, 0)),
        out_shape=jax.ShapeDtypeStruct((rows, A_WIDTH), _F32),
        compiler_params=_cparams(("parallel",)),
        name="gmlp_prompt",
    )(uv, ng_row, ws, bs_t)


def _mamba_kernel(z_ref, xbc_ref, dt_ref, cw_ref, cb_ref, hp_ref, dsk_ref, ng_ref,
                  o_ref, hout_ref, xp_ref, hs_ref):
    t = pl.program_id(0)
    L = SSD_CHUNK
    hist = SUBLANES

    @pl.when(t == 0)
    def _():
        xp_ref[:, 0:hist, :] = jnp.zeros((xp_ref.shape[0], hist, C_CONV_DIM), _F32)
        hs_ref[...] = jnp.zeros_like(hs_ref)

    a_row = -jnp.exp(hp_ref[1:2, :])
    row = lax.broadcasted_iota(jnp.int32, (L, L), 0)
    col = lax.broadcasted_iota(jnp.int32, (L, L), 1)
    causal = col <= row
    tril = jnp.where(causal, 1.0, 0.0)
    for b in range(z_ref.shape[0]):
        xp_ref[b, hist:hist + L, :] = xbc_ref[b]
        conv = cb_ref[...]
        for k in range(C_CONV):
            conv = conv + xp_ref[b, pl.ds(hist - (C_CONV - 1) + k, L), :] * cw_ref[k:k + 1, :]
        xp_ref[b, 0:hist, :] = xp_ref[b, L:L + hist, :]
        xc = _silu(conv)
        xs = xc[:, :C_WIDTH]
        bm = xc[:, C_WIDTH:C_WIDTH + C_GROUPS * C_STATE]
        cm = xc[:, C_WIDTH + C_GROUPS * C_STATE:]

        dt = jax.nn.softplus(dt_ref[b] + hp_ref[0:1, :])
        acum = _dot_exact(tril, dt * a_row)
        acum_t = acum.T
        dt_t = dt.T
        cb = [_dot_nt(cm[:, g * C_STATE:(g + 1) * C_STATE], bm[:, g * C_STATE:(g + 1) * C_STATE])
              for g in range(C_GROUPS)]
        ys = []
        for h in range(C_HEADS):
            g = h // (C_HEADS // C_GROUPS)
            ac_col = acum[:, h:h + 1]
            seg = ac_col - acum_t[h:h + 1, :]
            decay = jnp.where(causal, jnp.exp(jnp.where(causal, seg, 0.0)), 0.0)
            scores = cb[g] * decay * dt_t[h:h + 1, :]
            x_h = xs[:, h * C_HEAD_DIM:(h + 1) * C_HEAD_DIM]
            b_g = bm[:, g * C_STATE:(g + 1) * C_STATE]
            c_g = cm[:, g * C_STATE:(g + 1) * C_STATE]
            hs = hs_ref[b, h]
            ys.append(_dot(scores, x_h) + _dot_nt(c_g, hs) * jnp.exp(ac_col))
            ac_last = acum[L - 1:L, h:h + 1]
            w_end = jnp.exp(ac_last - ac_col) * dt[:, h:h + 1]
            hs_ref[b, h] = hs * jnp.exp(ac_last) + _dot_tn(x_h * w_end, b_g)
        y = jnp.concatenate(ys, axis=-1) + dsk_ref[...] * xs
        o_ref[b] = _rms(y * _silu(z_ref[b]), ng_ref[...])

    @pl.when(t == pl.num_programs(0) - 1)
    def _():
        hout_ref[...] = hs_ref[...]


def _mamba_prompt(z, xbc, dt, conv_w, conv_b_row, hp, dsk_row, ng_row, l):
    nb, t = z.shape[:2]
    L = SSD_CHUNK
    return pl.pallas_call(
        _mamba_kernel,
        grid=(t // L,),
        in_specs=[pl.BlockSpec((nb, L, C_WIDTH), lambda c: (0, c, 0)),
                  pl.BlockSpec((nb, L, C_CONV_DIM), lambda c: (0, c, 0)),
                  pl.BlockSpec((nb, L, LANES), lambda c: (0, c, 0)),
                  pl.BlockSpec((None, C_CONV, C_CONV_DIM), lambda c: (l, 0, 0)),
                  pl.BlockSpec((1, C_CONV_DIM), lambda c: (0, 0)),
                  pl.BlockSpec((SUBLANES, LANES), lambda c: (0, 0)),
                  pl.BlockSpec((1, C_WIDTH), lambda c: (0, 0)),
                  pl.BlockSpec((1, C_WIDTH), lambda c: (0, 0))],
        out_specs=[pl.BlockSpec((nb, L, C_WIDTH), lambda c: (0, c, 0)),
                   pl.BlockSpec((nb, C_HEADS, C_HEAD_DIM, C_STATE), lambda c: (0, 0, 0, 0))],
        out_shape=[jax.ShapeDtypeStruct((nb, t, C_WIDTH), _F32),
                   jax.ShapeDtypeStruct((nb, C_HEADS, C_HEAD_DIM, C_STATE), _F32)],
        scratch_shapes=[pltpu.VMEM((nb, L + 2 * SUBLANES, C_CONV_DIM), _F32),
                        pltpu.VMEM((nb, C_HEADS, C_HEAD_DIM, C_STATE), _F32)],
        compiler_params=_cparams(("arbitrary",)),
        name="mamba_prompt",
    )(z, xbc, dt, conv_w, conv_b_row, hp, dsk_row, ng_row)


def _split_dot(x, p):
    hi = x.astype(_MXU)
    lo = (x - hi.astype(_F32)).astype(_MXU)
    return jnp.dot(hi, p, preferred_element_type=_F32) + jnp.dot(lo, p, preferred_element_type=_F32)


def _pool_matrix(n_tok, per, n_out, first=0):
    r = lax.broadcasted_iota(jnp.int32, (n_tok, n_out), 0)
    c = lax.broadcasted_iota(jnp.int32, (n_tok, n_out), 1)
    return jnp.where(c == first + r // per, 1.0, 0.0).astype(_MXU)


def _pe_sums_t(pet_ref):
    pk = jnp.sum(pet_ref[0], axis=-1, keepdims=True)
    pv = jnp.sum(pet_ref[1], axis=-1, keepdims=True)
    return jnp.concatenate([pk, pk, pv, pv], axis=0)


def _compress_cols(m, wt_ref):
    hd = B_HEAD_DIM
    return jnp.concatenate([_dot(wt_ref[0], m[0:hd]), _dot(wt_ref[0], m[hd:2 * hd]),
                            _dot(wt_ref[1], m[2 * hd:3 * hd]), _dot(wt_ref[1], m[3 * hd:])], axis=0)


def _compress_kernel(kv_ref, pet_ref, wt_ref, o_ref, *, chunk):
    t = kv_ref.shape[-1]
    pool = _pool_matrix(chunk, CMP_BLOCK, chunk // CMP_BLOCK)
    sums = jnp.concatenate([_split_dot(kv_ref[0, :, c * chunk:(c + 1) * chunk], pool) for c in range(t // chunk)],
                           axis=-1)
    o_ref[0] = _compress_cols((sums + _pe_sums_t(pet_ref)) * (1.0 / CMP_BLOCK), wt_ref)


def _compress_prompt(kvc_t, pe_t, w_t, l, chunk=2048):
    nb, _, t = kvc_t.shape
    return pl.pallas_call(
        functools.partial(_compress_kernel, chunk=min(chunk, t)),
        grid=(nb,),
        in_specs=[pl.BlockSpec((1, _KV_ROWS, t), lambda b: (b, 0, 0)),
                  pl.BlockSpec((None, 2, B_HEAD_DIM, CMP_BLOCK), lambda b: (l, 0, 0, 0)),
                  pl.BlockSpec((None, 2, B_HEAD_DIM, B_HEAD_DIM), lambda b: (l, 0, 0, 0))],
        out_specs=pl.BlockSpec((1, _KV_ROWS, t // CMP_BLOCK), lambda b: (b, 0, 0)),
        out_shape=jax.ShapeDtypeStruct((nb, _KV_ROWS, t // CMP_BLOCK), _F32),
        compiler_params=_cparams(("parallel",)),
        name="nsa_compress_prompt",
    )(kvc_t, pe_t, w_t)


_KEY_CHUNK = 512
_FRONT_PAD = Q_BLOCK
_NEAR = 2 * Q_BLOCK
_MAIN_UNROLL = 2
_Q_PER_STEP = 1


def _nsa_tables(n_sel):
    r = np.arange(Q_BLOCK)[:, None]
    cmp_idx = np.full((2, Q_BLOCK, n_sel), REL_BUCKETS - 1, np.int32)
    for par in range(2):
        for u in (-2, -1, 0, 1):
            dist = r[:, 0] - (CMP_BLOCK - 1) - CMP_BLOCK * (2 * u + par)
            cmp_idx[par, :, u % n_sel] = _bucket_np(dist)
    c = np.arange(_NEAR)[None, :]
    dist = Q_BLOCK + r - c
    near_idx = np.where(dist >= 0, _bucket_np(dist), -1).astype(np.int32)
    c = np.arange(WINDOW + Q_BLOCK)[None, :]
    dist = r + WINDOW - c
    win_idx = np.where((dist >= 0) & (dist < WINDOW), _bucket_np(dist), -1).astype(np.int32)
    return cmp_idx, near_idx, win_idx


def _fill_bias(idx, rb_ref, hg, rel_to_last):
    base = rb_ref[(REL_BUCKETS - 1) * B_HEADS + hg] if rel_to_last else 0.0
    tile = jnp.where(idx < 0, NEG_INF, 0.0)
    for b in range(REL_BUCKETS):
        tile = jnp.where(idx == b, rb_ref[b * B_HEADS + hg] - base, tile)
    return tile


def _topk_mask(score, k, taken):
    n = float(score.shape[-1])
    lane = lax.broadcasted_iota(jnp.int32, score.shape, 1).astype(_F32)
    s = jnp.where(taken, -jnp.inf, score)
    for _ in range(k):
        m = jnp.max(s, axis=-1, keepdims=True)
        first = jnp.min(jnp.where(s == m, lane, n), axis=-1, keepdims=True)
        s = jnp.where(lane == first, -jnp.inf, s)
    return s == -jnp.inf


def _masked_softmax(l):
    m = jnp.maximum(jnp.max(l, axis=-1, keepdims=True), SOFTMAX_FLOOR)
    e = jnp.exp(l - m)
    return e / jnp.maximum(jnp.sum(e, axis=-1, keepdims=True), 1e-20)


def _online_step(s, vt, carry):
    m_run, l_run, acc = carry
    m_new = jnp.maximum(m_run, jnp.max(s, axis=-1, keepdims=True))
    alpha = jnp.exp(m_run - m_new)
    p = jnp.exp(s - m_new)
    return (m_new, alpha * l_run + jnp.sum(p, axis=-1, keepdims=True), alpha * acc + _dot_nt(p, vt))


def _key_tiles(ref, first, n, rows=slice(None)):
    return jnp.concatenate([ref[0, first + j, rows, :] for j in range(n)], axis=-1)


def _nsa_kernel(q_ref, gt_ref, kc_ref, vc_ref, ksa_ref, vs_ref, kwa_ref, vwa_ref,
                cidx_ref, nidx_ref, widx_ref, rb_ref, o_ref, bc_ref, bn_ref, bw_ref, *, n_sel):
    step = pl.program_id(1)
    hd, G = B_HEAD_DIM, B_GROUP
    QB = Q_BLOCK

    @pl.when((pl.program_id(0) == 0) & (step == 0))
    def _():
        for hg in range(B_HEADS):
            for par in range(2):
                bc_ref[hg, par] = _fill_bias(cidx_ref[par], rb_ref, hg, False)
            bn_ref[hg] = _fill_bias(nidx_ref[...], rb_ref, hg, True)
            bw_ref[hg] = _fill_bias(widx_ref[...], rb_ref, hg, False)

    heads = range(B_KV_HEADS)
    hrows = [slice(h * hd, (h + 1) * hd) for h in heads]
    r_col = lax.broadcasted_iota(jnp.int32, (QB, 1), 0)
    n_lane = lax.broadcasted_iota(jnp.int32, (QB, n_sel), 1)
    c_near = lax.broadcasted_iota(jnp.int32, (QB, _NEAR), 1)
    c_win = lax.broadcasted_iota(jnp.int32, (QB, WINDOW + QB), 1)
    zeros_h = jnp.zeros((G * QB, hd), _F32)
    n_win = (WINDOW + QB) // LANES
    per_chunk = _KEY_CHUNK // LANES

    def front(i, q, gate_logits):
        q = q * (hd ** -0.5)
        qpos = i * QB + r_col
        cur = qpos // SEL_BLOCK
        vis = [CMP_BLOCK * (2 * n_lane + par) + (CMP_BLOCK - 1) <= qpos for par in range(2)]
        vis3 = jnp.concatenate([jnp.concatenate(vis, axis=-1)] * G, axis=0)
        win_ok = jnp.concatenate([c_win >= WINDOW - i * QB] * G, axis=0)
        q3 = [jnp.concatenate([q[:, (h * G + g) * hd:(h * G + g + 1) * hd] for g in range(G)], axis=0)
              for h in heads]
        q3h = [jnp.concatenate([q3[h], zeros_h] if h == 0 else [zeros_h, q3[h]], axis=-1) for h in heads]
        t0 = i * (QB // LANES)

        forced = (n_lane == cur) | (n_lane == 0)
        started = n_lane <= cur
        o_c, imp = [], []
        for h in heads:
            lc = _dot(q3[h], kc_ref[0, hrows[h], :])
            bias_c = jnp.concatenate(
                [jnp.concatenate([pltpu.roll(bc_ref[h * G + g, par], 2 * i, 1) for par in range(2)], axis=-1)
                 for g in range(G)], axis=0)
            p_c = _masked_softmax(jnp.where(vis3, lc + bias_c, NEG_INF))
            o_c.append(_dot_nt(p_c, vc_ref[0, hrows[h], :]))
            imp_h = sum(p_c[g * QB:(g + 1) * QB, :n_sel] + p_c[g * QB:(g + 1) * QB, n_sel:] for g in range(G))
            imp.append(jnp.where(started, imp_h, -1.0))

        chosen = _topk_mask(jnp.concatenate(imp, axis=0), min(SEL_TOPK, n_sel) - 2,
                            jnp.concatenate([forced] * B_KV_HEADS, axis=0))
        kw = _key_tiles(kwa_ref, t0, n_win)
        o_w = []
        for h in heads:
            bias_w = jnp.concatenate([bw_ref[h * G + g] for g in range(G)], axis=0)
            p_w = _masked_softmax(jnp.where(win_ok, _dot(q3h[h], kw) + bias_w, NEG_INF))
            o_w.append(_dot_nt(p_w, _key_tiles(vwa_ref, t0, n_win, hrows[h])))

        qa_main, qa_near = [], []
        for h in heads:
            allowed = chosen[h * QB:(h + 1) * QB] & started
            m_main = jnp.where(allowed & (n_lane < 2 * i - 2), 0.0, NEG_INF)
            m_near = jnp.where(allowed, 0.0, NEG_INF)
            qa_main.append(jnp.concatenate([q3h[h], jnp.concatenate([m_main] * G, axis=0)], axis=-1).astype(_MXU))
            qa_near.append(jnp.concatenate([q3h[h], jnp.concatenate([m_near] * G, axis=0)], axis=-1).astype(_MXU))
        return dict(i=i, gate=jax.nn.sigmoid(gate_logits), o_c=o_c, o_w=o_w, qa_main=qa_main, qa_near=qa_near)

    def back(f, carry):
        i = f["i"]
        t0 = i * (QB // LANES)
        near_ok = jnp.concatenate([c_near >= _FRONT_PAD - i * QB] * G, axis=0)
        ks = _key_tiles(ksa_ref, t0, _NEAR // LANES)
        outs = []
        for h in heads:
            corr = jnp.concatenate([bn_ref[h * G + g] for g in range(G)], axis=0)
            s = jnp.where(near_ok, _dot(f["qa_near"][h], ks) + corr, NEG_INF)
            _, l_s, acc_s = _online_step(s, _key_tiles(vs_ref, t0, _NEAR // LANES, hrows[h]), carry[h])
            o_s = acc_s / l_s
            for g in range(G):
                k0 = (h * G + g) * N_BRANCH
                rows = slice(g * QB, (g + 1) * QB)
                gate = f["gate"]
                outs.append(gate[:, k0:k0 + 1] * f["o_c"][h][rows] + gate[:, k0 + 1:k0 + 2] * o_s[rows]
                            + gate[:, k0 + 2:k0 + 3] * f["o_w"][h][rows])
        return jnp.concatenate(outs, axis=-1)

    blocks = [front(step * _Q_PER_STEP + u, q_ref[0, u * QB:(u + 1) * QB, :], gt_ref[0, u * QB:(u + 1) * QB, :])
              for u in range(_Q_PER_STEP)]
    n_chain = _Q_PER_STEP * B_KV_HEADS
    qa_main_all = jnp.concatenate([f["qa_main"][h] for f in blocks for h in heads], axis=0)
    i_last = step * _Q_PER_STEP + _Q_PER_STEP - 1
    n_main = (jnp.maximum(i_last - 1, 0) * QB + _KEY_CHUNK - 1) // _KEY_CHUNK

    def main_body(c, carry):
        for u in range(_MAIN_UNROLL):
            tc = _FRONT_PAD // LANES + (c * _MAIN_UNROLL + u) * per_chunk
            s = _dot(qa_main_all, _key_tiles(ksa_ref, tc, per_chunk))
            carry = tuple(_online_step(s[k * G * QB:(k + 1) * G * QB],
                                       _key_tiles(vs_ref, tc, per_chunk, hrows[k % B_KV_HEADS]), carry[k])
                          for k in range(n_chain))
        return carry

    init = (jnp.full((G * QB, 1), SOFTMAX_FLOOR, _F32), jnp.zeros((G * QB, 1), _F32), zeros_h)
    carry = lax.fori_loop(0, (n_main + _MAIN_UNROLL - 1) // _MAIN_UNROLL, main_body, (init,) * n_chain)
    for u, f in enumerate(blocks):
        o_ref[0, u * QB:(u + 1) * QB, :] = back(f, carry[u * B_KV_HEADS:(u + 1) * B_KV_HEADS])


def _nsa_prompt(q, gate, kcp, vcp, ksa, vs, kwa, vwa, rb_flat):
    nb, t = q.shape[:2]
    n_sel = t // SEL_BLOCK
    cidx, nidx, widx = _nsa_tables(n_sel)
    tp, tw = ksa.shape[1], kwa.shape[1]
    q_rows = _Q_PER_STEP * Q_BLOCK
    assert t % q_rows == 0
    full = lambda shape: pl.BlockSpec(shape, lambda b, i: (0,) * len(shape))
    return pl.pallas_call(
        functools.partial(_nsa_kernel, n_sel=n_sel),
        grid=(nb, t // q_rows),
        in_specs=[pl.BlockSpec((1, q_rows, B_WIDTH), lambda b, i: (b, i, 0)),
                  pl.BlockSpec((1, q_rows, LANES), lambda b, i: (b, i, 0)),
                  pl.BlockSpec((1, B_KV_WIDTH, 2 * n_sel), lambda b, i: (b, 0, 0)),
                  pl.BlockSpec((1, B_KV_WIDTH, 2 * n_sel), lambda b, i: (b, 0, 0)),
                  pl.BlockSpec((1, tp, B_KV_WIDTH + n_sel, LANES), lambda b, i: (b, 0, 0, 0)),
                  pl.BlockSpec((1, tp, B_KV_WIDTH, LANES), lambda b, i: (b, 0, 0, 0)),
                  pl.BlockSpec((1, tw, B_KV_WIDTH, LANES), lambda b, i: (b, 0, 0, 0)),
                  pl.BlockSpec((1, tw, B_KV_WIDTH, LANES), lambda b, i: (b, 0, 0, 0)),
                  full(cidx.shape), full(nidx.shape), full(widx.shape),
                  pl.BlockSpec(memory_space=pltpu.SMEM)],
        out_specs=pl.BlockSpec((1, q_rows, B_WIDTH), lambda b, i: (b, i, 0)),
        out_shape=jax.ShapeDtypeStruct((nb, t, B_WIDTH), _F32),
        scratch_shapes=[pltpu.VMEM((B_HEADS, 2, Q_BLOCK, n_sel), _F32),
                        pltpu.VMEM((B_HEADS, Q_BLOCK, _NEAR), _F32),
                        pltpu.VMEM((B_HEADS, Q_BLOCK, WINDOW + Q_BLOCK), _F32)],
        compiler_params=_cparams(("arbitrary", "arbitrary")),
        name="nsa_prompt",
    )(q, gate, kcp, vcp, ksa, vs, kwa, vwa, jnp.asarray(cidx), jnp.asarray(nidx), jnp.asarray(widx), rb_flat)


def _nsa_prompt_inputs(kvcmp_t, kvs_t, kvw_t):
    nb, _, t = kvs_t.shape
    n_sel = t // SEL_BLOCK
    kvp = kvcmp_t.reshape(nb, _KV_ROWS, n_sel, 2).transpose(0, 1, 3, 2).reshape(nb, _KV_ROWS, 2 * n_sel)
    kcp = kvp[:, :B_KV_WIDTH].astype(_MXU)
    vcp = kvp[:, B_KV_WIDTH:].astype(_MXU)
    blk = (jnp.arange(n_sel)[:, None] == jnp.arange(t)[None, :] // SEL_BLOCK).astype(_MXU)
    ksa = jnp.concatenate([kvs_t[:, :B_KV_WIDTH].astype(_MXU), jnp.broadcast_to(blk, (nb, n_sel, t))], axis=1)
    pad_s = ((0, 0), (0, 0), (_FRONT_PAD, _MAIN_UNROLL * _KEY_CHUNK - _FRONT_PAD))
    pad_w = ((0, 0), (0, 0), (WINDOW, 0))
    def tiles(a, pad):
        a = jnp.pad(a, pad)
        return a.reshape(nb, a.shape[1], a.shape[2] // LANES, LANES).transpose(0, 2, 1, 3)

    vs = tiles(kvs_t[:, B_KV_WIDTH:].astype(_MXU), pad_s)
    kwa = tiles(kvw_t[:, :B_KV_WIDTH].astype(_MXU), pad_w)
    vwa = tiles(kvw_t[:, B_KV_WIDTH:].astype(_MXU), pad_w)
    return kcp, vcp, tiles(ksa, pad_s), vs, kwa, vwa


def _bias_rows(dist, rbt):
    n = jnp.maximum(dist, 0)
    nf = jnp.maximum(n, 1).astype(_F32)
    large = REL_MAX_EXACT + (jnp.log(nf / REL_MAX_EXACT) / math.log(REL_MAX_DIST / REL_MAX_EXACT)
                             * (REL_BUCKETS - REL_MAX_EXACT)).astype(jnp.int32)
    bucket = jnp.where(n < REL_MAX_EXACT, n, jnp.minimum(large, REL_BUCKETS - 1))
    out = jnp.zeros((SUBLANES, dist.shape[-1]), _F32)
    for b in range(REL_BUCKETS):
        out = jnp.where(bucket == b, rbt[:, b:b + 1], out)
    return out


def _scmp_kernel(pt_ref, *refs, n_pages, group):
    pages = refs[:group]
    q_ref, pet_ref, wt_ref, rbt_ref, oc_ref, imp_ref, kvm_ref = refs[group:]
    pg = pl.program_id(1)
    n_cmp = n_pages * (PAGE_SIZE // CMP_BLOCK)
    per = PAGE_SIZE // CMP_BLOCK * group

    x = jnp.concatenate([p[...] for p in pages], axis=-1)
    kvm_ref[pg] = _split_dot(x, _pool_matrix(group * PAGE_SIZE, CMP_BLOCK, per))

    @pl.when(pg == pl.num_programs(1) - 1)
    def _():
        hd = B_HEAD_DIM
        n_sel = n_cmp // 2
        past = n_pages * PAGE_SIZE
        sums = jnp.concatenate([kvm_ref[s] for s in range(n_pages // group)], axis=-1)
        kv = _compress_cols((sums + _pe_sums_t(pet_ref)) * (1.0 / CMP_BLOCK), wt_ref)
        q8 = q_ref[0] * (hd ** -0.5)
        row = lax.broadcasted_iota(jnp.int32, (SUBLANES, 1), 0)
        head0 = row < B_GROUP
        lc = jnp.where(head0, _dot(q8, kv[0:hd]), _dot(q8, kv[hd:2 * hd]))
        blk = lax.broadcasted_iota(jnp.int32, (1, n_cmp), 1)
        dist = past - (blk * CMP_BLOCK + CMP_BLOCK - 1)
        p = _masked_softmax(jnp.where(dist >= 0, lc + _bias_rows(dist, rbt_ref[...]), NEG_INF))
        oc_ref[0] = jnp.where(head0, _dot_nt(p, kv[2 * hd:3 * hd]), _dot_nt(p, kv[3 * hd:]))
        pool = (lax.broadcasted_iota(jnp.int32, (n_cmp, n_sel), 0) // (SEL_BLOCK // CMP_BLOCK)
                == lax.broadcasted_iota(jnp.int32, (n_cmp, n_sel), 1))
        pp = _dot_exact(p, jnp.where(pool, 1.0, 0.0))
        imp0 = jnp.sum(pp[0:B_GROUP], axis=0, keepdims=True)
        imp1 = jnp.sum(pp[B_GROUP:2 * B_GROUP], axis=0, keepdims=True)
        imp_ref[0] = jnp.where(row == 0, imp0, jnp.where(row == 1, imp1, 0.0))


def _sample_cmp(page_table, cache_t, q8, pe_t, w_t, rbt, l, group=16):
    ns, n_pages = page_table.shape
    n_cmp = n_pages * (PAGE_SIZE // CMP_BLOCK)
    page_spec = lambda k: pl.BlockSpec((None, None, _KV_ROWS, PAGE_SIZE),
                                       lambda b, g, pt: (l, pt[b, g * group + k], 0, 0))
    return pl.pallas_call(
        functools.partial(_scmp_kernel, n_pages=n_pages, group=group),
        grid_spec=pltpu.PrefetchScalarGridSpec(
            num_scalar_prefetch=1,
            grid=(ns, n_pages // group),
            in_specs=[page_spec(k) for k in range(group)] + [
                pl.BlockSpec((1, SUBLANES, B_HEAD_DIM), lambda b, g, pt: (b, 0, 0)),
                pl.BlockSpec((None, 2, B_HEAD_DIM, CMP_BLOCK), lambda b, g, pt: (l, 0, 0, 0)),
                pl.BlockSpec((None, 2, B_HEAD_DIM, B_HEAD_DIM), lambda b, g, pt: (l, 0, 0, 0)),
                pl.BlockSpec((SUBLANES, REL_BUCKETS), lambda b, g, pt: (0, 0))],
            out_specs=[pl.BlockSpec((1, SUBLANES, B_HEAD_DIM), lambda b, g, pt: (b, 0, 0)),
                       pl.BlockSpec((1, SUBLANES, n_cmp // 2), lambda b, g, pt: (b, 0, 0))],
            scratch_shapes=[pltpu.VMEM((n_pages // group, _KV_ROWS, n_cmp * group // n_pages), _F32)]),
        out_shape=[jax.ShapeDtypeStruct((ns, SUBLANES, B_HEAD_DIM), _F32),
                   jax.ShapeDtypeStruct((ns, SUBLANES, n_cmp // 2), _F32)],
        compiler_params=_cparams(("arbitrary", "arbitrary")),
        name="nsa_sample_cmp",
    )(page_table, *([cache_t] * group), q8, pe_t, w_t, rbt)


def _stopk_kernel(imp_ref, idx_ref, *, k):
    s = imp_ref[...]
    n = s.shape[-1]
    lane = lax.broadcasted_iota(jnp.int32, s.shape, 1)
    s = jnp.where(lane == 0, FORCE_SCORE, s)

    def body(it, carry):
        s, out = carry
        m = jnp.max(s, axis=-1, keepdims=True)
        first = jnp.min(jnp.where(s == m, lane, n), axis=-1, keepdims=True)
        return jnp.where(lane == first, -jnp.inf, s), jnp.where(lane == it, first, out)

    _, out = lax.fori_loop(0, k, body, (s, jnp.zeros(s.shape, jnp.int32)))
    idx_ref[...] = out


def _sample_topk(imp2d, k):
    return pl.pallas_call(
        functools.partial(_stopk_kernel, k=k),
        out_shape=jax.ShapeDtypeStruct(imp2d.shape, jnp.int32),
        name="nsa_sample_topk",
    )(imp2d)


def _sattn_kernel(pt_ref, idx_ref, *refs, n_blk, past):
    blocks = refs[:2 * n_blk]
    q_ref, gt_ref, new_ref, oc_ref, win_ref, rbt_ref, o_ref = refs[2 * n_blk:]
    b = pl.program_id(0)
    hd = B_HEAD_DIM
    q8 = q_ref[0] * (hd ** -0.5)
    rbt = rbt_ref[...]
    row = lax.broadcasted_iota(jnp.int32, (SUBLANES, 1), 0)
    head0 = row < B_GROUP
    new = new_ref[0]
    bias0 = _bias_rows(jnp.zeros((1, 1), jnp.int32), rbt)

    def attend(s, v_of_head, k_new, v_new):
        s_new = jnp.sum(q8 * k_new, axis=-1, keepdims=True) + bias0
        m = jnp.maximum(jnp.max(s, axis=-1, keepdims=True), s_new)
        p = jnp.exp(s - m)
        p_new = jnp.exp(s_new - m)
        den = jnp.sum(p, axis=-1, keepdims=True) + p_new
        num = jnp.where(head0, _dot_nt(p, v_of_head(0)), _dot_nt(p, v_of_head(1))) + p_new * v_new
        return num / den

    def per_head(a0, a1):
        return jnp.where(head0, a0, a1)

    per_page = PAGE_SIZE // SEL_BLOCK
    t_in = lax.broadcasted_iota(jnp.int32, (1, PAGE_SIZE), 1)
    s_h, vs = [], []
    for h in range(B_KV_HEADS):
        kt = jnp.concatenate([blocks[h * n_blk + k][h * hd:(h + 1) * hd, :] for k in range(n_blk)], axis=-1)
        vs.append(jnp.concatenate(
            [blocks[h * n_blk + k][B_KV_WIDTH + h * hd:B_KV_WIDTH + (h + 1) * hd, :] for k in range(n_blk)], axis=-1))
        dist = []
        for k in range(n_blk):
            blk = idx_ref[b, h, k]
            in_blk = t_in // SEL_BLOCK == blk % per_page
            dist.append(jnp.where(in_blk, past - ((blk // per_page) * PAGE_SIZE + t_in), -1))
        dist = jnp.concatenate(dist, axis=-1)
        s_h.append(jnp.where(dist >= 0, _dot(q8, kt) + _bias_rows(dist, rbt), NEG_INF))
    ksn = new[:, 2 * B_KV_WIDTH:3 * B_KV_WIDTH]
    vsn = new[:, 3 * B_KV_WIDTH:4 * B_KV_WIDTH]
    o_s = attend(per_head(s_h[0], s_h[1]), lambda h: vs[h],
                 per_head(ksn[:, :hd], ksn[:, hd:]), per_head(vsn[:, :hd], vsn[:, hd:]))

    win = win_ref[0]
    wb = win.shape[1]
    dist = wb - lax.broadcasted_iota(jnp.int32, (1, wb), 1)
    okw = (dist < WINDOW) & (past - dist >= 0)
    lw = per_head(_dot(q8, win[0:hd]), _dot(q8, win[hd:2 * hd]))
    sw = jnp.where(okw, lw + _bias_rows(dist, rbt), NEG_INF)
    kwn = new[:, 4 * B_KV_WIDTH:5 * B_KV_WIDTH]
    vwn = new[:, 5 * B_KV_WIDTH:6 * B_KV_WIDTH]
    o_w = attend(sw, lambda h: win[B_KV_WIDTH + h * hd:B_KV_WIDTH + (h + 1) * hd],
                 per_head(kwn[:, :hd], kwn[:, hd:]), per_head(vwn[:, :hd], vwn[:, hd:]))

    gate = jax.nn.sigmoid(gt_ref[0])
    o_ref[0] = gate[:, 0:1] * oc_ref[0] + gate[:, 1:2] * o_s + gate[:, 2:3] * o_w


def _sample_attn(page_table, idx, cache_t, q8, gate8, kv_new, oc, win_t, rbt, l):
    ns, n_pages = page_table.shape
    n_blk = idx.shape[-1]
    past = n_pages * PAGE_SIZE
    per_page = PAGE_SIZE // SEL_BLOCK

    pages = jnp.take_along_axis(page_table[:, None, :], idx // per_page, axis=-1)

    def blk_spec(h, k):
        return pl.BlockSpec((None, None, _KV_ROWS, PAGE_SIZE), lambda b, pg, ix: (l, pg[b, h, k], 0, 0))

    wb = win_t.shape[3]
    return pl.pallas_call(
        functools.partial(_sattn_kernel, n_blk=n_blk, past=past),
        grid_spec=pltpu.PrefetchScalarGridSpec(
            num_scalar_prefetch=2,
            grid=(ns,),
            in_specs=[blk_spec(h, k) for h in range(B_KV_HEADS) for k in range(n_blk)] + [
                pl.BlockSpec((1, SUBLANES, B_HEAD_DIM), lambda b, pt, ix: (b, 0, 0)),
                pl.BlockSpec((1, SUBLANES, LANES), lambda b, pt, ix: (b, 0, 0)),
                pl.BlockSpec((1, 1, 6 * B_KV_WIDTH), lambda b, pt, ix: (b, 0, 0)),
                pl.BlockSpec((1, SUBLANES, B_HEAD_DIM), lambda b, pt, ix: (b, 0, 0)),
                pl.BlockSpec((None, 1, _KV_ROWS, wb), lambda b, pt, ix: (l, b, 0, 0)),
                pl.BlockSpec((SUBLANES, REL_BUCKETS), lambda b, pt, ix: (0, 0))],
            out_specs=pl.BlockSpec((1, SUBLANES, B_HEAD_DIM), lambda b, pt, ix: (b, 0, 0))),
        out_shape=jax.ShapeDtypeStruct((ns, SUBLANES, B_HEAD_DIM), _F32),
        compiler_params=_cparams(("arbitrary",)),
        name="nsa_sample_attn",
    )(pages, idx, *([cache_t] * (B_KV_HEADS * n_blk)), q8, gate8, kv_new, oc, win_t, rbt)


def _smix_kernel(uv_ref, ng_ref, wd_ref, b0_ref, xbc_ref, st_ref, cw_ref, cb_ref, dt_ref, hp_ref,
                 dsk_ref, ex_ref, oa_ref, v_ref, xdt_ref, ea_ref, y1_ref, bc_ref):
    u, v = _gelu_ln(uv_ref[...], ng_ref[...])
    v_ref[...] = v
    oa_ref[...] = u * (v * wd_ref[...] + b0_ref[...])
    conv = cb_ref[...] + xbc_ref[...] * cw_ref[C_CONV - 1:C_CONV, :]
    for k in range(C_CONV - 1):
        conv = conv + st_ref[k] * cw_ref[k:k + 1, :]
    xc = _silu(conv)
    xs = xc[:, :C_WIDTH]
    bm = xc[:, C_WIDTH:C_WIDTH + C_GROUPS * C_STATE]
    cm = xc[:, C_WIDTH + C_GROUPS * C_STATE:]
    bc_ref[...] = xc[:, C_WIDTH:]
    dt = jax.nn.softplus(dt_ref[...] + hp_ref[0:1, :])
    acum = dt * (-jnp.exp(hp_ref[1:2, :]))
    dt_rep = _dot_exact(dt, ex_ref[...])
    ea_ref[...] = jnp.exp(_dot_exact(acum, ex_ref[...]))
    xdt = dt_rep * xs
    xdt_ref[...] = xdt
    per_g = C_WIDTH // C_GROUPS
    cb = [jnp.sum(cm[:, g * C_STATE:(g + 1) * C_STATE] * bm[:, g * C_STATE:(g + 1) * C_STATE],
                  axis=-1, keepdims=True) for g in range(C_GROUPS)]
    lane = lax.broadcasted_iota(jnp.int32, xs.shape, 1)
    y1_ref[...] = jnp.where(lane < per_g, cb[0], cb[1]) * xdt + dsk_ref[...] * xs


def _sample_mix(uv, ng_row, wd_row, b0_row, xbc, st, conv_w_l, conv_b_row, dt, hp, dsk_row, expand):
    ns = uv.shape[0]
    f = lambda w: jax.ShapeDtypeStruct((ns, w), _F32)
    return pl.pallas_call(
        _smix_kernel,
        out_shape=[f(A_WIDTH), f(A_WIDTH), f(C_WIDTH), f(C_WIDTH), f(C_WIDTH), f(2 * C_GROUPS * C_STATE)],
        name="sample_gmlp_conv",
    )(uv, ng_row, wd_row, b0_row, xbc, st, conv_w_l, conv_b_row, dt, hp, dsk_row, expand)


def _sssm_kernel(h0_ref, xdt_ref, ea_ref, y1_ref, bc_ref, z_ref, ng_ref, o_ref, hout_ref):
    per_g = C_WIDTH // C_GROUPS
    h0 = h0_ref[0]
    bc = bc_ref[0]
    bm = bc[:, :C_GROUPS * C_STATE]
    cm = bc[:, C_GROUPS * C_STATE:]
    c8 = [jnp.broadcast_to(cm[:, g * C_STATE:(g + 1) * C_STATE], (SUBLANES, C_STATE)) for g in range(C_GROUPS)]
    ch = jnp.concatenate([_dot_nt(c8[g], h0[g * per_g:(g + 1) * per_g, :])[0:1] for g in range(C_GROUPS)], axis=-1)
    y = y1_ref[0] + ch * ea_ref[0]
    o_ref[0] = _rms(y * _silu(z_ref[0]), ng_ref[...])
    row = lax.broadcasted_iota(jnp.int32, (LANES, C_WIDTH), 0)
    cols = jnp.where(row == 0, xdt_ref[0], jnp.where(row == 1, ea_ref[0], 0.0)).T
    rsel = lax.broadcasted_iota(jnp.int32, (C_WIDTH, C_STATE), 0) < per_g
    b_full = jnp.where(rsel, bm[:, :C_STATE], bm[:, C_STATE:])
    hout_ref[0] = h0 * cols[:, 1:2] + cols[:, 0:1] * b_full


def _sample_ssm(h0, xdt, ea, y1, bc, z, ng_row, l):
    ns = xdt.shape[0]
    r3 = lambda a: a.reshape(ns, 1, a.shape[-1])
    row_spec = lambda w: pl.BlockSpec((1, 1, w), lambda b: (b, 0, 0))
    out, hout = pl.pallas_call(
        _sssm_kernel,
        grid=(ns,),
        in_specs=[pl.BlockSpec((None, 1, C_WIDTH, C_STATE), lambda b: (l, b, 0, 0)),
                  row_spec(C_WIDTH), row_spec(C_WIDTH), row_spec(C_WIDTH), row_spec(2 * C_GROUPS * C_STATE),
                  row_spec(C_WIDTH), pl.BlockSpec((1, C_WIDTH), lambda b: (0, 0))],
        out_specs=[row_spec(C_WIDTH), pl.BlockSpec((1, C_WIDTH, C_STATE), lambda b: (b, 0, 0))],
        out_shape=[jax.ShapeDtypeStruct((ns, 1, C_WIDTH), _F32),
                   jax.ShapeDtypeStruct((ns, C_WIDTH, C_STATE), _F32)],
        compiler_params=_cparams(("parallel",)),
        name="sample_ssm",
    )(h0, r3(xdt), r3(ea), r3(y1), r3(bc), r3(z), ng_row)
    return out.reshape(ns, C_WIDTH), hout


def _pad_cols(a, width):
    return jnp.pad(a, ((0, 0),) * (a.ndim - 1) + ((0, width - a.shape[-1]),))


def _pack_w_in(w_in):
    o = np.cumsum((0, A_WIDTH, A_WIDTH, B_WIDTH, 6 * B_KV_WIDTH, _GATE_COLS, C_WIDTH, C_CONV_DIM, C_HEADS))
    u0, q0, kv0, gate0, z0, xbc0, dt0, end = o[0], o[2], o[3], o[4], o[5], o[6], o[7], o[8]
    parts = [w_in[..., u0:kv0], w_in[..., z0:xbc0], w_in[..., xbc0:dt0],
             _pad_cols(w_in[..., gate0:z0], LANES), _pad_cols(w_in[..., dt0:end], LANES)]
    w_rows = jnp.concatenate(parts, axis=-1).astype(_MXU)
    w_kv_t = jnp.swapaxes(w_in[..., kv0:gate0], -1, -2).astype(_MXU)
    return w_rows, w_kv_t


def _head_rows(dt_bias, a_log):
    hp = jnp.zeros((DEPTH, SUBLANES, LANES), _F32)
    hp = hp.at[:, 0, :C_HEADS].set(dt_bias)
    return hp.at[:, 1, :C_HEADS].set(a_log)


def kernel(x_prompt, x_sample, cache_cmp_kv, cache_slc_kv, page_table, state_win_kv, state_conv, state_ssm,
           norm_g, ffn_w_gate, ffn_w_up, ffn_w_down, w_in, w_out, gmlp_norm_g, gmlp_w_s, gmlp_b_s,
           nsa_pe_cmp, nsa_w_cmp, rel_bias, conv_w, conv_b, dt_bias, a_log, d_skip, ssm_norm_g):
    bp, t = x_prompt.shape[:2]
    ns = x_sample.shape[0]
    n_pages = page_table.shape[1]
    n_phys = cache_cmp_kv.shape[1]
    tm_p, tm_s = 512, ns
    assert t % _KEY_CHUNK == 0 and t % tm_p == 0

    wg, wu, wd = (w.astype(_MXU) for w in (ffn_w_gate, ffn_w_up, ffn_w_down))
    w_in_p, w_kv_t = _pack_w_in(w_in)
    w_out_b = w_out.astype(_MXU)
    pe_t = jnp.swapaxes(nsa_pe_cmp, -1, -2)
    wc_t = jnp.swapaxes(nsa_w_cmp, -1, -2)
    bs_t = jnp.swapaxes(gmlp_b_s, 1, 2)
    wdiag = jnp.repeat(gmlp_w_s[:, :, 0, 0], A_HEAD_DIM, axis=-1)
    b0 = jnp.repeat(gmlp_b_s[:, :, 0], A_HEAD_DIM, axis=-1)
    hp = _head_rows(dt_bias, a_log)
    dsk = jnp.repeat(d_skip, C_HEAD_DIM, axis=-1)
    expand = (jnp.arange(LANES)[:, None] == jnp.arange(C_WIDTH)[None, :] // C_HEAD_DIM).astype(_F32)
    rb_flat = rel_bias.reshape(-1)
    rbt = _pad_cols(rel_bias, SUBLANES).T
    def chan_major(a):
        lead, tok = a.shape[:-4], a.shape[-4]
        perm = tuple(range(len(lead))) + tuple(len(lead) + k for k in (1, 2, 3, 0))
        return a.transpose(perm).reshape(*lead, _KV_ROWS, tok)

    def token_major(a_t):
        lead, tok = a_t.shape[:-2], a_t.shape[-1]
        a5 = a_t.reshape(*lead, 2, B_KV_HEADS, B_HEAD_DIM, tok)
        perm = tuple(range(len(lead))) + tuple(len(lead) + k for k in (3, 0, 1, 2))
        return a5.transpose(perm)

    cache_c = chan_major(cache_cmp_kv)
    cache_s = chan_major(cache_slc_kv)
    win_state = chan_major(state_win_kv)
    ssm_state = state_ssm.reshape(DEPTH, ns, C_WIDTH, C_STATE)

    xp = x_prompt.reshape(bp * t, D_MODEL)
    xs = x_sample.reshape(ns, D_MODEL)
    outs = [[] for _ in range(11)]
    for l in range(DEPTH):
        g = norm_g[l]
        row = lambda a: a.reshape(1, -1)
        xp = _ffn(xp, g[0:2], wg, wu, wd, l, 0, tm_p)
        xs = _ffn(xs, g[0:2], wg, wu, wd, l, 0, tm_s)

        uv, q, z, xbc, gate, dt, kvc_t, kvs_t, kvw_t = _inproj(xp.reshape(bp, t, D_MODEL), row(g[2]), w_in_p,
                                                               w_kv_t, l, tm_p)
        oa = _gmlp_prompt(uv.reshape(bp * t, 2 * A_WIDTH), row(gmlp_norm_g[l]), gmlp_w_s, bs_t, l)
        kvcmp_t = _compress_prompt(kvc_t, pe_t, wc_t, l)
        ob = _nsa_prompt(q, gate, *_nsa_prompt_inputs(kvcmp_t, kvs_t, kvw_t), rb_flat)
        oc, h_p = _mamba_prompt(z, xbc, dt, conv_w, row(conv_b[l]), hp[l], row(dsk[l]), row(ssm_norm_g[l]), l)
        xp = _outproj(xp, oa, ob.reshape(bp * t, B_WIDTH), oc.reshape(bp * t, C_WIDTH), row(g[3]), w_out_b, l, tm_p)
        wkeep = min(WINDOW, t)
        outs[0].append(token_major(kvc_t))
        outs[1].append(token_major(kvs_t))
        outs[2].append(token_major(kvw_t[:, :, t - wkeep:]))
        outs[3].append(xbc[:, t - (C_CONV - 1):])
        outs[4].append(h_p)

        uv, q, z, xbc, gate, dt, kvc_t, kvs_t, kvw_t = (
            a[0] for a in _inproj(xs.reshape(1, ns, D_MODEL), row(g[2]), w_in_p, w_kv_t, l, tm_s))
        q8 = jnp.pad(q.reshape(ns, B_HEADS, B_HEAD_DIM), ((0, 0), (0, SUBLANES - B_HEADS), (0, 0)))
        gate8 = jnp.pad(gate[:, :_GATE_COLS].reshape(ns, B_HEADS, N_BRANCH),
                        ((0, 0), (0, SUBLANES - B_HEADS), (0, LANES - N_BRANCH)))
        o_cmp, imp = _sample_cmp(page_table, cache_c, q8, pe_t, wc_t, rbt, l)
        n_sel_past = imp.shape[-1]
        idx = _sample_topk(imp.reshape(ns * SUBLANES, n_sel_past), SEL_TOPK - 1)
        idx = idx.reshape(ns, SUBLANES, n_sel_past)[:, :B_KV_HEADS, :SEL_TOPK - 1]
        kvc, kvs, kvwin = kvc_t.T, kvs_t.T, kvw_t.T
        kv_new = jnp.concatenate([kvc, kvs, kvwin], axis=-1).reshape(ns, 1, 6 * B_KV_WIDTH)
        ob8 = _sample_attn(page_table, idx, cache_s, q8, gate8, kv_new, o_cmp, win_state, rbt, l)
        ob = ob8[:, :B_HEADS].reshape(ns, B_WIDTH)
        st = jnp.swapaxes(state_conv[l], 0, 1)
        oa, v_rows, xdt, ea, y1, bc = _sample_mix(uv, row(gmlp_norm_g[l]), row(wdiag[l]), row(b0[l]), xbc, st,
                                                  conv_w[l], row(conv_b[l]), dt, hp[l], row(dsk[l]), expand)
        oc, h_s = _sample_ssm(ssm_state, xdt, ea, y1, bc, z, row(ssm_norm_g[l]), l)
        xs = _outproj(xs, oa, ob, oc, row(g[3]), w_out_b, l, tm_s)
        outs[5].append(kvc.reshape(ns, 1, 2, B_KV_HEADS, B_HEAD_DIM))
        outs[6].append(kvs.reshape(ns, 1, 2, B_KV_HEADS, B_HEAD_DIM))
        outs[7].append(token_major(jnp.concatenate([win_state[l][:, :, 1:], kvwin[:, :, None]], axis=-1)))
        outs[8].append(jnp.concatenate([state_conv[l][:, 1:], xbc[:, None]], axis=1))
        outs[9].append(h_s.reshape(ns, C_HEADS, C_HEAD_DIM, C_STATE))
        outs[10].append(v_rows.reshape(ns, 1, A_WIDTH))

        xp = _ffn(xp, g[4:6], wg, wu, wd, l, 1, tm_p)
        xs = _ffn(xs, g[4:6], wg, wu, wd, l, 1, tm_s)
    stacked = [jnp.stack(o) for o in outs]
    return (xp.reshape(bp, t, D_MODEL), xs.reshape(ns, 1, D_MODEL), *stacked)
```

```python
import functools
import math

import numpy as np
import jax
import jax.numpy as jnp
from jax import lax
from jax.experimental import pallas as pl
from jax.experimental.pallas import tpu as pltpu

D_MODEL = 1024
DEPTH = 4
PAGE_SIZE = 128
A_HEADS, A_HEAD_DIM, A_CHUNK = 4, 64, 128
A_WIDTH = A_HEADS * A_HEAD_DIM
B_HEADS, B_KV_HEADS, B_HEAD_DIM = 6, 2, 64
B_GROUP = B_HEADS // B_KV_HEADS
B_WIDTH = B_HEADS * B_HEAD_DIM
B_KV_WIDTH = B_KV_HEADS * B_HEAD_DIM
N_BRANCH = 3
CMP_BLOCK, SEL_BLOCK, SEL_TOPK, WINDOW, Q_BLOCK = 32, 64, 16, 512, 128
FORCE_SCORE = 1e4
C_HEADS, C_HEAD_DIM, C_GROUPS, C_STATE, C_CONV = 6, 64, 2, 64, 4
C_WIDTH = C_HEADS * C_HEAD_DIM
C_CONV_DIM = C_WIDTH + 2 * C_GROUPS * C_STATE
SSD_CHUNK = 128
D_FF = 2816
REL_BUCKETS, REL_MAX_EXACT, REL_MAX_DIST = 32, 16, 128
EPS = 1e-6
NEG_INF = -1e30
SOFTMAX_FLOOR = -1e20

LANES = 128
SUBLANES = 8
VMEM_LIMIT_BYTES = 56 * 1024 * 1024

_MXU = jnp.bfloat16
_F32 = jnp.float32

_GATE_COLS = N_BRANCH * B_HEADS
_SEG = (("uv", 2 * A_WIDTH), ("q", B_WIDTH), ("z", C_WIDTH), ("xbc", C_CONV_DIM), ("gate", LANES), ("dt", LANES))
_D_IN_PAD = sum(w for _, w in _SEG)
_KV_ROWS = 2 * B_KV_WIDTH
_N_KV = 3


def _dot(a, b):
    return jnp.dot(a.astype(_MXU), b.astype(_MXU), preferred_element_type=_F32)


def _dot_nt(a, b):
    return lax.dot_general(a.astype(_MXU), b.astype(_MXU), (((1,), (1,)), ((), ())),
                           preferred_element_type=_F32)


def _dot_tn(a, b):
    return lax.dot_general(a.astype(_MXU), b.astype(_MXU), (((0,), (0,)), ((), ())),
                           preferred_element_type=_F32)


def _dot_exact(a, b):
    return jnp.dot(a, b, preferred_element_type=_F32, precision=lax.Precision.HIGHEST)


def _rms(x, g):
    return x * lax.rsqrt(jnp.mean(x * x, axis=-1, keepdims=True) + EPS) * g


def _silu(x):
    return x * jax.nn.sigmoid(x)


def _cparams(sem):
    return pltpu.CompilerParams(dimension_semantics=sem, vmem_limit_bytes=VMEM_LIMIT_BYTES)


def _bucket_np(dist):
    n = np.maximum(dist, 0)
    nf = np.maximum(n, 1).astype(np.float32)
    large = REL_MAX_EXACT + (np.log(nf / np.float32(REL_MAX_EXACT))
                             / np.float32(math.log(REL_MAX_DIST / REL_MAX_EXACT))
                             * np.float32(REL_BUCKETS - REL_MAX_EXACT)).astype(np.int32)
    large = np.minimum(large, REL_BUCKETS - 1)
    return np.where(n < REL_MAX_EXACT, n, large).astype(np.int32)


def _ffn_kernel(x_ref, g_ref, wg_ref, wu_ref, wd_ref, o_ref, *, tf):
    x = x_ref[...]
    h = _rms(x, g_ref[0:1, :]).astype(_MXU)
    y = jnp.zeros(x.shape, _F32)
    for f in range(D_FF // tf):
        cols = slice(f * tf, (f + 1) * tf)
        a = _silu(_dot(h, wg_ref[:, cols])) * _dot(h, wu_ref[:, cols])
        y = y + _dot(a, wd_ref[cols, :])
    o_ref[...] = x + 0.5 * _rms(y, g_ref[1:2, :])


def _ffn(x, g2, wg, wu, wd, l, j, tm, tf=256):
    rows = x.shape[0]
    once = pl.Buffered(1)
    return pl.pallas_call(
        functools.partial(_ffn_kernel, tf=tf),
        grid=(rows // tm,),
        in_specs=[pl.BlockSpec((tm, D_MODEL), lambda r: (r, 0)),
                  pl.BlockSpec((2, D_MODEL), lambda r: (0, 0)),
                  pl.BlockSpec((None, None, D_MODEL, D_FF), lambda r: (l, j, 0, 0), pipeline_mode=once),
                  pl.BlockSpec((None, None, D_MODEL, D_FF), lambda r: (l, j, 0, 0), pipeline_mode=once),
                  pl.BlockSpec((None, None, D_FF, D_MODEL), lambda r: (l, j, 0, 0), pipeline_mode=once)],
        out_specs=pl.BlockSpec((tm, D_MODEL), lambda r: (r, 0)),
        out_shape=jax.ShapeDtypeStruct((rows, D_MODEL), _F32),
        compiler_params=_cparams(("parallel",)),
        name="half_ffn",
    )(x, g2, wg, wu, wd)


def _inproj_kernel(x_ref, g_ref, w_ref, wkv_ref, *o_refs):
    h = _rms(x_ref[0], g_ref[...]).astype(_MXU)
    off = 0
    for (_, width), o_ref in zip(_SEG, o_refs):
        o_ref[0] = _dot(h, w_ref[:, off:off + width])
        off += width
    for k, o_ref in enumerate(o_refs[len(_SEG):]):
        o_ref[0] = _dot_nt(wkv_ref[k * _KV_ROWS:(k + 1) * _KV_ROWS, :], h)


def _inproj(x, g_row, w_in_p, w_kv_t, l, tm):
    nb, t = x.shape[:2]
    return pl.pallas_call(
        _inproj_kernel,
        grid=(nb, t // tm),
        in_specs=[pl.BlockSpec((1, tm, D_MODEL), lambda b, r: (b, r, 0)),
                  pl.BlockSpec((1, D_MODEL), lambda b, r: (0, 0)),
                  pl.BlockSpec((None, D_MODEL, _D_IN_PAD), lambda b, r: (l, 0, 0)),
                  pl.BlockSpec((None, _N_KV * _KV_ROWS, D_MODEL), lambda b, r: (l, 0, 0))],
        out_specs=[pl.BlockSpec((1, tm, w), lambda b, r: (b, r, 0)) for _, w in _SEG]
        + [pl.BlockSpec((1, _KV_ROWS, tm), lambda b, r: (b, 0, r))] * _N_KV,
        out_shape=[jax.ShapeDtypeStruct((nb, t, w), _F32) for _, w in _SEG]
        + [jax.ShapeDtypeStruct((nb, _KV_ROWS, t), _F32)] * _N_KV,
        compiler_params=_cparams(("parallel", "parallel")),
        name="in_proj",
    )(x, g_row, w_in_p, w_kv_t)


def _outproj_kernel(x_ref, oa_ref, ob_ref, oc_ref, g_ref, w_ref, o_ref):
    y = (_dot(oa_ref[...], w_ref[0:A_WIDTH, :])
         + _dot(ob_ref[...], w_ref[A_WIDTH:A_WIDTH + B_WIDTH, :])
         + _dot(oc_ref[...], w_ref[A_WIDTH + B_WIDTH:, :]))
    o_ref[...] = x_ref[...] + _rms(y, g_ref[...])


def _outproj(x, oa, ob, oc, g_row, w_out, l, tm):
    rows = x.shape[0]
    return pl.pallas_call(
        _outproj_kernel,
        grid=(rows // tm,),
        in_specs=[pl.BlockSpec((tm, D_MODEL), lambda r: (r, 0)),
                  pl.BlockSpec((tm, A_WIDTH), lambda r: (r, 0)),
                  pl.BlockSpec((tm, B_WIDTH), lambda r: (r, 0)),
                  pl.BlockSpec((tm, C_WIDTH), lambda r: (r, 0)),
                  pl.BlockSpec((1, D_MODEL), lambda r: (0, 0)),
                  pl.BlockSpec((None, D_MODEL, D_MODEL), lambda r: (l, 0, 0))],
        out_specs=pl.BlockSpec((tm, D_MODEL), lambda r: (r, 0)),
        out_shape=jax.ShapeDtypeStruct((rows, D_MODEL), _F32),
        compiler_params=_cparams(("parallel",)),
        name="out_proj",
    )(x, oa, ob, oc, g_row, w_out)


def _gelu_ln(uv, ng):
    u = jax.nn.gelu(uv[:, :A_WIDTH])
    v = jax.nn.gelu(uv[:, A_WIDTH:])
    mu = jnp.mean(v, axis=-1, keepdims=True)
    var = jnp.mean(jnp.square(v - mu), axis=-1, keepdims=True)
    return u, (v - mu) * lax.rsqrt(var + EPS) * ng


def _gmlp_kernel(uv_ref, ng_ref, ws_ref, bs_ref, o_ref, *, chunks):
    row = lax.broadcasted_iota(jnp.int32, (A_CHUNK, A_CHUNK), 0)
    col = lax.broadcasted_iota(jnp.int32, (A_CHUNK, A_CHUNK), 1)
    ws = [jnp.where(col <= row, ws_ref[h], 0.0).astype(_MXU) for h in range(A_HEADS)]
    for c in range(chunks):
        u, v = _gelu_ln(uv_ref[c * A_CHUNK:(c + 1) * A_CHUNK, :], ng_ref[...])
        sg = [_dot(ws[h], v[:, h * A_HEAD_DIM:(h + 1) * A_HEAD_DIM]) + bs_ref[:, h:h + 1]
              for h in range(A_HEADS)]
        o_ref[c * A_CHUNK:(c + 1) * A_CHUNK, :] = u * jnp.concatenate(sg, axis=-1)


def _gmlp_prompt(uv, ng_row, ws, bs_t, l, chunks=4):
    rows = uv.shape[0]
    tm = chunks * A_CHUNK
    return pl.pallas_call(
        functools.partial(_gmlp_kernel, chunks=chunks),
        grid=(rows // tm,),
        in_specs=[pl.BlockSpec((tm, 2 * A_WIDTH), lambda r: (r, 0)),
                  pl.BlockSpec((1, A_WIDTH), lambda r: (0, 0)),
                  pl.BlockSpec((None, A_HEADS, A_CHUNK, A_CHUNK), lambda r: (l, 0, 0, 0)),
                  pl.BlockSpec((None, A_CHUNK, A_HEADS), lambda r: (l, 0, 0))],
        out_specs=pl.BlockSpec((tm, A_WIDTH), lambda r: (r, 0)),
        out_shape=jax.ShapeDtypeStruct((rows, A_WIDTH), _F32),
        compiler_params=_cparams(("parallel",)),
        name="gmlp_prompt",
    )(uv, ng_row, ws, bs_t)


def _mamba_kernel(z_ref, xbc_ref, dt_ref, cw_ref, cb_ref, hp_ref, dsk_ref, ng_ref,
                  o_ref, hout_ref, xp_ref, hs_ref):
    t = pl.program_id(0)
    L = SSD_CHUNK
    hist = SUBLANES

    @pl.when(t == 0)
    def _():
        xp_ref[:, 0:hist, :] = jnp.zeros((xp_ref.shape[0], hist, C_CONV_DIM), _F32)
        hs_ref[...] = jnp.zeros_like(hs_ref)

    a_row = -jnp.exp(hp_ref[1:2, :])
    row = lax.broadcasted_iota(jnp.int32, (L, L), 0)
    col = lax.broadcasted_iota(jnp.int32, (L, L), 1)
    causal = col <= row
    tril = jnp.where(causal, 1.0, 0.0)
    for b in range(z_ref.shape[0]):
        xp_ref[b, hist:hist + L, :] = xbc_ref[b]
        conv = cb_ref[...]
        for k in range(C_CONV):
            conv = conv + xp_ref[b, pl.ds(hist - (C_CONV - 1) + k, L), :] * cw_ref[k:k + 1, :]
        xp_ref[b, 0:hist, :] = xp_ref[b, L:L + hist, :]
        xc = _silu(conv)
        xs = xc[:, :C_WIDTH]
        bm = xc[:, C_WIDTH:C_WIDTH + C_GROUPS * C_STATE]
        cm = xc[:, C_WIDTH + C_GROUPS * C_STATE:]

        dt = jax.nn.softplus(dt_ref[b] + hp_ref[0:1, :])
        acum = _dot_exact(tril, dt * a_row)
        acum_t = acum.T
        dt_t = dt.T
        cb = [_dot_nt(cm[:, g * C_STATE:(g + 1) * C_STATE], bm[:, g * C_STATE:(g + 1) * C_STATE])
              for g in range(C_GROUPS)]
        ys = []
        for h in range(C_HEADS):
            g = h // (C_HEADS // C_GROUPS)
            ac_col = acum[:, h:h + 1]
            seg = ac_col - acum_t[h:h + 1, :]
            decay = jnp.where(causal, jnp.exp(jnp.where(causal, seg, 0.0)), 0.0)
            scores = cb[g] * decay * dt_t[h:h + 1, :]
            x_h = xs[:, h * C_HEAD_DIM:(h + 1) * C_HEAD_DIM]
            b_g = bm[:, g * C_STATE:(g + 1) * C_STATE]
            c_g = cm[:, g * C_STATE:(g + 1) * C_STATE]
            hs = hs_ref[b, h]
            ys.append(_dot(scores, x_h) + _dot_nt(c_g, hs) * jnp.exp(ac_col))
            ac_last = acum[L - 1:L, h:h + 1]
            w_end = jnp.exp(ac_last - ac_col) * dt[:, h:h + 1]
            hs_ref[b, h] = hs * jnp.exp(ac_last) + _dot_tn(x_h * w_end, b_g)
        y = jnp.concatenate(ys, axis=-1) + dsk_ref[...] * xs
        o_ref[b] = _rms(y * _silu(z_ref[b]), ng_ref[...])

    @pl.when(t == pl.num_programs(0) - 1)
    def _():
        hout_ref[...] = hs_ref[...]


def _mamba_prompt(z, xbc, dt, conv_w, conv_b_row, hp, dsk_row, ng_row, l):
    nb, t = z.shape[:2]
    L = SSD_CHUNK
    return pl.pallas_call(
        _mamba_kernel,
        grid=(t // L,),
        in_specs=[pl.BlockSpec((nb, L, C_WIDTH), lambda c: (0, c, 0)),
                  pl.BlockSpec((nb, L, C_CONV_DIM), lambda c: (0, c, 0)),
                  pl.BlockSpec((nb, L, LANES), lambda c: (0, c, 0)),
                  pl.BlockSpec((None, C_CONV, C_CONV_DIM), lambda c: (l, 0, 0)),
                  pl.BlockSpec((1, C_CONV_DIM), lambda c: (0, 0)),
                  pl.BlockSpec((SUBLANES, LANES), lambda c: (0, 0)),
                  pl.BlockSpec((1, C_WIDTH), lambda c: (0, 0)),
                  pl.BlockSpec((1, C_WIDTH), lambda c: (0, 0))],
        out_specs=[pl.BlockSpec((nb, L, C_WIDTH), lambda c: (0, c, 0)),
                   pl.BlockSpec((nb, C_HEADS, C_HEAD_DIM, C_STATE), lambda c: (0, 0, 0, 0))],
        out_shape=[jax.ShapeDtypeStruct((nb, t, C_WIDTH), _F32),
                   jax.ShapeDtypeStruct((nb, C_HEADS, C_HEAD_DIM, C_STATE), _F32)],
        scratch_shapes=[pltpu.VMEM((nb, L + 2 * SUBLANES, C_CONV_DIM), _F32),
                        pltpu.VMEM((nb, C_HEADS, C_HEAD_DIM, C_STATE), _F32)],
        compiler_params=_cparams(("arbitrary",)),
        name="mamba_prompt",
    )(z, xbc, dt, conv_w, conv_b_row, hp, dsk_row, ng_row)


def _split_dot(x, p):
    hi = x.astype(_MXU)
    lo = (x - hi.astype(_F32)).astype(_MXU)
    return jnp.dot(hi, p, preferred_element_type=_F32) + jnp.dot(lo, p, preferred_element_type=_F32)


def _pool_matrix(n_tok, per, n_out, first=0):
    r = lax.broadcasted_iota(jnp.int32, (n_tok, n_out), 0)
    c = lax.broadcasted_iota(jnp.int32, (n_tok, n_out), 1)
    return jnp.where(c == first + r // per, 1.0, 0.0).astype(_MXU)


def _pe_sums_t(pet_ref):
    pk = jnp.sum(pet_ref[0], axis=-1, keepdims=True)
    pv = jnp.sum(pet_ref[1], axis=-1, keepdims=True)
    return jnp.concatenate([pk, pk, pv, pv], axis=0)


def _compress_cols(m, wt_ref):
    hd = B_HEAD_DIM
    return jnp.concatenate([_dot(wt_ref[0], m[0:hd]), _dot(wt_ref[0], m[hd:2 * hd]),
                            _dot(wt_ref[1], m[2 * hd:3 * hd]), _dot(wt_ref[1], m[3 * hd:])], axis=0)


def _compress_kernel(kv_ref, pet_ref, wt_ref, o_ref, *, chunk):
    t = kv_ref.shape[-1]
    pool = _pool_matrix(chunk, CMP_BLOCK, chunk // CMP_BLOCK)
    sums = jnp.concatenate([_split_dot(kv_ref[0, :, c * chunk:(c + 1) * chunk], pool) for c in range(t // chunk)],
                           axis=-1)
    o_ref[0] = _compress_cols((sums + _pe_sums_t(pet_ref)) * (1.0 / CMP_BLOCK), wt_ref)


def _compress_prompt(kvc_t, pe_t, w_t, l, chunk=2048):
    nb, _, t = kvc_t.shape
    return pl.pallas_call(
        functools.partial(_compress_kernel, chunk=min(chunk, t)),
        grid=(nb,),
        in_specs=[pl.BlockSpec((1, _KV_ROWS, t), lambda b: (b, 0, 0)),
                  pl.BlockSpec((None, 2, B_HEAD_DIM, CMP_BLOCK), lambda b: (l, 0, 0, 0)),
                  pl.BlockSpec((None, 2, B_HEAD_DIM, B_HEAD_DIM), lambda b: (l, 0, 0, 0))],
        out_specs=pl.BlockSpec((1, _KV_ROWS, t // CMP_BLOCK), lambda b: (b, 0, 0)),
        out_shape=jax.ShapeDtypeStruct((nb, _KV_ROWS, t // CMP_BLOCK), _F32),
        compiler_params=_cparams(("parallel",)),
        name="nsa_compress_prompt",
    )(kvc_t, pe_t, w_t)


_KEY_CHUNK = 512
_FRONT_PAD = Q_BLOCK
_NEAR = 2 * Q_BLOCK
_MAIN_UNROLL = 2
_Q_PER_STEP = 1
_V_ROWS = B_HEAD_DIM + 16


def _nsa_tables(n_sel):
    r = np.arange(Q_BLOCK)[:, None]
    cmp_idx = np.full((2, Q_BLOCK, n_sel), REL_BUCKETS - 1, np.int32)
    for par in range(2):
        for u in (-2, -1, 0, 1):
            dist = r[:, 0] - (CMP_BLOCK - 1) - CMP_BLOCK * (2 * u + par)
            cmp_idx[par, :, u % n_sel] = _bucket_np(dist)
    c = np.arange(_NEAR)[None, :]
    dist = Q_BLOCK + r - c
    near_idx = np.where(dist >= 0, _bucket_np(dist), -1).astype(np.int32)
    c = np.arange(WINDOW + Q_BLOCK)[None, :]
    dist = r + WINDOW - c
    win_idx = np.where((dist >= 0) & (dist < WINDOW), _bucket_np(dist), -1).astype(np.int32)
    return cmp_idx, near_idx, win_idx


def _fill_bias(idx, rb_ref, hg, rel_to_last):
    base = rb_ref[(REL_BUCKETS - 1) * B_HEADS + hg] if rel_to_last else 0.0
    tile = jnp.where(idx < 0, NEG_INF, 0.0)
    for b in range(REL_BUCKETS):
        tile = jnp.where(idx == b, rb_ref[b * B_HEADS + hg] - base, tile)
    return tile


def _topk_mask(score, k, taken):
    n = float(score.shape[-1])
    lane = lax.broadcasted_iota(jnp.int32, score.shape, 1).astype(_F32)
    s = jnp.where(taken, -jnp.inf, score)
    for _ in range(k):
        m = jnp.max(s, axis=-1, keepdims=True)
        first = jnp.min(jnp.where(s == m, lane, n), axis=-1, keepdims=True)
        s = jnp.where(lane == first, -jnp.inf, s)
    return s == -jnp.inf


def _masked_softmax(l):
    m = jnp.maximum(jnp.max(l, axis=-1, keepdims=True), SOFTMAX_FLOOR)
    e = jnp.exp(l - m)
    return e / jnp.maximum(jnp.sum(e, axis=-1, keepdims=True), 1e-20)


def _online_step(s, vt, carry):
    m_run, acc = carry
    m_new = jnp.maximum(m_run, jnp.max(s, axis=-1, keepdims=True))
    return m_new, jnp.exp(m_run - m_new) * acc + _dot_nt(jnp.exp(s - m_new), vt)


def _key_tiles(ref, first, n, rows=slice(None)):
    return jnp.concatenate([ref[0, first + j, rows, :] for j in range(n)], axis=-1)


def _nsa_kernel(q_ref, gt_ref, kc_ref, vc_ref, ksa_ref, vs_ref, kwa_ref, vwa_ref,
                cidx_ref, nidx_ref, widx_ref, rb_ref, o_ref, bc_ref, bn_ref, bw_ref, *, n_sel):
    step = pl.program_id(1)
    hd, G = B_HEAD_DIM, B_GROUP
    QB = Q_BLOCK

    @pl.when((pl.program_id(0) == 0) & (step == 0))
    def _():
        for hg in range(B_HEADS):
            for par in range(2):
                bc_ref[hg, par] = _fill_bias(cidx_ref[par], rb_ref, hg, False)
            bn_ref[hg] = _fill_bias(nidx_ref[...], rb_ref, hg, True)
            bw_ref[hg] = _fill_bias(widx_ref[...], rb_ref, hg, False)

    heads = range(B_KV_HEADS)
    hrows = [slice(h * hd, (h + 1) * hd) for h in heads]
    vrows = [slice(h * _V_ROWS, (h + 1) * _V_ROWS) for h in heads]
    r_col = lax.broadcasted_iota(jnp.int32, (QB, 1), 0)
    n_lane = lax.broadcasted_iota(jnp.int32, (QB, n_sel), 1)
    c_near = lax.broadcasted_iota(jnp.int32, (QB, _NEAR), 1)
    c_win = lax.broadcasted_iota(jnp.int32, (QB, WINDOW + QB), 1)
    zeros_h = jnp.zeros((G * QB, hd), _F32)
    n_win = (WINDOW + QB) // LANES
    per_chunk = _KEY_CHUNK // LANES

    def front(i, q, gate_logits):
        q = q * (hd ** -0.5)
        qpos = i * QB + r_col
        cur = qpos // SEL_BLOCK
        vis = [CMP_BLOCK * (2 * n_lane + par) + (CMP_BLOCK - 1) <= qpos for par in range(2)]
        vis3 = jnp.concatenate([jnp.concatenate(vis, axis=-1)] * G, axis=0)
        win_ok = jnp.concatenate([c_win >= WINDOW - i * QB] * G, axis=0)
        q3 = [jnp.concatenate([q[:, (h * G + g) * hd:(h * G + g + 1) * hd] for g in range(G)], axis=0)
              for h in heads]
        q3h = [jnp.concatenate([q3[h], zeros_h] if h == 0 else [zeros_h, q3[h]], axis=-1) for h in heads]
        t0 = i * (QB // LANES)

        forced = (n_lane == cur) | (n_lane == 0)
        started = n_lane <= cur
        o_c, imp = [], []
        for h in heads:
            lc = _dot(q3[h], kc_ref[0, hrows[h], :])
            bias_c = jnp.concatenate(
                [jnp.concatenate([pltpu.roll(bc_ref[h * G + g, par], 2 * i, 1) for par in range(2)], axis=-1)
                 for g in range(G)], axis=0)
            p_c = _masked_softmax(jnp.where(vis3, lc + bias_c, NEG_INF))
            o_c.append(_dot_nt(p_c, vc_ref[0, hrows[h], :]))
            imp_h = sum(p_c[g * QB:(g + 1) * QB, :n_sel] + p_c[g * QB:(g + 1) * QB, n_sel:] for g in range(G))
            imp.append(jnp.where(started, imp_h, -1.0))

        chosen = _topk_mask(jnp.concatenate(imp, axis=0), min(SEL_TOPK, n_sel) - 2,
                            jnp.concatenate([forced] * B_KV_HEADS, axis=0))
        kw = _key_tiles(kwa_ref, t0, n_win)
        o_w = []
        for h in heads:
            bias_w = jnp.concatenate([bw_ref[h * G + g] for g in range(G)], axis=0)
            p_w = _masked_softmax(jnp.where(win_ok, _dot(q3h[h], kw) + bias_w, NEG_INF))
            o_w.append(_dot_nt(p_w, _key_tiles(vwa_ref, t0, n_win, hrows[h])))

        qa_main, qa_near = [], []
        for h in heads:
            allowed = chosen[h * QB:(h + 1) * QB] & started
            m_main = jnp.where(allowed & (n_lane < 2 * i - 2), 0.0, NEG_INF)
            m_near = jnp.where(allowed, 0.0, NEG_INF)
            qa_main.append(jnp.concatenate([q3h[h], jnp.concatenate([m_main] * G, axis=0)], axis=-1).astype(_MXU))
            qa_near.append(jnp.concatenate([q3h[h], jnp.concatenate([m_near] * G, axis=0)], axis=-1).astype(_MXU))
        return dict(i=i, gate=jax.nn.sigmoid(gate_logits), o_c=o_c, o_w=o_w, qa_main=qa_main, qa_near=qa_near)

    def back(f, carry):
        i = f["i"]
        t0 = i * (QB // LANES)
        near_ok = jnp.concatenate([c_near >= _FRONT_PAD - i * QB] * G, axis=0)
        ks = _key_tiles(ksa_ref, t0, _NEAR // LANES)
        outs = []
        for h in heads:
            corr = jnp.concatenate([bn_ref[h * G + g] for g in range(G)], axis=0)
            s = jnp.where(near_ok, _dot(f["qa_near"][h], ks) + corr, NEG_INF)
            _, acc_s = _online_step(s, _key_tiles(vs_ref, t0, _NEAR // LANES, vrows[h]), carry[h])
            o_s = acc_s[:, :hd] / acc_s[:, hd:hd + 1]
            for g in range(G):
                k0 = (h * G + g) * N_BRANCH
                rows = slice(g * QB, (g + 1) * QB)
                gate = f["gate"]
                outs.append(gate[:, k0:k0 + 1] * f["o_c"][h][rows] + gate[:, k0 + 1:k0 + 2] * o_s[rows]
                            + gate[:, k0 + 2:k0 + 3] * f["o_w"][h][rows])
        return jnp.concatenate(outs, axis=-1)

    blocks = [front(step * _Q_PER_STEP + u, q_ref[0, u * QB:(u + 1) * QB, :], gt_ref[0, u * QB:(u + 1) * QB, :])
              for u in range(_Q_PER_STEP)]
    n_chain = _Q_PER_STEP * B_KV_HEADS
    qa_main_all = jnp.concatenate([f["qa_main"][h] for f in blocks for h in heads], axis=0)
    i_last = step * _Q_PER_STEP + _Q_PER_STEP - 1
    n_main = (jnp.maximum(i_last - 1, 0) * QB + _KEY_CHUNK - 1) // _KEY_CHUNK

    def main_body(c, carry):
        for u in range(_MAIN_UNROLL):
            tc = _FRONT_PAD // LANES + (c * _MAIN_UNROLL + u) * per_chunk
            s = _dot(qa_main_all, _key_tiles(ksa_ref, tc, per_chunk))
            carry = tuple(_online_step(s[k * G * QB:(k + 1) * G * QB],
                                       _key_tiles(vs_ref, tc, per_chunk, vrows[k % B_KV_HEADS]), carry[k])
                          for k in range(n_chain))
        return carry

    init = (jnp.full((G * QB, 1), SOFTMAX_FLOOR, _F32), jnp.zeros((G * QB, _V_ROWS), _F32))
    carry = lax.fori_loop(0, (n_main + _MAIN_UNROLL - 1) // _MAIN_UNROLL, main_body, (init,) * n_chain)
    for u, f in enumerate(blocks):
        o_ref[0, u * QB:(u + 1) * QB, :] = back(f, carry[u * B_KV_HEADS:(u + 1) * B_KV_HEADS])


def _nsa_prompt(q, gate, kcp, vcp, ksa, vs, kwa, vwa, rb_flat):
    nb, t = q.shape[:2]
    n_sel = t // SEL_BLOCK
    cidx, nidx, widx = _nsa_tables(n_sel)
    tp, tw = ksa.shape[1], kwa.shape[1]
    q_rows = _Q_PER_STEP * Q_BLOCK
    assert t % q_rows == 0
    full = lambda shape: pl.BlockSpec(shape, lambda b, i: (0,) * len(shape))
    return pl.pallas_call(
        functools.partial(_nsa_kernel, n_sel=n_sel),
        grid=(nb, t // q_rows),
        in_specs=[pl.BlockSpec((1, q_rows, B_WIDTH), lambda b, i: (b, i, 0)),
                  pl.BlockSpec((1, q_rows, LANES), lambda b, i: (b, i, 0)),
                  pl.BlockSpec((1, B_KV_WIDTH, 2 * n_sel), lambda b, i: (b, 0, 0)),
                  pl.BlockSpec((1, B_KV_WIDTH, 2 * n_sel), lambda b, i: (b, 0, 0)),
                  pl.BlockSpec((1, tp, B_KV_WIDTH + n_sel, LANES), lambda b, i: (b, 0, 0, 0)),
                  pl.BlockSpec((1, tp, B_KV_HEADS * _V_ROWS, LANES), lambda b, i: (b, 0, 0, 0)),
                  pl.BlockSpec((1, tw, B_KV_WIDTH, LANES), lambda b, i: (b, 0, 0, 0)),
                  pl.BlockSpec((1, tw, B_KV_WIDTH, LANES), lambda b, i: (b, 0, 0, 0)),
                  full(cidx.shape), full(nidx.shape), full(widx.shape),
                  pl.BlockSpec(memory_space=pltpu.SMEM)],
        out_specs=pl.BlockSpec((1, q_rows, B_WIDTH), lambda b, i: (b, i, 0)),
        out_shape=jax.ShapeDtypeStruct((nb, t, B_WIDTH), _F32),
        scratch_shapes=[pltpu.VMEM((B_HEADS, 2, Q_BLOCK, n_sel), _F32),
                        pltpu.VMEM((B_HEADS, Q_BLOCK, _NEAR), _F32),
                        pltpu.VMEM((B_HEADS, Q_BLOCK, WINDOW + Q_BLOCK), _F32)],
        compiler_params=_cparams(("arbitrary", "arbitrary")),
        name="nsa_prompt",
    )(q, gate, kcp, vcp, ksa, vs, kwa, vwa, jnp.asarray(cidx), jnp.asarray(nidx), jnp.asarray(widx), rb_flat)


def _nsa_prompt_inputs(kvcmp_t, kvs_t, kvw_t):
    nb, _, t = kvs_t.shape
    n_sel = t // SEL_BLOCK
    kvp = kvcmp_t.reshape(nb, _KV_ROWS, n_sel, 2).transpose(0, 1, 3, 2).reshape(nb, _KV_ROWS, 2 * n_sel)
    kcp = kvp[:, :B_KV_WIDTH].astype(_MXU)
    vcp = kvp[:, B_KV_WIDTH:].astype(_MXU)
    blk = (jnp.arange(n_sel)[:, None] == jnp.arange(t)[None, :] // SEL_BLOCK).astype(_MXU)
    ksa = jnp.concatenate([kvs_t[:, :B_KV_WIDTH].astype(_MXU), jnp.broadcast_to(blk, (nb, n_sel, t))], axis=1)
    pad_s = ((0, 0), (0, 0), (_FRONT_PAD, _MAIN_UNROLL * _KEY_CHUNK - _FRONT_PAD))
    pad_w = ((0, 0), (0, 0), (WINDOW, 0))
    def tiles(a, pad):
        a = jnp.pad(a, pad)
        return a.reshape(nb, a.shape[1], a.shape[2] // LANES, LANES).transpose(0, 2, 1, 3)

    ones = jnp.pad(jnp.ones((nb, 1, t), _MXU), ((0, 0), (0, _V_ROWS - B_HEAD_DIM - 1), (0, 0)))
    v_t = kvs_t[:, B_KV_WIDTH:].astype(_MXU)
    vs = tiles(jnp.concatenate([a for h in range(B_KV_HEADS)
                                for a in (v_t[:, h * B_HEAD_DIM:(h + 1) * B_HEAD_DIM], ones)], axis=1), pad_s)
    kwa = tiles(kvw_t[:, :B_KV_WIDTH].astype(_MXU), pad_w)
    vwa = tiles(kvw_t[:, B_KV_WIDTH:].astype(_MXU), pad_w)
    return kcp, vcp, tiles(ksa, pad_s), vs, kwa, vwa


def _bias_rows(dist, rbt):
    n = jnp.maximum(dist, 0)
    nf = jnp.maximum(n, 1).astype(_F32)
    large = REL_MAX_EXACT + (jnp.log(nf / REL_MAX_EXACT) / math.log(REL_MAX_DIST / REL_MAX_EXACT)
                             * (REL_BUCKETS - REL_MAX_EXACT)).astype(jnp.int32)
    bucket = jnp.where(n < REL_MAX_EXACT, n, jnp.minimum(large, REL_BUCKETS - 1))
    out = jnp.zeros((SUBLANES, dist.shape[-1]), _F32)
    for b in range(REL_BUCKETS):
        out = jnp.where(bucket == b, rbt[:, b:b + 1], out)
    return out


def _scmp_kernel(pt_ref, *refs, n_pages, group):
    pages = refs[:group]
    q_ref, pet_ref, wt_ref, rbt_ref, oc_ref, imp_ref, kvm_ref = refs[group:]
    pg = pl.program_id(1)
    n_cmp = n_pages * (PAGE_SIZE // CMP_BLOCK)
    per = PAGE_SIZE // CMP_BLOCK * group

    x = jnp.concatenate([p[...] for p in pages], axis=-1)
    kvm_ref[pg] = _split_dot(x, _pool_matrix(group * PAGE_SIZE, CMP_BLOCK, per))

    @pl.when(pg == pl.num_programs(1) - 1)
    def _():
        hd = B_HEAD_DIM
        n_sel = n_cmp // 2
        past = n_pages * PAGE_SIZE
        sums = jnp.concatenate([kvm_ref[s] for s in range(n_pages // group)], axis=-1)
        kv = _compress_cols((sums + _pe_sums_t(pet_ref)) * (1.0 / CMP_BLOCK), wt_ref)
        q8 = q_ref[0] * (hd ** -0.5)
        row = lax.broadcasted_iota(jnp.int32, (SUBLANES, 1), 0)
        head0 = row < B_GROUP
        lc = jnp.where(head0, _dot(q8, kv[0:hd]), _dot(q8, kv[hd:2 * hd]))
        blk = lax.broadcasted_iota(jnp.int32, (1, n_cmp), 1)
        dist = past - (blk * CMP_BLOCK + CMP_BLOCK - 1)
        p = _masked_softmax(jnp.where(dist >= 0, lc + _bias_rows(dist, rbt_ref[...]), NEG_INF))
        oc_ref[0] = jnp.where(head0, _dot_nt(p, kv[2 * hd:3 * hd]), _dot_nt(p, kv[3 * hd:]))
        pool = (lax.broadcasted_iota(jnp.int32, (n_cmp, n_sel), 0) // (SEL_BLOCK // CMP_BLOCK)
                == lax.broadcasted_iota(jnp.int32, (n_cmp, n_sel), 1))
        pp = _dot_exact(p, jnp.where(pool, 1.0, 0.0))
        imp0 = jnp.sum(pp[0:B_GROUP], axis=0, keepdims=True)
        imp1 = jnp.sum(pp[B_GROUP:2 * B_GROUP], axis=0, keepdims=True)
        imp_ref[0] = jnp.where(row == 0, imp0, jnp.where(row == 1, imp1, 0.0))


def _sample_cmp(page_table, cache_t, q8, pe_t, w_t, rbt, l, group=16):
    ns, n_pages = page_table.shape
    n_cmp = n_pages * (PAGE_SIZE // CMP_BLOCK)
    page_spec = lambda k: pl.BlockSpec((None, None, _KV_ROWS, PAGE_SIZE),
                                       lambda b, g, pt: (l, pt[b, g * group + k], 0, 0))
    return pl.pallas_call(
        functools.partial(_scmp_kernel, n_pages=n_pages, group=group),
        grid_spec=pltpu.PrefetchScalarGridSpec(
            num_scalar_prefetch=1,
            grid=(ns, n_pages // group),
            in_specs=[page_spec(k) for k in range(group)] + [
                pl.BlockSpec((1, SUBLANES, B_HEAD_DIM), lambda b, g, pt: (b, 0, 0)),
                pl.BlockSpec((None, 2, B_HEAD_DIM, CMP_BLOCK), lambda b, g, pt: (l, 0, 0, 0)),
                pl.BlockSpec((None, 2, B_HEAD_DIM, B_HEAD_DIM), lambda b, g, pt: (l, 0, 0, 0)),
                pl.BlockSpec((SUBLANES, REL_BUCKETS), lambda b, g, pt: (0, 0))],
            out_specs=[pl.BlockSpec((1, SUBLANES, B_HEAD_DIM), lambda b, g, pt: (b, 0, 0)),
                       pl.BlockSpec((1, SUBLANES, n_cmp // 2), lambda b, g, pt: (b, 0, 0))],
            scratch_shapes=[pltpu.VMEM((n_pages // group, _KV_ROWS, n_cmp * group // n_pages), _F32)]),
        out_shape=[jax.ShapeDtypeStruct((ns, SUBLANES, B_HEAD_DIM), _F32),
                   jax.ShapeDtypeStruct((ns, SUBLANES, n_cmp // 2), _F32)],
        compiler_params=_cparams(("arbitrary", "arbitrary")),
        name="nsa_sample_cmp",
    )(page_table, *([cache_t] * group), q8, pe_t, w_t, rbt)


def _stopk_kernel(imp_ref, idx_ref, *, k):
    s = imp_ref[...]
    n = s.shape[-1]
    lane = lax.broadcasted_iota(jnp.int32, s.shape, 1)
    s = jnp.where(lane == 0, FORCE_SCORE, s)

    def body(it, carry):
        s, out = carry
        m = jnp.max(s, axis=-1, keepdims=True)
        first = jnp.min(jnp.where(s == m, lane, n), axis=-1, keepdims=True)
        return jnp.where(lane == first, -jnp.inf, s), jnp.where(lane == it, first, out)

    _, out = lax.fori_loop(0, k, body, (s, jnp.zeros(s.shape, jnp.int32)))
    idx_ref[...] = out


def _sample_topk(imp2d, k):
    return pl.pallas_call(
        functools.partial(_stopk_kernel, k=k),
        out_shape=jax.ShapeDtypeStruct(imp2d.shape, jnp.int32),
        name="nsa_sample_topk",
    )(imp2d)


def _sattn_kernel(pt_ref, idx_ref, *refs, n_blk, past):
    blocks = refs[:2 * n_blk]
    q_ref, gt_ref, new_ref, oc_ref, win_ref, rbt_ref, o_ref = refs[2 * n_blk:]
    b = pl.program_id(0)
    hd = B_HEAD_DIM
    q8 = q_ref[0] * (hd ** -0.5)
    rbt = rbt_ref[...]
    row = lax.broadcasted_iota(jnp.int32, (SUBLANES, 1), 0)
    head0 = row < B_GROUP
    new = new_ref[0]
    bias0 = _bias_rows(jnp.zeros((1, 1), jnp.int32), rbt)

    def attend(s, v_of_head, k_new, v_new):
        s_new = jnp.sum(q8 * k_new, axis=-1, keepdims=True) + bias0
        m = jnp.maximum(jnp.max(s, axis=-1, keepdims=True), s_new)
        p = jnp.exp(s - m)
        p_new = jnp.exp(s_new - m)
        den = jnp.sum(p, axis=-1, keepdims=True) + p_new
        num = jnp.where(head0, _dot_nt(p, v_of_head(0)), _dot_nt(p, v_of_head(1))) + p_new * v_new
        return num / den

    def per_head(a0, a1):
        return jnp.where(head0, a0, a1)

    per_page = PAGE_SIZE // SEL_BLOCK
    t_in = lax.broadcasted_iota(jnp.int32, (1, PAGE_SIZE), 1)
    s_h, vs = [], []
    for h in range(B_KV_HEADS):
        kt = jnp.concatenate([blocks[h * n_blk + k][h * hd:(h + 1) * hd, :] for k in range(n_blk)], axis=-1)
        vs.append(jnp.concatenate(
            [blocks[h * n_blk + k][B_KV_WIDTH + h * hd:B_KV_WIDTH + (h + 1) * hd, :] for k in range(n_blk)], axis=-1))
        dist = []
        for k in range(n_blk):
            blk = idx_ref[b, h, k]
            in_blk = t_in // SEL_BLOCK == blk % per_page
            dist.append(jnp.where(in_blk, past - ((blk // per_page) * PAGE_SIZE + t_in), -1))
        dist = jnp.concatenate(dist, axis=-1)
        s_h.append(jnp.where(dist >= 0, _dot(q8, kt) + _bias_rows(dist, rbt), NEG_INF))
    ksn = new[:, 2 * B_KV_WIDTH:3 * B_KV_WIDTH]
    vsn = new[:, 3 * B_KV_WIDTH:4 * B_KV_WIDTH]
    o_s = attend(per_head(s_h[0], s_h[1]), lambda h: vs[h],
                 per_head(ksn[:, :hd], ksn[:, hd:]), per_head(vsn[:, :hd], vsn[:, hd:]))

    win = win_ref[0]
    wb = win.shape[1]
    dist = wb - lax.broadcasted_iota(jnp.int32, (1, wb), 1)
    okw = (dist < WINDOW) & (past - dist >= 0)
    lw = per_head(_dot(q8, win[0:hd]), _dot(q8, win[hd:2 * hd]))
    sw = jnp.where(okw, lw + _bias_rows(dist, rbt), NEG_INF)
    kwn = new[:, 4 * B_KV_WIDTH:5 * B_KV_WIDTH]
    vwn = new[:, 5 * B_KV_WIDTH:6 * B_KV_WIDTH]
    o_w = attend(sw, lambda h: win[B_KV_WIDTH + h * hd:B_KV_WIDTH + (h + 1) * hd],
                 per_head(kwn[:, :hd], kwn[:, hd:]), per_head(vwn[:, :hd], vwn[:, hd:]))

    gate = jax.nn.sigmoid(gt_ref[0])
    o_ref[0] = gate[:, 0:1] * oc_ref[0] + gate[:, 1:2] * o_s + gate[:, 2:3] * o_w


def _sample_attn(page_table, idx, cache_t, q8, gate8, kv_new, oc, win_t, rbt, l):
    ns, n_pages = page_table.shape
    n_blk = idx.shape[-1]
    past = n_pages * PAGE_SIZE
    per_page = PAGE_SIZE // SEL_BLOCK

    pages = jnp.take_along_axis(page_table[:, None, :], idx // per_page, axis=-1)

    def blk_spec(h, k):
        return pl.BlockSpec((None, None, _KV_ROWS, PAGE_SIZE), lambda b, pg, ix: (l, pg[b, h, k], 0, 0))

    wb = win_t.shape[3]
    return pl.pallas_call(
        functools.partial(_sattn_kernel, n_blk=n_blk, past=past),
        grid_spec=pltpu.PrefetchScalarGridSpec(
            num_scalar_prefetch=2,
            grid=(ns,),
            in_specs=[blk_spec(h, k) for h in range(B_KV_HEADS) for k in range(n_blk)] + [
                pl.BlockSpec((1, SUBLANES, B_HEAD_DIM), lambda b, pt, ix: (b, 0, 0)),
                pl.BlockSpec((1, SUBLANES, LANES), lambda b, pt, ix: (b, 0, 0)),
                pl.BlockSpec((1, 1, 6 * B_KV_WIDTH), lambda b, pt, ix: (b, 0, 0)),
                pl.BlockSpec((1, SUBLANES, B_HEAD_DIM), lambda b, pt, ix: (b, 0, 0)),
                pl.BlockSpec((None, 1, _KV_ROWS, wb), lambda b, pt, ix: (l, b, 0, 0)),
                pl.BlockSpec((SUBLANES, REL_BUCKETS), lambda b, pt, ix: (0, 0))],
            out_specs=pl.BlockSpec((1, SUBLANES, B_HEAD_DIM), lambda b, pt, ix: (b, 0, 0))),
        out_shape=jax.ShapeDtypeStruct((ns, SUBLANES, B_HEAD_DIM), _F32),
        compiler_params=_cparams(("arbitrary",)),
        name="nsa_sample_attn",
    )(pages, idx, *([cache_t] * (B_KV_HEADS * n_blk)), q8, gate8, kv_new, oc, win_t, rbt)


def _smix_kernel(uv_ref, ng_ref, wd_ref, b0_ref, xbc_ref, st_ref, cw_ref, cb_ref, dt_ref, hp_ref,
                 dsk_ref, ex_ref, oa_ref, v_ref, xdt_ref, ea_ref, y1_ref, bc_ref):
    u, v = _gelu_ln(uv_ref[...], ng_ref[...])
    v_ref[...] = v
    oa_ref[...] = u * (v * wd_ref[...] + b0_ref[...])
    conv = cb_ref[...] + xbc_ref[...] * cw_ref[C_CONV - 1:C_CONV, :]
    for k in range(C_CONV - 1):
        conv = conv + st_ref[k] * cw_ref[k:k + 1, :]
    xc = _silu(conv)
    xs = xc[:, :C_WIDTH]
    bm = xc[:, C_WIDTH:C_WIDTH + C_GROUPS * C_STATE]
    cm = xc[:, C_WIDTH + C_GROUPS * C_STATE:]
    bc_ref[...] = xc[:, C_WIDTH:]
    dt = jax.nn.softplus(dt_ref[...] + hp_ref[0:1, :])
    acum = dt * (-jnp.exp(hp_ref[1:2, :]))
    dt_rep = _dot_exact(dt, ex_ref[...])
    ea_ref[...] = jnp.exp(_dot_exact(acum, ex_ref[...]))
    xdt = dt_rep * xs
    xdt_ref[...] = xdt
    per_g = C_WIDTH // C_GROUPS
    cb = [jnp.sum(cm[:, g * C_STATE:(g + 1) * C_STATE] * bm[:, g * C_STATE:(g + 1) * C_STATE],
                  axis=-1, keepdims=True) for g in range(C_GROUPS)]
    lane = lax.broadcasted_iota(jnp.int32, xs.shape, 1)
    y1_ref[...] = jnp.where(lane < per_g, cb[0], cb[1]) * xdt + dsk_ref[...] * xs


def _sample_mix(uv, ng_row, wd_row, b0_row, xbc, st, conv_w_l, conv_b_row, dt, hp, dsk_row, expand):
    ns = uv.shape[0]
    f = lambda w: jax.ShapeDtypeStruct((ns, w), _F32)
    return pl.pallas_call(
        _smix_kernel,
        out_shape=[f(A_WIDTH), f(A_WIDTH), f(C_WIDTH), f(C_WIDTH), f(C_WIDTH), f(2 * C_GROUPS * C_STATE)],
        name="sample_gmlp_conv",
    )(uv, ng_row, wd_row, b0_row, xbc, st, conv_w_l, conv_b_row, dt, hp, dsk_row, expand)


def _sssm_kernel(h0_ref, xdt_ref, ea_ref, y1_ref, bc_ref, z_ref, ng_ref, o_ref, hout_ref):
    per_g = C_WIDTH // C_GROUPS
    h0 = h0_ref[0]
    bc = bc_ref[0]
    bm = bc[:, :C_GROUPS * C_STATE]
    cm = bc[:, C_GROUPS * C_STATE:]
    c8 = [jnp.broadcast_to(cm[:, g * C_STATE:(g + 1) * C_STATE], (SUBLANES, C_STATE)) for g in range(C_GROUPS)]
    ch = jnp.concatenate([_dot_nt(c8[g], h0[g * per_g:(g + 1) * per_g, :])[0:1] for g in range(C_GROUPS)], axis=-1)
    y = y1_ref[0] + ch * ea_ref[0]
    o_ref[0] = _rms(y * _silu(z_ref[0]), ng_ref[...])
    row = lax.broadcasted_iota(jnp.int32, (LANES, C_WIDTH), 0)
    cols = jnp.where(row == 0, xdt_ref[0], jnp.where(row == 1, ea_ref[0], 0.0)).T
    rsel = lax.broadcasted_iota(jnp.int32, (C_WIDTH, C_STATE), 0) < per_g
    b_full = jnp.where(rsel, bm[:, :C_STATE], bm[:, C_STATE:])
    hout_ref[0] = h0 * cols[:, 1:2] + cols[:, 0:1] * b_full


def _sample_ssm(h0, xdt, ea, y1, bc, z, ng_row, l):
    ns = xdt.shape[0]
    r3 = lambda a: a.reshape(ns, 1, a.shape[-1])
    row_spec = lambda w: pl.BlockSpec((1, 1, w), lambda b: (b, 0, 0))
    out, hout = pl.pallas_call(
        _sssm_kernel,
        grid=(ns,),
        in_specs=[pl.BlockSpec((None, 1, C_WIDTH, C_STATE), lambda b: (l, b, 0, 0)),
                  row_spec(C_WIDTH), row_spec(C_WIDTH), row_spec(C_WIDTH), row_spec(2 * C_GROUPS * C_STATE),
                  row_spec(C_WIDTH), pl.BlockSpec((1, C_WIDTH), lambda b: (0, 0))],
        out_specs=[row_spec(C_WIDTH), pl.BlockSpec((1, C_WIDTH, C_STATE), lambda b: (b, 0, 0))],
        out_shape=[jax.ShapeDtypeStruct((ns, 1, C_WIDTH), _F32),
                   jax.ShapeDtypeStruct((ns, C_WIDTH, C_STATE), _F32)],
        compiler_params=_cparams(("parallel",)),
        name="sample_ssm",
    )(h0, r3(xdt), r3(ea), r3(y1), r3(bc), r3(z), ng_row)
    return out.reshape(ns, C_WIDTH), hout


def _pad_cols(a, width):
    return jnp.pad(a, ((0, 0),) * (a.ndim - 1) + ((0, width - a.shape[-1]),))


def _pack_w_in(w_in):
    o = np.cumsum((0, A_WIDTH, A_WIDTH, B_WIDTH, 6 * B_KV_WIDTH, _GATE_COLS, C_WIDTH, C_CONV_DIM, C_HEADS))
    u0, q0, kv0, gate0, z0, xbc0, dt0, end = o[0], o[2], o[3], o[4], o[5], o[6], o[7], o[8]
    parts = [w_in[..., u0:kv0], w_in[..., z0:xbc0], w_in[..., xbc0:dt0],
             _pad_cols(w_in[..., gate0:z0], LANES), _pad_cols(w_in[..., dt0:end], LANES)]
    w_rows = jnp.concatenate(parts, axis=-1).astype(_MXU)
    w_kv_t = jnp.swapaxes(w_in[..., kv0:gate0], -1, -2).astype(_MXU)
    return w_rows, w_kv_t


def _head_rows(dt_bias, a_log):
    hp = jnp.zeros((DEPTH, SUBLANES, LANES), _F32)
    hp = hp.at[:, 0, :C_HEADS].set(dt_bias)
    return hp.at[:, 1, :C_HEADS].set(a_log)


def kernel(x_prompt, x_sample, cache_cmp_kv, cache_slc_kv, page_table, state_win_kv, state_conv, state_ssm,
           norm_g, ffn_w_gate, ffn_w_up, ffn_w_down, w_in, w_out, gmlp_norm_g, gmlp_w_s, gmlp_b_s,
           nsa_pe_cmp, nsa_w_cmp, rel_bias, conv_w, conv_b, dt_bias, a_log, d_skip, ssm_norm_g):
    bp, t = x_prompt.shape[:2]
    ns = x_sample.shape[0]
    n_pages = page_table.shape[1]
    n_phys = cache_cmp_kv.shape[1]
    tm_p, tm_s = 512, ns
    assert t % _KEY_CHUNK == 0 and t % tm_p == 0

    wg, wu, wd = (w.astype(_MXU) for w in (ffn_w_gate, ffn_w_up, ffn_w_down))
    w_in_p, w_kv_t = _pack_w_in(w_in)
    w_out_b = w_out.astype(_MXU)
    pe_t = jnp.swapaxes(nsa_pe_cmp, -1, -2)
    wc_t = jnp.swapaxes(nsa_w_cmp, -1, -2)
    bs_t = jnp.swapaxes(gmlp_b_s, 1, 2)
    wdiag = jnp.repeat(gmlp_w_s[:, :, 0, 0], A_HEAD_DIM, axis=-1)
    b0 = jnp.repeat(gmlp_b_s[:, :, 0], A_HEAD_DIM, axis=-1)
    hp = _head_rows(dt_bias, a_log)
    dsk = jnp.repeat(d_skip, C_HEAD_DIM, axis=-1)
    expand = (jnp.arange(LANES)[:, None] == jnp.arange(C_WIDTH)[None, :] // C_HEAD_DIM).astype(_F32)
    rb_flat = rel_bias.reshape(-1)
    rbt = _pad_cols(rel_bias, SUBLANES).T
    def chan_major(a):
        lead, tok = a.shape[:-4], a.shape[-4]
        perm = tuple(range(len(lead))) + tuple(len(lead) + k for k in (1, 2, 3, 0))
        return a.transpose(perm).reshape(*lead, _KV_ROWS, tok)

    def token_major(a_t):
        lead, tok = a_t.shape[:-2], a_t.shape[-1]
        a5 = a_t.reshape(*lead, 2, B_KV_HEADS, B_HEAD_DIM, tok)
        perm = tuple(range(len(lead))) + tuple(len(lead) + k for k in (3, 0, 1, 2))
        return a5.transpose(perm)

    cache_c = chan_major(cache_cmp_kv)
    cache_s = chan_major(cache_slc_kv)
    win_state = chan_major(state_win_kv)
    ssm_state = state_ssm.reshape(DEPTH, ns, C_WIDTH, C_STATE)

    xp = x_prompt.reshape(bp * t, D_MODEL)
    xs = x_sample.reshape(ns, D_MODEL)
    outs = [[] for _ in range(11)]
    for l in range(DEPTH):
        g = norm_g[l]
        row = lambda a: a.reshape(1, -1)
        xp = _ffn(xp, g[0:2], wg, wu, wd, l, 0, tm_p)
        xs = _ffn(xs, g[0:2], wg, wu, wd, l, 0, tm_s)

        uv, q, z, xbc, gate, dt, kvc_t, kvs_t, kvw_t = _inproj(xp.reshape(bp, t, D_MODEL), row(g[2]), w_in_p,
                                                               w_kv_t, l, tm_p)
        oa = _gmlp_prompt(uv.reshape(bp * t, 2 * A_WIDTH), row(gmlp_norm_g[l]), gmlp_w_s, bs_t, l)
        kvcmp_t = _compress_prompt(kvc_t, pe_t, wc_t, l)
        ob = _nsa_prompt(q, gate, *_nsa_prompt_inputs(kvcmp_t, kvs_t, kvw_t), rb_flat)
        oc, h_p = _mamba_prompt(z, xbc, dt, conv_w, row(conv_b[l]), hp[l], row(dsk[l]), row(ssm_norm_g[l]), l)
        xp = _outproj(xp, oa, ob.reshape(bp * t, B_WIDTH), oc.reshape(bp * t, C_WIDTH), row(g[3]), w_out_b, l, tm_p)
        wkeep = min(WINDOW, t)
        outs[0].append(token_major(kvc_t))
        outs[1].append(token_major(kvs_t))
        outs[2].append(token_major(kvw_t[:, :, t - wkeep:]))
        outs[3].append(xbc[:, t - (C_CONV - 1):])
        outs[4].append(h_p)

        uv, q, z, xbc, gate, dt, kvc_t, kvs_t, kvw_t = (
            a[0] for a in _inproj(xs.reshape(1, ns, D_MODEL), row(g[2]), w_in_p, w_kv_t, l, tm_s))
        q8 = jnp.pad(q.reshape(ns, B_HEADS, B_HEAD_DIM), ((0, 0), (0, SUBLANES - B_HEADS), (0, 0)))
        gate8 = jnp.pad(gate[:, :_GATE_COLS].reshape(ns, B_HEADS, N_BRANCH),
                        ((0, 0), (0, SUBLANES - B_HEADS), (0, LANES - N_BRANCH)))
        o_cmp, imp = _sample_cmp(page_table, cache_c, q8, pe_t, wc_t, rbt, l)
        n_sel_past = imp.shape[-1]
        idx = _sample_topk(imp.reshape(ns * SUBLANES, n_sel_past), SEL_TOPK - 1)
        idx = idx.reshape(ns, SUBLANES, n_sel_past)[:, :B_KV_HEADS, :SEL_TOPK - 1]
        kvc, kvs, kvwin = kvc_t.T, kvs_t.T, kvw_t.T
        kv_new = jnp.concatenate([kvc, kvs, kvwin], axis=-1).reshape(ns, 1, 6 * B_KV_WIDTH)
        ob8 = _sample_attn(page_table, idx, cache_s, q8, gate8, kv_new, o_cmp, win_state, rbt, l)
        ob = ob8[:, :B_HEADS].reshape(ns, B_WIDTH)
        st = jnp.swapaxes(state_conv[l], 0, 1)
        oa, v_rows, xdt, ea, y1, bc = _sample_mix(uv, row(gmlp_norm_g[l]), row(wdiag[l]), row(b0[l]), xbc, st,
                                                  conv_w[l], row(conv_b[l]), dt, hp[l], row(dsk[l]), expand)
        oc, h_s = _sample_ssm(ssm_state, xdt, ea, y1, bc, z, row(ssm_norm_g[l]), l)
        xs = _outproj(xs, oa, ob, oc, row(g[3]), w_out_b, l, tm_s)
        outs[5].append(kvc.reshape(ns, 1, 2, B_KV_HEADS, B_HEAD_DIM))
        outs[6].append(kvs.reshape(ns, 1, 2, B_KV_HEADS, B_HEAD_DIM))
        outs[7].append(token_major(jnp.concatenate([win_state[l][:, :, 1:], kvwin[:, :, None]], axis=-1)))
        outs[8].append(jnp.concatenate([state_conv[l][:, 1:], xbc[:, None]], axis=1))
        outs[9].append(h_s.reshape(ns, C_HEADS, C_HEAD_DIM, C_STATE))
        outs[10].append(v_rows.reshape(ns, 1, A_WIDTH))

        xp = _ffn(xp, g[4:6], wg, wu, wd, l, 1, tm_p)
        xs = _ffn(xs, g[4:6], wg, wu, wd, l, 1, tm_s)
    stacked = [jnp.stack(o) for o in outs]
    return (xp.reshape(bp, t, D_MODEL), xs.reshape(ns, 1, D_MODEL), *stacked)
```

```python
import functools
import math

import numpy as np
import jax
import jax.numpy as jnp
from jax import lax
from jax.experimental import pallas as pl
from jax.experimental.pallas import tpu as pltpu

D_MODEL = 1024
DEPTH = 4
PAGE_SIZE = 128
A_HEADS, A_HEAD_DIM, A_CHUNK = 4, 64, 128
A_WIDTH = A_HEADS * A_HEAD_DIM
B_HEADS, B_KV_HEADS, B_HEAD_DIM = 6, 2, 64
B_GROUP = B_HEADS // B_KV_HEADS
B_WIDTH = B_HEADS * B_HEAD_DIM
B_KV_WIDTH = B_KV_HEADS * B_HEAD_DIM
N_BRANCH = 3
CMP_BLOCK, SEL_BLOCK, SEL_TOPK, WINDOW, Q_BLOCK = 32, 64, 16, 512, 128
FORCE_SCORE = 1e4
C_HEADS, C_HEAD_DIM, C_GROUPS, C_STATE, C_CONV = 6, 64, 2, 64, 4
C_WIDTH = C_HEADS * C_HEAD_DIM
C_CONV_DIM = C_WIDTH + 2 * C_GROUPS * C_STATE
SSD_CHUNK = 128
D_FF = 2816
REL_BUCKETS, REL_MAX_EXACT, REL_MAX_DIST = 32, 16, 128
EPS = 1e-6
NEG_INF = -1e30
SOFTMAX_FLOOR = -1e20

LANES = 128
SUBLANES = 8
VMEM_LIMIT_BYTES = 56 * 1024 * 1024

_MXU = jnp.bfloat16
_F32 = jnp.float32

_GATE_COLS = N_BRANCH * B_HEADS
_SEG = (("uv", 2 * A_WIDTH), ("q", B_WIDTH), ("z", C_WIDTH), ("xbc", C_CONV_DIM), ("gate", LANES), ("dt", LANES))
_D_IN_PAD = sum(w for _, w in _SEG)
_KV_ROWS = 2 * B_KV_WIDTH
_N_KV = 3


def _dot(a, b):
    return jnp.dot(a.astype(_MXU), b.astype(_MXU), preferred_element_type=_F32)


def _dot_nt(a, b):
    return lax.dot_general(a.astype(_MXU), b.astype(_MXU), (((1,), (1,)), ((), ())),
                           preferred_element_type=_F32)


def _dot_tn(a, b):
    return lax.dot_general(a.astype(_MXU), b.astype(_MXU), (((0,), (0,)), ((), ())),
                           preferred_element_type=_F32)


def _dot_exact(a, b):
    return jnp.dot(a, b, preferred_element_type=_F32, precision=lax.Precision.HIGHEST)


def _rms(x, g):
    return x * lax.rsqrt(jnp.mean(x * x, axis=-1, keepdims=True) + EPS) * g


def _silu(x):
    return x * jax.nn.sigmoid(x)


def _cparams(sem):
    return pltpu.CompilerParams(dimension_semantics=sem, vmem_limit_bytes=VMEM_LIMIT_BYTES)


def _bucket_np(dist):
    n = np.maximum(dist, 0)
    nf = np.maximum(n, 1).astype(np.float32)
    large = REL_MAX_EXACT + (np.log(nf / np.float32(REL_MAX_EXACT))
                             / np.float32(math.log(REL_MAX_DIST / REL_MAX_EXACT))
                             * np.float32(REL_BUCKETS - REL_MAX_EXACT)).astype(np.int32)
    large = np.minimum(large, REL_BUCKETS - 1)
    return np.where(n < REL_MAX_EXACT, n, large).astype(np.int32)


def _ffn_kernel(x_ref, g_ref, wg_ref, wu_ref, wd_ref, o_ref, *, tf):
    x = x_ref[...]
    h = _rms(x, g_ref[0:1, :]).astype(_MXU)
    y = jnp.zeros(x.shape, _F32)
    for f in range(D_FF // tf):
        cols = slice(f * tf, (f + 1) * tf)
        a = _silu(_dot(h, wg_ref[:, cols])) * _dot(h, wu_ref[:, cols])
        y = y + _dot(a, wd_ref[cols, :])
    o_ref[...] = x + 0.5 * _rms(y, g_ref[1:2, :])


def _ffn(x, g2, wg, wu, wd, l, j, tm, tf=256):
    rows = x.shape[0]
    once = pl.Buffered(1)
    return pl.pallas_call(
        functools.partial(_ffn_kernel, tf=tf),
        grid=(rows // tm,),
        in_specs=[pl.BlockSpec((tm, D_MODEL), lambda r: (r, 0)),
                  pl.BlockSpec((2, D_MODEL), lambda r: (0, 0)),
                  pl.BlockSpec((None, None, D_MODEL, D_FF), lambda r: (l, j, 0, 0), pipeline_mode=once),
                  pl.BlockSpec((None, None, D_MODEL, D_FF), lambda r: (l, j, 0, 0), pipeline_mode=once),
                  pl.BlockSpec((None, None, D_FF, D_MODEL), lambda r: (l, j, 0, 0), pipeline_mode=once)],
        out_specs=pl.BlockSpec((tm, D_MODEL), lambda r: (r, 0)),
        out_shape=jax.ShapeDtypeStruct((rows, D_MODEL), _F32),
        compiler_params=_cparams(("parallel",)),
        name="half_ffn",
    )(x, g2, wg, wu, wd)


def _inproj_kernel(x_ref, g_ref, w_ref, wkv_ref, *o_refs):
    h = _rms(x_ref[0], g_ref[...]).astype(_MXU)
    off = 0
    for (_, width), o_ref in zip(_SEG, o_refs):
        o_ref[0] = _dot(h, w_ref[:, off:off + width])
        off += width
    for k, o_ref in enumerate(o_refs[len(_SEG):]):
        o_ref[0] = _dot_nt(wkv_ref[k * _KV_ROWS:(k + 1) * _KV_ROWS, :], h)


def _inproj(x, g_row, w_in_p, w_kv_t, l, tm):
    nb, t = x.shape[:2]
    return pl.pallas_call(
        _inproj_kernel,
        grid=(nb, t // tm),
        in_specs=[pl.BlockSpec((1, tm, D_MODEL), lambda b, r: (b, r, 0)),
                  pl.BlockSpec((1, D_MODEL), lambda b, r: (0, 0)),
                  pl.BlockSpec((None, D_MODEL, _D_IN_PAD), lambda b, r: (l, 0, 0)),
                  pl.BlockSpec((None, _N_KV * _KV_ROWS, D_MODEL), lambda b, r: (l, 0, 0))],
        out_specs=[pl.BlockSpec((1, tm, w), lambda b, r: (b, r, 0)) for _, w in _SEG]
        + [pl.BlockSpec((1, _KV_ROWS, tm), lambda b, r: (b, 0, r))] * _N_KV,
        out_shape=[jax.ShapeDtypeStruct((nb, t, w), _F32) for _, w in _SEG]
        + [jax.ShapeDtypeStruct((nb, _KV_ROWS, t), _F32)] * _N_KV,
        compiler_params=_cparams(("parallel", "parallel")),
        name="in_proj",
    )(x, g_row, w_in_p, w_kv_t)


def _outproj_kernel(x_ref, oa_ref, ob_ref, oc_ref, g_ref, w_ref, o_ref):
    y = (_dot(oa_ref[...], w_ref[0:A_WIDTH, :])
         + _dot(ob_ref[...], w_ref[A_WIDTH:A_WIDTH + B_WIDTH, :])
         + _dot(oc_ref[...], w_ref[A_WIDTH + B_WIDTH:, :]))
    o_ref[...] = x_ref[...] + _rms(y, g_ref[...])


def _outproj(x, oa, ob, oc, g_row, w_out, l, tm):
    rows = x.shape[0]
    return pl.pallas_call(
        _outproj_kernel,
        grid=(rows // tm,),
        in_specs=[pl.BlockSpec((tm, D_MODEL), lambda r: (r, 0)),
                  pl.BlockSpec((tm, A_WIDTH), lambda r: (r, 0)),
                  pl.BlockSpec((tm, B_WIDTH), lambda r: (r, 0)),
                  pl.BlockSpec((tm, C_WIDTH), lambda r: (r, 0)),
                  pl.BlockSpec((1, D_MODEL), lambda r: (0, 0)),
                  pl.BlockSpec((None, D_MODEL, D_MODEL), lambda r: (l, 0, 0))],
        out_specs=pl.BlockSpec((tm, D_MODEL), lambda r: (r, 0)),
        out_shape=jax.ShapeDtypeStruct((rows, D_MODEL), _F32),
        compiler_params=_cparams(("parallel",)),
        name="out_proj",
    )(x, oa, ob, oc, g_row, w_out)


def _gelu_ln(uv, ng):
    u = jax.nn.gelu(uv[:, :A_WIDTH])
    v = jax.nn.gelu(uv[:, A_WIDTH:])
    mu = jnp.mean(v, axis=-1, keepdims=True)
    var = jnp.mean(jnp.square(v - mu), axis=-1, keepdims=True)
    return u, (v - mu) * lax.rsqrt(var + EPS) * ng


def _gmlp_kernel(uv_ref, ng_ref, ws_ref, bs_ref, o_ref, *, chunks):
    row = lax.broadcasted_iota(jnp.int32, (A_CHUNK, A_CHUNK), 0)
    col = lax.broadcasted_iota(jnp.int32, (A_CHUNK, A_CHUNK), 1)
    ws = [jnp.where(col <= row, ws_ref[h], 0.0).astype(_MXU) for h in range(A_HEADS)]
    for c in range(chunks):
        u, v = _gelu_ln(uv_ref[c * A_CHUNK:(c + 1) * A_CHUNK, :], ng_ref[...])
        sg = [_dot(ws[h], v[:, h * A_HEAD_DIM:(h + 1) * A_HEAD_DIM]) + bs_ref[:, h:h + 1]
              for h in range(A_HEADS)]
        o_ref[c * A_CHUNK:(c + 1) * A_CHUNK, :] = u * jnp.concatenate(sg, axis=-1)


def _gmlp_prompt(uv, ng_row, ws, bs_t, l, chunks=4):
    rows = uv.shape[0]
    tm = chunks * A_CHUNK
    return pl.pallas_call(
        functools.partial(_gmlp_kernel, chunks=chunks),
        grid=(rows // tm,),
        in_specs=[pl.BlockSpec((tm, 2 * A_WIDTH), lambda r: (r, 0)),
                  pl.BlockSpec((1, A_WIDTH), lambda r: (0, 0)),
                  pl.BlockSpec((None, A_HEADS, A_CHUNK, A_CHUNK), lambda r: (l, 0, 0, 0)),
                  pl.BlockSpec((None, A_CHUNK, A_HEADS), lambda r: (l, 0, 0))],
        out_specs=pl.BlockSpec((tm, A_WIDTH), lambda r: (r, 0)),
        out_shape=jax.ShapeDtypeStruct((rows, A_WIDTH), _F32),
        compiler_params=_cparams(("parallel",)),
        name="gmlp_prompt",
    )(uv, ng_row, ws, bs_t)


def _mamba_kernel(z_ref, xbc_ref, dt_ref, cw_ref, cb_ref, hp_ref, dsk_ref, ng_ref,
                  o_ref, hout_ref, xp_ref, hs_ref):
    t = pl.program_id(0)
    L = SSD_CHUNK
    hist = SUBLANES

    @pl.when(t == 0)
    def _():
        xp_ref[:, 0:hist, :] = jnp.zeros((xp_ref.shape[0], hist, C_CONV_DIM), _F32)
        hs_ref[...] = jnp.zeros_like(hs_ref)

    a_row = -jnp.exp(hp_ref[1:2, :])
    row = lax.broadcasted_iota(jnp.int32, (L, L), 0)
    col = lax.broadcasted_iota(jnp.int32, (L, L), 1)
    causal = col <= row
    tril = jnp.where(causal, 1.0, 0.0)
    for b in range(z_ref.shape[0]):
        xp_ref[b, hist:hist + L, :] = xbc_ref[b]
        conv = cb_ref[...]
        for k in range(C_CONV):
            conv = conv + xp_ref[b, pl.ds(hist - (C_CONV - 1) + k, L), :] * cw_ref[k:k + 1, :]
        xp_ref[b, 0:hist, :] = xp_ref[b, L:L + hist, :]
        xc = _silu(conv)
        xs = xc[:, :C_WIDTH]
        bm = xc[:, C_WIDTH:C_WIDTH + C_GROUPS * C_STATE]
        cm = xc[:, C_WIDTH + C_GROUPS * C_STATE:]

        dt = jax.nn.softplus(dt_ref[b] + hp_ref[0:1, :])
        acum = _dot_exact(tril, dt * a_row)
        acum_t = acum.T
        dt_t = dt.T
        cb = [_dot_nt(cm[:, g * C_STATE:(g + 1) * C_STATE], bm[:, g * C_STATE:(g + 1) * C_STATE])
              for g in range(C_GROUPS)]
        ys = []
        for h in range(C_HEADS):
            g = h // (C_HEADS // C_GROUPS)
            ac_col = acum[:, h:h + 1]
            seg = ac_col - acum_t[h:h + 1, :]
            decay = jnp.where(causal, jnp.exp(jnp.where(causal, seg, 0.0)), 0.0)
            scores = cb[g] * decay * dt_t[h:h + 1, :]
            x_h = xs[:, h * C_HEAD_DIM:(h + 1) * C_HEAD_DIM]
            b_g = bm[:, g * C_STATE:(g + 1) * C_STATE]
            c_g = cm[:, g * C_STATE:(g + 1) * C_STATE]
            hs = hs_ref[b, h]
            ys.append(_dot(scores, x_h) + _dot_nt(c_g, hs) * jnp.exp(ac_col))
            ac_last = acum[L - 1:L, h:h + 1]
            w_end = jnp.exp(ac_last - ac_col) * dt[:, h:h + 1]
            hs_ref[b, h] = hs * jnp.exp(ac_last) + _dot_tn(x_h * w_end, b_g)
        y = jnp.concatenate(ys, axis=-1) + dsk_ref[...] * xs
        o_ref[b] = _rms(y * _silu(z_ref[b]), ng_ref[...])

    @pl.when(t == pl.num_programs(0) - 1)
    def _():
        hout_ref[...] = hs_ref[...]


def _mamba_prompt(z, xbc, dt, conv_w, conv_b_row, hp, dsk_row, ng_row, l):
    nb, t = z.shape[:2]
    L = SSD_CHUNK
    return pl.pallas_call(
        _mamba_kernel,
        grid=(t // L,),
        in_specs=[pl.BlockSpec((nb, L, C_WIDTH), lambda c: (0, c, 0)),
                  pl.BlockSpec((nb, L, C_CONV_DIM), lambda c: (0, c, 0)),
                  pl.BlockSpec((nb, L, LANES), lambda c: (0, c, 0)),
                  pl.BlockSpec((None, C_CONV, C_CONV_DIM), lambda c: (l, 0, 0)),
                  pl.BlockSpec((1, C_CONV_DIM), lambda c: (0, 0)),
                  pl.BlockSpec((SUBLANES, LANES), lambda c: (0, 0)),
                  pl.BlockSpec((1, C_WIDTH), lambda c: (0, 0)),
                  pl.BlockSpec((1, C_WIDTH), lambda c: (0, 0))],
        out_specs=[pl.BlockSpec((nb, L, C_WIDTH), lambda c: (0, c, 0)),
                   pl.BlockSpec((nb, C_HEADS, C_HEAD_DIM, C_STATE), lambda c: (0, 0, 0, 0))],
        out_shape=[jax.ShapeDtypeStruct((nb, t, C_WIDTH), _F32),
                   jax.ShapeDtypeStruct((nb, C_HEADS, C_HEAD_DIM, C_STATE), _F32)],
        scratch_shapes=[pltpu.VMEM((nb, L + 2 * SUBLANES, C_CONV_DIM), _F32),
                        pltpu.VMEM((nb, C_HEADS, C_HEAD_DIM, C_STATE), _F32)],
        compiler_params=_cparams(("arbitrary",)),
        name="mamba_prompt",
    )(z, xbc, dt, conv_w, conv_b_row, hp, dsk_row, ng_row)


def _split_dot(x, p):
    hi = x.astype(_MXU)
    lo = (x - hi.astype(_F32)).astype(_MXU)
    return jnp.dot(hi, p, preferred_element_type=_F32) + jnp.dot(lo, p, preferred_element_type=_F32)


def _pool_matrix(n_tok, per, n_out, first=0):
    r = lax.broadcasted_iota(jnp.int32, (n_tok, n_out), 0)
    c = lax.broadcasted_iota(jnp.int32, (n_tok, n_out), 1)
    return jnp.where(c == first + r // per, 1.0, 0.0).astype(_MXU)


def _pe_sums_t(pet_ref):
    pk = jnp.sum(pet_ref[0], axis=-1, keepdims=True)
    pv = jnp.sum(pet_ref[1], axis=-1, keepdims=True)
    return jnp.concatenate([pk, pk, pv, pv], axis=0)


def _compress_cols(m, wt_ref):
    hd = B_HEAD_DIM
    return jnp.concatenate([_dot(wt_ref[0], m[0:hd]), _dot(wt_ref[0], m[hd:2 * hd]),
                            _dot(wt_ref[1], m[2 * hd:3 * hd]), _dot(wt_ref[1], m[3 * hd:])], axis=0)


def _compress_kernel(kv_ref, pet_ref, wt_ref, o_ref, *, chunk):
    t = kv_ref.shape[-1]
    pool = _pool_matrix(chunk, CMP_BLOCK, chunk // CMP_BLOCK)
    sums = jnp.concatenate([_split_dot(kv_ref[0, :, c * chunk:(c + 1) * chunk], pool) for c in range(t // chunk)],
                           axis=-1)
    o_ref[0] = _compress_cols((sums + _pe_sums_t(pet_ref)) * (1.0 / CMP_BLOCK), wt_ref)


def _compress_prompt(kvc_t, pe_t, w_t, l, chunk=2048):
    nb, _, t = kvc_t.shape
    return pl.pallas_call(
        functools.partial(_compress_kernel, chunk=min(chunk, t)),
        grid=(nb,),
        in_specs=[pl.BlockSpec((1, _KV_ROWS, t), lambda b: (b, 0, 0)),
                  pl.BlockSpec((None, 2, B_HEAD_DIM, CMP_BLOCK), lambda b: (l, 0, 0, 0)),
                  pl.BlockSpec((None, 2, B_HEAD_DIM, B_HEAD_DIM), lambda b: (l, 0, 0, 0))],
        out_specs=pl.BlockSpec((1, _KV_ROWS, t // CMP_BLOCK), lambda b: (b, 0, 0)),
        out_shape=jax.ShapeDtypeStruct((nb, _KV_ROWS, t // CMP_BLOCK), _F32),
        compiler_params=_cparams(("parallel",)),
        name="nsa_compress_prompt",
    )(kvc_t, pe_t, w_t)


_KEY_CHUNK = 1024
_FRONT_PAD = Q_BLOCK
_NEAR = 2 * Q_BLOCK
_MAIN_UNROLL = 1
_Q_PER_STEP = 1
_V_ROWS = B_HEAD_DIM + 16


def _nsa_tables(n_sel):
    r = np.arange(Q_BLOCK)[:, None]
    cmp_idx = np.full((2, Q_BLOCK, n_sel), REL_BUCKETS - 1, np.int32)
    for par in range(2):
        for u in (-2, -1, 0, 1):
            dist = r[:, 0] - (CMP_BLOCK - 1) - CMP_BLOCK * (2 * u + par)
            cmp_idx[par, :, u % n_sel] = _bucket_np(dist)
    c = np.arange(_NEAR)[None, :]
    dist = Q_BLOCK + r - c
    near_idx = np.where(dist >= 0, _bucket_np(dist), -1).astype(np.int32)
    c = np.arange(WINDOW + Q_BLOCK)[None, :]
    dist = r + WINDOW - c
    win_idx = np.where((dist >= 0) & (dist < WINDOW), _bucket_np(dist), -1).astype(np.int32)
    return cmp_idx, near_idx, win_idx


def _fill_bias(idx, rb_ref, hg, rel_to_last):
    base = rb_ref[(REL_BUCKETS - 1) * B_HEADS + hg] if rel_to_last else 0.0
    tile = jnp.where(idx < 0, NEG_INF, 0.0)
    for b in range(REL_BUCKETS):
        tile = jnp.where(idx == b, rb_ref[b * B_HEADS + hg] - base, tile)
    return tile


def _topk_mask(score, k, taken):
    n = float(score.shape[-1])
    lane = lax.broadcasted_iota(jnp.int32, score.shape, 1).astype(_F32)
    s = jnp.where(taken, -jnp.inf, score)
    for _ in range(k):
        m = jnp.max(s, axis=-1, keepdims=True)
        first = jnp.min(jnp.where(s == m, lane, n), axis=-1, keepdims=True)
        s = jnp.where(lane == first, -jnp.inf, s)
    return s == -jnp.inf


def _masked_softmax(l):
    m = jnp.maximum(jnp.max(l, axis=-1, keepdims=True), SOFTMAX_FLOOR)
    e = jnp.exp(l - m)
    return e / jnp.maximum(jnp.sum(e, axis=-1, keepdims=True), 1e-20)


def _online_step(s, vt, carry):
    m_run, acc = carry
    m_new = jnp.maximum(m_run, jnp.max(s, axis=-1, keepdims=True))
    return m_new, jnp.exp(m_run - m_new) * acc + _dot_nt(jnp.exp(s - m_new), vt)


def _key_tiles(ref, first, n, rows=slice(None)):
    return jnp.concatenate([ref[0, first + j, rows, :] for j in range(n)], axis=-1)


def _nsa_kernel(q_ref, gt_ref, kc_ref, vc_ref, ksa_ref, vs_ref, kwa_ref, vwa_ref,
                cidx_ref, nidx_ref, widx_ref, rb_ref, o_ref, bc_ref, bn_ref, bw_ref, *, n_sel):
    step = pl.program_id(1)
    hd, G = B_HEAD_DIM, B_GROUP
    QB = Q_BLOCK

    @pl.when((pl.program_id(0) == 0) & (step == 0))
    def _():
        for hg in range(B_HEADS):
            for par in range(2):
                bc_ref[hg, par] = _fill_bias(cidx_ref[par], rb_ref, hg, False)
            bn_ref[hg] = _fill_bias(nidx_ref[...], rb_ref, hg, True)
            bw_ref[hg] = _fill_bias(widx_ref[...], rb_ref, hg, False)

    heads = range(B_KV_HEADS)
    hrows = [slice(h * hd, (h + 1) * hd) for h in heads]
    vrows = [slice(h * _V_ROWS, (h + 1) * _V_ROWS) for h in heads]
    r_col = lax.broadcasted_iota(jnp.int32, (QB, 1), 0)
    n_lane = lax.broadcasted_iota(jnp.int32, (QB, n_sel), 1)
    c_near = lax.broadcasted_iota(jnp.int32, (QB, _NEAR), 1)
    c_win = lax.broadcasted_iota(jnp.int32, (QB, WINDOW + QB), 1)
    zeros_h = jnp.zeros((G * QB, hd), _F32)
    n_win = (WINDOW + QB) // LANES
    per_chunk = _KEY_CHUNK // LANES

    def front(i, q, gate_logits):
        q = q * (hd ** -0.5)
        qpos = i * QB + r_col
        cur = qpos // SEL_BLOCK
        vis = [CMP_BLOCK * (2 * n_lane + par) + (CMP_BLOCK - 1) <= qpos for par in range(2)]
        vis3 = jnp.concatenate([jnp.concatenate(vis, axis=-1)] * G, axis=0)
        win_ok = jnp.concatenate([c_win >= WINDOW - i * QB] * G, axis=0)
        q3 = [jnp.concatenate([q[:, (h * G + g) * hd:(h * G + g + 1) * hd] for g in range(G)], axis=0)
              for h in heads]
        q3h = [jnp.concatenate([q3[h], zeros_h] if h == 0 else [zeros_h, q3[h]], axis=-1) for h in heads]
        t0 = i * (QB // LANES)

        forced = (n_lane == cur) | (n_lane == 0)
        started = n_lane <= cur
        o_c, imp = [], []
        for h in heads:
            lc = _dot(q3[h], kc_ref[0, hrows[h], :])
            bias_c = jnp.concatenate(
                [jnp.concatenate([pltpu.roll(bc_ref[h * G + g, par], 2 * i, 1) for par in range(2)], axis=-1)
                 for g in range(G)], axis=0)
            p_c = _masked_softmax(jnp.where(vis3, lc + bias_c, NEG_INF))
            o_c.append(_dot_nt(p_c, vc_ref[0, hrows[h], :]))
            imp_h = sum(p_c[g * QB:(g + 1) * QB, :n_sel] + p_c[g * QB:(g + 1) * QB, n_sel:] for g in range(G))
            imp.append(jnp.where(started, imp_h, -1.0))

        chosen = _topk_mask(jnp.concatenate(imp, axis=0), min(SEL_TOPK, n_sel) - 2,
                            jnp.concatenate([forced] * B_KV_HEADS, axis=0))
        kw = _key_tiles(kwa_ref, t0, n_win)
        o_w = []
        for h in heads:
            bias_w = jnp.concatenate([bw_ref[h * G + g] for g in range(G)], axis=0)
            p_w = _masked_softmax(jnp.where(win_ok, _dot(q3h[h], kw) + bias_w, NEG_INF))
            o_w.append(_dot_nt(p_w, _key_tiles(vwa_ref, t0, n_win, hrows[h])))

        qa_main, qa_near = [], []
        for h in heads:
            allowed = chosen[h * QB:(h + 1) * QB] & started
            m_main = jnp.where(allowed & (n_lane < 2 * i - 2), 0.0, NEG_INF)
            m_near = jnp.where(allowed, 0.0, NEG_INF)
            qa_main.append(jnp.concatenate([q3h[h], jnp.concatenate([m_main] * G, axis=0)], axis=-1).astype(_MXU))
            qa_near.append(jnp.concatenate([q3h[h], jnp.concatenate([m_near] * G, axis=0)], axis=-1).astype(_MXU))
        return dict(i=i, gate=jax.nn.sigmoid(gate_logits), o_c=o_c, o_w=o_w, qa_main=qa_main, qa_near=qa_near)

    def back(f, carry):
        i = f["i"]
        t0 = i * (QB // LANES)
        near_ok = jnp.concatenate([c_near >= _FRONT_PAD - i * QB] * G, axis=0)
        ks = _key_tiles(ksa_ref, t0, _NEAR // LANES)
        outs = []
        for h in heads:
            corr = jnp.concatenate([bn_ref[h * G + g] for g in range(G)], axis=0)
            s = jnp.where(near_ok, _dot(f["qa_near"][h], ks) + corr, NEG_INF)
            _, acc_s = _online_step(s, _key_tiles(vs_ref, t0, _NEAR // LANES, vrows[h]), carry[h])
            o_s = acc_s[:, :hd] / acc_s[:, hd:hd + 1]
            for g in range(G):
                k0 = (h * G + g) * N_BRANCH
                rows = slice(g * QB, (g + 1) * QB)
                gate = f["gate"]
                outs.append(gate[:, k0:k0 + 1] * f["o_c"][h][rows] + gate[:, k0 + 1:k0 + 2] * o_s[rows]
                            + gate[:, k0 + 2:k0 + 3] * f["o_w"][h][rows])
        return jnp.concatenate(outs, axis=-1)

    blocks = [front(step * _Q_PER_STEP + u, q_ref[0, u * QB:(u + 1) * QB, :], gt_ref[0, u * QB:(u + 1) * QB, :])
              for u in range(_Q_PER_STEP)]
    n_chain = _Q_PER_STEP * B_KV_HEADS
    qa_main_all = jnp.concatenate([f["qa_main"][h] for f in blocks for h in heads], axis=0)
    i_last = step * _Q_PER_STEP + _Q_PER_STEP - 1
    n_main = (jnp.maximum(i_last - 1, 0) * QB + _KEY_CHUNK - 1) // _KEY_CHUNK

    def main_body(c, carry):
        for u in range(_MAIN_UNROLL):
            tc = _FRONT_PAD // LANES + (c * _MAIN_UNROLL + u) * per_chunk
            s = _dot(qa_main_all, _key_tiles(ksa_ref, tc, per_chunk))
            carry = tuple(_online_step(s[k * G * QB:(k + 1) * G * QB],
                                       _key_tiles(vs_ref, tc, per_chunk, vrows[k % B_KV_HEADS]), carry[k])
                          for k in range(n_chain))
        return carry

    init = (jnp.full((G * QB, 1), SOFTMAX_FLOOR, _F32), jnp.zeros((G * QB, _V_ROWS), _F32))
    carry = lax.fori_loop(0, (n_main + _MAIN_UNROLL - 1) // _MAIN_UNROLL, main_body, (init,) * n_chain)
    for u, f in enumerate(blocks):
        o_ref[0, u * QB:(u + 1) * QB, :] = back(f, carry[u * B_KV_HEADS:(u + 1) * B_KV_HEADS])


def _nsa_prompt(q, gate, kcp, vcp, ksa, vs, kwa, vwa, rb_flat):
    nb, t = q.shape[:2]
    n_sel = t // SEL_BLOCK
    cidx, nidx, widx = _nsa_tables(n_sel)
    tp, tw = ksa.shape[1], kwa.shape[1]
    q_rows = _Q_PER_STEP * Q_BLOCK
    assert t % q_rows == 0
    full = lambda shape: pl.BlockSpec(shape, lambda b, i: (0,) * len(shape))
    return pl.pallas_call(
        functools.partial(_nsa_kernel, n_sel=n_sel),
        grid=(nb, t // q_rows),
        in_specs=[pl.BlockSpec((1, q_rows, B_WIDTH), lambda b, i: (b, i, 0)),
                  pl.BlockSpec((1, q_rows, LANES), lambda b, i: (b, i, 0)),
                  pl.BlockSpec((1, B_KV_WIDTH, 2 * n_sel), lambda b, i: (b, 0, 0)),
                  pl.BlockSpec((1, B_KV_WIDTH, 2 * n_sel), lambda b, i: (b, 0, 0)),
                  pl.BlockSpec((1, tp, B_KV_WIDTH + n_sel, LANES), lambda b, i: (b, 0, 0, 0)),
                  pl.BlockSpec((1, tp, B_KV_HEADS * _V_ROWS, LANES), lambda b, i: (b, 0, 0, 0)),
                  pl.BlockSpec((1, tw, B_KV_WIDTH, LANES), lambda b, i: (b, 0, 0, 0)),
                  pl.BlockSpec((1, tw, B_KV_WIDTH, LANES), lambda b, i: (b, 0, 0, 0)),
                  full(cidx.shape), full(nidx.shape), full(widx.shape),
                  pl.BlockSpec(memory_space=pltpu.SMEM)],
        out_specs=pl.BlockSpec((1, q_rows, B_WIDTH), lambda b, i: (b, i, 0)),
        out_shape=jax.ShapeDtypeStruct((nb, t, B_WIDTH), _F32),
        scratch_shapes=[pltpu.VMEM((B_HEADS, 2, Q_BLOCK, n_sel), _F32),
                        pltpu.VMEM((B_HEADS, Q_BLOCK, _NEAR), _F32),
                        pltpu.VMEM((B_HEADS, Q_BLOCK, WINDOW + Q_BLOCK), _F32)],
        compiler_params=_cparams(("arbitrary", "arbitrary")),
        name="nsa_prompt",
    )(q, gate, kcp, vcp, ksa, vs, kwa, vwa, jnp.asarray(cidx), jnp.asarray(nidx), jnp.asarray(widx), rb_flat)


def _nsa_prompt_inputs(kvcmp_t, kvs_t, kvw_t):
    nb, _, t = kvs_t.shape
    n_sel = t // SEL_BLOCK
    kvp = kvcmp_t.reshape(nb, _KV_ROWS, n_sel, 2).transpose(0, 1, 3, 2).reshape(nb, _KV_ROWS, 2 * n_sel)
    kcp = kvp[:, :B_KV_WIDTH].astype(_MXU)
    vcp = kvp[:, B_KV_WIDTH:].astype(_MXU)
    blk = (jnp.arange(n_sel)[:, None] == jnp.arange(t)[None, :] // SEL_BLOCK).astype(_MXU)
    ksa = jnp.concatenate([kvs_t[:, :B_KV_WIDTH].astype(_MXU), jnp.broadcast_to(blk, (nb, n_sel, t))], axis=1)
    pad_s = ((0, 0), (0, 0), (_FRONT_PAD, _MAIN_UNROLL * _KEY_CHUNK - _FRONT_PAD))
    pad_w = ((0, 0), (0, 0), (WINDOW, 0))
    def tiles(a, pad):
        a = jnp.pad(a, pad)
        return a.reshape(nb, a.shape[1], a.shape[2] // LANES, LANES).transpose(0, 2, 1, 3)

    ones = jnp.pad(jnp.ones((nb, 1, t), _MXU), ((0, 0), (0, _V_ROWS - B_HEAD_DIM - 1), (0, 0)))
    v_t = kvs_t[:, B_KV_WIDTH:].astype(_MXU)
    vs = tiles(jnp.concatenate([a for h in range(B_KV_HEADS)
                                for a in (v_t[:, h * B_HEAD_DIM:(h + 1) * B_HEAD_DIM], ones)], axis=1), pad_s)
    kwa = tiles(kvw_t[:, :B_KV_WIDTH].astype(_MXU), pad_w)
    vwa = tiles(kvw_t[:, B_KV_WIDTH:].astype(_MXU), pad_w)
    return kcp, vcp, tiles(ksa, pad_s), vs, kwa, vwa


def _bias_rows(dist, rbt):
    n = jnp.maximum(dist, 0)
    nf = jnp.maximum(n, 1).astype(_F32)
    large = REL_MAX_EXACT + (jnp.log(nf / REL_MAX_EXACT) / math.log(REL_MAX_DIST / REL_MAX_EXACT)
                             * (REL_BUCKETS - REL_MAX_EXACT)).astype(jnp.int32)
    bucket = jnp.where(n < REL_MAX_EXACT, n, jnp.minimum(large, REL_BUCKETS - 1))
    out = jnp.zeros((SUBLANES, dist.shape[-1]), _F32)
    for b in range(REL_BUCKETS):
        out = jnp.where(bucket == b, rbt[:, b:b + 1], out)
    return out


def _scmp_kernel(pt_ref, *refs, n_pages, group):
    pages = refs[:group]
    q_ref, pet_ref, wt_ref, rbt_ref, oc_ref, imp_ref, kvm_ref = refs[group:]
    pg = pl.program_id(1)
    n_cmp = n_pages * (PAGE_SIZE // CMP_BLOCK)
    per = PAGE_SIZE // CMP_BLOCK * group

    x = jnp.concatenate([p[...] for p in pages], axis=-1)
    kvm_ref[pg] = _split_dot(x, _pool_matrix(group * PAGE_SIZE, CMP_BLOCK, per))

    @pl.when(pg == pl.num_programs(1) - 1)
    def _():
        hd = B_HEAD_DIM
        n_sel = n_cmp // 2
        past = n_pages * PAGE_SIZE
        sums = jnp.concatenate([kvm_ref[s] for s in range(n_pages // group)], axis=-1)
        kv = _compress_cols((sums + _pe_sums_t(pet_ref)) * (1.0 / CMP_BLOCK), wt_ref)
        q8 = q_ref[0] * (hd ** -0.5)
        row = lax.broadcasted_iota(jnp.int32, (SUBLANES, 1), 0)
        head0 = row < B_GROUP
        lc = jnp.where(head0, _dot(q8, kv[0:hd]), _dot(q8, kv[hd:2 * hd]))
        blk = lax.broadcasted_iota(jnp.int32, (1, n_cmp), 1)
        dist = past - (blk * CMP_BLOCK + CMP_BLOCK - 1)
        p = _masked_softmax(jnp.where(dist >= 0, lc + _bias_rows(dist, rbt_ref[...]), NEG_INF))
        oc_ref[0] = jnp.where(head0, _dot_nt(p, kv[2 * hd:3 * hd]), _dot_nt(p, kv[3 * hd:]))
        pool = (lax.broadcasted_iota(jnp.int32, (n_cmp, n_sel), 0) // (SEL_BLOCK // CMP_BLOCK)
                == lax.broadcasted_iota(jnp.int32, (n_cmp, n_sel), 1))
        pp = _dot_exact(p, jnp.where(pool, 1.0, 0.0))
        imp0 = jnp.sum(pp[0:B_GROUP], axis=0, keepdims=True)
        imp1 = jnp.sum(pp[B_GROUP:2 * B_GROUP], axis=0, keepdims=True)
        imp_ref[0] = jnp.where(row == 0, imp0, jnp.where(row == 1, imp1, 0.0))


def _sample_cmp(page_table, cache_t, q8, pe_t, w_t, rbt, l, group=32):
    ns, n_pages = page_table.shape
    group = min(group, n_pages)
    n_cmp = n_pages * (PAGE_SIZE // CMP_BLOCK)
    page_spec = lambda k: pl.BlockSpec((None, None, _KV_ROWS, PAGE_SIZE),
                                       lambda b, g, pt: (l, pt[b, g * group + k], 0, 0))
    return pl.pallas_call(
        functools.partial(_scmp_kernel, n_pages=n_pages, group=group),
        grid_spec=pltpu.PrefetchScalarGridSpec(
            num_scalar_prefetch=1,
            grid=(ns, n_pages // group),
            in_specs=[page_spec(k) for k in range(group)] + [
                pl.BlockSpec((1, SUBLANES, B_HEAD_DIM), lambda b, g, pt: (b, 0, 0)),
                pl.BlockSpec((None, 2, B_HEAD_DIM, CMP_BLOCK), lambda b, g, pt: (l, 0, 0, 0)),
                pl.BlockSpec((None, 2, B_HEAD_DIM, B_HEAD_DIM), lambda b, g, pt: (l, 0, 0, 0)),
                pl.BlockSpec((SUBLANES, REL_BUCKETS), lambda b, g, pt: (0, 0))],
            out_specs=[pl.BlockSpec((1, SUBLANES, B_HEAD_DIM), lambda b, g, pt: (b, 0, 0)),
                       pl.BlockSpec((1, SUBLANES, n_cmp // 2), lambda b, g, pt: (b, 0, 0))],
            scratch_shapes=[pltpu.VMEM((n_pages // group, _KV_ROWS, n_cmp * group // n_pages), _F32)]),
        out_shape=[jax.ShapeDtypeStruct((ns, SUBLANES, B_HEAD_DIM), _F32),
                   jax.ShapeDtypeStruct((ns, SUBLANES, n_cmp // 2), _F32)],
        compiler_params=_cparams(("arbitrary", "arbitrary")),
        name="nsa_sample_cmp",
    )(page_table, *([cache_t] * group), q8, pe_t, w_t, rbt)


def _stopk_kernel(imp_ref, idx_ref, *, k):
    s = imp_ref[...]
    n = s.shape[-1]
    lane = lax.broadcasted_iota(jnp.int32, s.shape, 1)
    s = jnp.where(lane == 0, FORCE_SCORE, s)

    def body(it, carry):
        s, out = carry
        m = jnp.max(s, axis=-1, keepdims=True)
        first = jnp.min(jnp.where(s == m, lane, n), axis=-1, keepdims=True)
        return jnp.where(lane == first, -jnp.inf, s), jnp.where(lane == it, first, out)

    _, out = lax.fori_loop(0, k, body, (s, jnp.zeros(s.shape, jnp.int32)))
    idx_ref[...] = out


def _sample_topk(imp2d, k):
    return pl.pallas_call(
        functools.partial(_stopk_kernel, k=k),
        out_shape=jax.ShapeDtypeStruct(imp2d.shape, jnp.int32),
        name="nsa_sample_topk",
    )(imp2d)


def _sattn_kernel(pt_ref, idx_ref, *refs, n_blk, past):
    blocks = refs[:2 * n_blk]
    q_ref, gt_ref, new_ref, oc_ref, win_ref, rbt_ref, o_ref = refs[2 * n_blk:]
    b = pl.program_id(0)
    hd = B_HEAD_DIM
    q8 = q_ref[0] * (hd ** -0.5)
    rbt = rbt_ref[...]
    row = lax.broadcasted_iota(jnp.int32, (SUBLANES, 1), 0)
    head0 = row < B_GROUP
    new = new_ref[0]
    bias0 = _bias_rows(jnp.zeros((1, 1), jnp.int32), rbt)

    def attend(s, v_of_head, k_new, v_new):
        s_new = jnp.sum(q8 * k_new, axis=-1, keepdims=True) + bias0
        m = jnp.maximum(jnp.max(s, axis=-1, keepdims=True), s_new)
        p = jnp.exp(s - m)
        p_new = jnp.exp(s_new - m)
        den = jnp.sum(p, axis=-1, keepdims=True) + p_new
        num = jnp.where(head0, _dot_nt(p, v_of_head(0)), _dot_nt(p, v_of_head(1))) + p_new * v_new
        return num / den

    def per_head(a0, a1):
        return jnp.where(head0, a0, a1)

    per_page = PAGE_SIZE // SEL_BLOCK
    t_in = lax.broadcasted_iota(jnp.int32, (1, PAGE_SIZE), 1)
    s_h, vs = [], []
    for h in range(B_KV_HEADS):
        kt = jnp.concatenate([blocks[h * n_blk + k][h * hd:(h + 1) * hd, :] for k in range(n_blk)], axis=-1)
        vs.append(jnp.concatenate(
            [blocks[h * n_blk + k][B_KV_WIDTH + h * hd:B_KV_WIDTH + (h + 1) * hd, :] for k in range(n_blk)], axis=-1))
        dist = []
        for k in range(n_blk):
            blk = idx_ref[b, h, k]
            in_blk = t_in // SEL_BLOCK == blk % per_page
            dist.append(jnp.where(in_blk, past - ((blk // per_page) * PAGE_SIZE + t_in), -1))
        dist = jnp.concatenate(dist, axis=-1)
        s_h.append(jnp.where(dist >= 0, _dot(q8, kt) + _bias_rows(dist, rbt), NEG_INF))
    ksn = new[:, 2 * B_KV_WIDTH:3 * B_KV_WIDTH]
    vsn = new[:, 3 * B_KV_WIDTH:4 * B_KV_WIDTH]
    o_s = attend(per_head(s_h[0], s_h[1]), lambda h: vs[h],
                 per_head(ksn[:, :hd], ksn[:, hd:]), per_head(vsn[:, :hd], vsn[:, hd:]))

    win = win_ref[0]
    wb = win.shape[1]
    dist = wb - lax.broadcasted_iota(jnp.int32, (1, wb), 1)
    okw = (dist < WINDOW) & (past - dist >= 0)
    lw = per_head(_dot(q8, win[0:hd]), _dot(q8, win[hd:2 * hd]))
    sw = jnp.where(okw, lw + _bias_rows(dist, rbt), NEG_INF)
    kwn = new[:, 4 * B_KV_WIDTH:5 * B_KV_WIDTH]
    vwn = new[:, 5 * B_KV_WIDTH:6 * B_KV_WIDTH]
    o_w = attend(sw, lambda h: win[B_KV_WIDTH + h * hd:B_KV_WIDTH + (h + 1) * hd],
                 per_head(kwn[:, :hd], kwn[:, hd:]), per_head(vwn[:, :hd], vwn[:, hd:]))

    gate = jax.nn.sigmoid(gt_ref[0])
    o_ref[0] = gate[:, 0:1] * oc_ref[0] + gate[:, 1:2] * o_s + gate[:, 2:3] * o_w


def _sample_attn(page_table, idx, cache_t, q8, gate8, kv_new, oc, win_t, rbt, l):
    ns, n_pages = page_table.shape
    n_blk = idx.shape[-1]
    past = n_pages * PAGE_SIZE
    per_page = PAGE_SIZE // SEL_BLOCK

    pages = jnp.take_along_axis(page_table[:, None, :], idx // per_page, axis=-1)

    def blk_spec(h, k):
        return pl.BlockSpec((None, None, _KV_ROWS, PAGE_SIZE), lambda b, pg, ix: (l, pg[b, h, k], 0, 0))

    wb = win_t.shape[3]
    return pl.pallas_call(
        functools.partial(_sattn_kernel, n_blk=n_blk, past=past),
        grid_spec=pltpu.PrefetchScalarGridSpec(
            num_scalar_prefetch=2,
            grid=(ns,),
            in_specs=[blk_spec(h, k) for h in range(B_KV_HEADS) for k in range(n_blk)] + [
                pl.BlockSpec((1, SUBLANES, B_HEAD_DIM), lambda b, pt, ix: (b, 0, 0)),
                pl.BlockSpec((1, SUBLANES, LANES), lambda b, pt, ix: (b, 0, 0)),
                pl.BlockSpec((1, 1, 6 * B_KV_WIDTH), lambda b, pt, ix: (b, 0, 0)),
                pl.BlockSpec((1, SUBLANES, B_HEAD_DIM), lambda b, pt, ix: (b, 0, 0)),
                pl.BlockSpec((None, 1, _KV_ROWS, wb), lambda b, pt, ix: (l, b, 0, 0)),
                pl.BlockSpec((SUBLANES, REL_BUCKETS), lambda b, pt, ix: (0, 0))],
            out_specs=pl.BlockSpec((1, SUBLANES, B_HEAD_DIM), lambda b, pt, ix: (b, 0, 0))),
        out_shape=jax.ShapeDtypeStruct((ns, SUBLANES, B_HEAD_DIM), _F32),
        compiler_params=_cparams(("arbitrary",)),
        name="nsa_sample_attn",
    )(pages, idx, *([cache_t] * (B_KV_HEADS * n_blk)), q8, gate8, kv_new, oc, win_t, rbt)


def _smix_kernel(uv_ref, ng_ref, wd_ref, b0_ref, xbc_ref, st_ref, cw_ref, cb_ref, dt_ref, hp_ref,
                 dsk_ref, ex_ref, oa_ref, v_ref, xdt_ref, ea_ref, y1_ref, bc_ref):
    u, v = _gelu_ln(uv_ref[...], ng_ref[...])
    v_ref[...] = v
    oa_ref[...] = u * (v * wd_ref[...] + b0_ref[...])
    conv = cb_ref[...] + xbc_ref[...] * cw_ref[C_CONV - 1:C_CONV, :]
    for k in range(C_CONV - 1):
        conv = conv + st_ref[k] * cw_ref[k:k + 1, :]
    xc = _silu(conv)
    xs = xc[:, :C_WIDTH]
    bm = xc[:, C_WIDTH:C_WIDTH + C_GROUPS * C_STATE]
    cm = xc[:, C_WIDTH + C_GROUPS * C_STATE:]
    bc_ref[...] = xc[:, C_WIDTH:]
    dt = jax.nn.softplus(dt_ref[...] + hp_ref[0:1, :])
    acum = dt * (-jnp.exp(hp_ref[1:2, :]))
    dt_rep = _dot_exact(dt, ex_ref[...])
    ea_ref[...] = jnp.exp(_dot_exact(acum, ex_ref[...]))
    xdt = dt_rep * xs
    xdt_ref[...] = xdt
    per_g = C_WIDTH // C_GROUPS
    cb = [jnp.sum(cm[:, g * C_STATE:(g + 1) * C_STATE] * bm[:, g * C_STATE:(g + 1) * C_STATE],
                  axis=-1, keepdims=True) for g in range(C_GROUPS)]
    lane = lax.broadcasted_iota(jnp.int32, xs.shape, 1)
    y1_ref[...] = jnp.where(lane < per_g, cb[0], cb[1]) * xdt + dsk_ref[...] * xs


def _sample_mix(uv, ng_row, wd_row, b0_row, xbc, st, conv_w_l, conv_b_row, dt, hp, dsk_row, expand):
    ns = uv.shape[0]
    f = lambda w: jax.ShapeDtypeStruct((ns, w), _F32)
    return pl.pallas_call(
        _smix_kernel,
        out_shape=[f(A_WIDTH), f(A_WIDTH), f(C_WIDTH), f(C_WIDTH), f(C_WIDTH), f(2 * C_GROUPS * C_STATE)],
        name="sample_gmlp_conv",
    )(uv, ng_row, wd_row, b0_row, xbc, st, conv_w_l, conv_b_row, dt, hp, dsk_row, expand)


def _sssm_kernel(h0_ref, xdt_ref, ea_ref, y1_ref, bc_ref, z_ref, ng_ref, o_ref, hout_ref):
    per_g = C_WIDTH // C_GROUPS
    h0 = h0_ref[0]
    bc = bc_ref[0]
    bm = bc[:, :C_GROUPS * C_STATE]
    cm = bc[:, C_GROUPS * C_STATE:]
    c8 = [jnp.broadcast_to(cm[:, g * C_STATE:(g + 1) * C_STATE], (SUBLANES, C_STATE)) for g in range(C_GROUPS)]
    ch = jnp.concatenate([_dot_nt(c8[g], h0[g * per_g:(g + 1) * per_g, :])[0:1] for g in range(C_GROUPS)], axis=-1)
    y = y1_ref[0] + ch * ea_ref[0]
    o_ref[0] = _rms(y * _silu(z_ref[0]), ng_ref[...])
    row = lax.broadcasted_iota(jnp.int32, (LANES, C_WIDTH), 0)
    cols = jnp.where(row == 0, xdt_ref[0], jnp.where(row == 1, ea_ref[0], 0.0)).T
    rsel = lax.broadcasted_iota(jnp.int32, (C_WIDTH, C_STATE), 0) < per_g
    b_full = jnp.where(rsel, bm[:, :C_STATE], bm[:, C_STATE:])
    hout_ref[0] = h0 * cols[:, 1:2] + cols[:, 0:1] * b_full


def _sample_ssm(h0, xdt, ea, y1, bc, z, ng_row, l):
    ns = xdt.shape[0]
    r3 = lambda a: a.reshape(ns, 1, a.shape[-1])
    row_spec = lambda w: pl.BlockSpec((1, 1, w), lambda b: (b, 0, 0))
    out, hout = pl.pallas_call(
        _sssm_kernel,
        grid=(ns,),
        in_specs=[pl.BlockSpec((None, 1, C_WIDTH, C_STATE), lambda b: (l, b, 0, 0)),
                  row_spec(C_WIDTH), row_spec(C_WIDTH), row_spec(C_WIDTH), row_spec(2 * C_GROUPS * C_STATE),
                  row_spec(C_WIDTH), pl.BlockSpec((1, C_WIDTH), lambda b: (0, 0))],
        out_specs=[row_spec(C_WIDTH), pl.BlockSpec((1, C_WIDTH, C_STATE), lambda b: (b, 0, 0))],
        out_shape=[jax.ShapeDtypeStruct((ns, 1, C_WIDTH), _F32),
                   jax.ShapeDtypeStruct((ns, C_WIDTH, C_STATE), _F32)],
        compiler_params=_cparams(("parallel",)),
        name="sample_ssm",
    )(h0, r3(xdt), r3(ea), r3(y1), r3(bc), r3(z), ng_row)
    return out.reshape(ns, C_WIDTH), hout


def _pad_cols(a, width):
    return jnp.pad(a, ((0, 0),) * (a.ndim - 1) + ((0, width - a.shape[-1]),))


def _pack_w_in(w_in):
    o = np.cumsum((0, A_WIDTH, A_WIDTH, B_WIDTH, 6 * B_KV_WIDTH, _GATE_COLS, C_WIDTH, C_CONV_DIM, C_HEADS))
    u0, q0, kv0, gate0, z0, xbc0, dt0, end = o[0], o[2], o[3], o[4], o[5], o[6], o[7], o[8]
    parts = [w_in[..., u0:kv0], w_in[..., z0:xbc0], w_in[..., xbc0:dt0],
             _pad_cols(w_in[..., gate0:z0], LANES), _pad_cols(w_in[..., dt0:end], LANES)]
    w_rows = jnp.concatenate(parts, axis=-1).astype(_MXU)
    w_kv_t = jnp.swapaxes(w_in[..., kv0:gate0], -1, -2).astype(_MXU)
    return w_rows, w_kv_t


def _head_rows(dt_bias, a_log):
    hp = jnp.zeros((DEPTH, SUBLANES, LANES), _F32)
    hp = hp.at[:, 0, :C_HEADS].set(dt_bias)
    return hp.at[:, 1, :C_HEADS].set(a_log)


def kernel(x_prompt, x_sample, cache_cmp_kv, cache_slc_kv, page_table, state_win_kv, state_conv, state_ssm,
           norm_g, ffn_w_gate, ffn_w_up, ffn_w_down, w_in, w_out, gmlp_norm_g, gmlp_w_s, gmlp_b_s,
           nsa_pe_cmp, nsa_w_cmp, rel_bias, conv_w, conv_b, dt_bias, a_log, d_skip, ssm_norm_g):
    bp, t = x_prompt.shape[:2]
    ns = x_sample.shape[0]
    n_pages = page_table.shape[1]
    n_phys = cache_cmp_kv.shape[1]
    tm_p, tm_s = 512, ns
    assert t % _KEY_CHUNK == 0 and t % tm_p == 0

    wg, wu, wd = (w.astype(_MXU) for w in (ffn_w_gate, ffn_w_up, ffn_w_down))
    w_in_p, w_kv_t = _pack_w_in(w_in)
    w_out_b = w_out.astype(_MXU)
    pe_t = jnp.swapaxes(nsa_pe_cmp, -1, -2)
    wc_t = jnp.swapaxes(nsa_w_cmp, -1, -2)
    bs_t = jnp.swapaxes(gmlp_b_s, 1, 2)
    wdiag = jnp.repeat(gmlp_w_s[:, :, 0, 0], A_HEAD_DIM, axis=-1)
    b0 = jnp.repeat(gmlp_b_s[:, :, 0], A_HEAD_DIM, axis=-1)
    hp = _head_rows(dt_bias, a_log)
    dsk = jnp.repeat(d_skip, C_HEAD_DIM, axis=-1)
    expand = (jnp.arange(LANES)[:, None] == jnp.arange(C_WIDTH)[None, :] // C_HEAD_DIM).astype(_F32)
    rb_flat = rel_bias.reshape(-1)
    rbt = _pad_cols(rel_bias, SUBLANES).T
    def chan_major(a):
        lead, tok = a.shape[:-4], a.shape[-4]
        perm = tuple(range(len(lead))) + tuple(len(lead) + k for k in (1, 2, 3, 0))
        return a.transpose(perm).reshape(*lead, _KV_ROWS, tok)

    def token_major(a_t):
        lead, tok = a_t.shape[:-2], a_t.shape[-1]
        a5 = a_t.reshape(*lead, 2, B_KV_HEADS, B_HEAD_DIM, tok)
        perm = tuple(range(len(lead))) + tuple(len(lead) + k for k in (3, 0, 1, 2))
        return a5.transpose(perm)

    cache_c = chan_major(cache_cmp_kv)
    cache_s = chan_major(cache_slc_kv)
    win_state = chan_major(state_win_kv)
    ssm_state = state_ssm.reshape(DEPTH, ns, C_WIDTH, C_STATE)

    xp = x_prompt.reshape(bp * t, D_MODEL)
    xs = x_sample.reshape(ns, D_MODEL)
    outs = [[] for _ in range(11)]
    for l in range(DEPTH):
        g = norm_g[l]
        row = lambda a: a.reshape(1, -1)
        xp = _ffn(xp, g[0:2], wg, wu, wd, l, 0, tm_p)
        xs = _ffn(xs, g[0:2], wg, wu, wd, l, 0, tm_s)

        uv, q, z, xbc, gate, dt, kvc_t, kvs_t, kvw_t = _inproj(xp.reshape(bp, t, D_MODEL), row(g[2]), w_in_p,
                                                               w_kv_t, l, tm_p)
        oa = _gmlp_prompt(uv.reshape(bp * t, 2 * A_WIDTH), row(gmlp_norm_g[l]), gmlp_w_s, bs_t, l)
        kvcmp_t = _compress_prompt(kvc_t, pe_t, wc_t, l)
        ob = _nsa_prompt(q, gate, *_nsa_prompt_inputs(kvcmp_t, kvs_t, kvw_t), rb_flat)
        oc, h_p = _mamba_prompt(z, xbc, dt, conv_w, row(conv_b[l]), hp[l], row(dsk[l]), row(ssm_norm_g[l]), l)
        xp = _outproj(xp, oa, ob.reshape(bp * t, B_WIDTH), oc.reshape(bp * t, C_WIDTH), row(g[3]), w_out_b, l, tm_p)
        wkeep = min(WINDOW, t)
        outs[0].append(token_major(kvc_t))
        outs[1].append(token_major(kvs_t))
        outs[2].append(token_major(kvw_t[:, :, t - wkeep:]))
        outs[3].append(xbc[:, t - (C_CONV - 1):])
        outs[4].append(h_p)

        uv, q, z, xbc, gate, dt, kvc_t, kvs_t, kvw_t = (
            a[0] for a in _inproj(xs.reshape(1, ns, D_MODEL), row(g[2]), w_in_p, w_kv_t, l, tm_s))
        q8 = jnp.pad(q.reshape(ns, B_HEADS, B_HEAD_DIM), ((0, 0), (0, SUBLANES - B_HEADS), (0, 0)))
        gate8 = jnp.pad(gate[:, :_GATE_COLS].reshape(ns, B_HEADS, N_BRANCH),
                        ((0, 0), (0, SUBLANES - B_HEADS), (0, LANES - N_BRANCH)))
        o_cmp, imp = _sample_cmp(page_table, cache_c, q8, pe_t, wc_t, rbt, l)
        n_sel_past = imp.shape[-1]
        idx = _sample_topk(imp.reshape(ns * SUBLANES, n_sel_past), SEL_TOPK - 1)
        idx = idx.reshape(ns, SUBLANES, n_sel_past)[:, :B_KV_HEADS, :SEL_TOPK - 1]
        kvc, kvs, kvwin = kvc_t.T, kvs_t.T, kvw_t.T
        kv_new = jnp.concatenate([kvc, kvs, kvwin], axis=-1).reshape(ns, 1, 6 * B_KV_WIDTH)
        ob8 = _sample_attn(page_table, idx, cache_s, q8, gate8, kv_new, o_cmp, win_state, rbt, l)
        ob = ob8[:, :B_HEADS].reshape(ns, B_WIDTH)
        st = jnp.swapaxes(state_conv[l], 0, 1)
        oa, v_rows, xdt, ea, y1, bc = _sample_mix(uv, row(gmlp_norm_g[l]), row(wdiag[l]), row(b0[l]), xbc, st,
                                                  conv_w[l], row(conv_b[l]), dt, hp[l], row(dsk[l]), expand)
        oc, h_s = _sample_ssm(ssm_state, xdt, ea, y1, bc, z, row(ssm_norm_g[l]), l)
        xs = _outproj(xs, oa, ob, oc, row(g[3]), w_out_b, l, tm_s)
        outs[5].append(kvc.reshape(ns, 1, 2, B_KV_HEADS, B_HEAD_DIM))
        outs[6].append(kvs.reshape(ns, 1, 2, B_KV_HEADS, B_HEAD_DIM))
        outs[7].append(token_major(jnp.concatenate([win_state[l][:, :, 1:], kvwin[:, :, None]], axis=-1)))
        outs[8].append(jnp.concatenate([state_conv[l][:, 1:], xbc[:, None]], axis=1))
        outs[9].append(h_s.reshape(ns, C_HEADS, C_HEAD_DIM, C_STATE))
        outs[10].append(v_rows.reshape(ns, 1, A_WIDTH))

        xp = _ffn(xp, g[4:6], wg, wu, wd, l, 1, tm_p)
        xs = _ffn(xs, g[4:6], wg, wu, wd, l, 1, tm_s)
    stacked = [jnp.stack(o) for o in outs]
    return (xp.reshape(bp, t, D_MODEL), xs.reshape(ns, 1, D_MODEL), *stacked)
```

```python
import functools
import math

import numpy as np
import jax
import jax.numpy as jnp
from jax import lax
from jax.experimental import pallas as pl
from jax.experimental.pallas import tpu as pltpu

D_MODEL = 1024
DEPTH = 4
PAGE_SIZE = 128
A_HEADS, A_HEAD_DIM, A_CHUNK = 4, 64, 128
A_WIDTH = A_HEADS * A_HEAD_DIM
B_HEADS, B_KV_HEADS, B_HEAD_DIM = 6, 2, 64
B_GROUP = B_HEADS // B_KV_HEADS
B_WIDTH = B_HEADS * B_HEAD_DIM
B_KV_WIDTH = B_KV_HEADS * B_HEAD_DIM
N_BRANCH = 3
CMP_BLOCK, SEL_BLOCK, SEL_TOPK, WINDOW, Q_BLOCK = 32, 64, 16, 512, 128
FORCE_SCORE = 1e4
C_HEADS, C_HEAD_DIM, C_GROUPS, C_STATE, C_CONV = 6, 64, 2, 64, 4
C_WIDTH = C_HEADS * C_HEAD_DIM
C_CONV_DIM = C_WIDTH + 2 * C_GROUPS * C_STATE
SSD_CHUNK = 128
D_FF = 2816
REL_BUCKETS, REL_MAX_EXACT, REL_MAX_DIST = 32, 16, 128
EPS = 1e-6
NEG_INF = -1e30
SOFTMAX_FLOOR = -1e20

LANES = 128
SUBLANES = 8
VMEM_LIMIT_BYTES = 56 * 1024 * 1024

_MXU = jnp.bfloat16
_F32 = jnp.float32

_GATE_COLS = N_BRANCH * B_HEADS
_SEG = (("uv", 2 * A_WIDTH), ("q", B_WIDTH), ("z", C_WIDTH), ("xbc", C_CONV_DIM), ("gate", LANES), ("dt", LANES))
_D_IN_PAD = sum(w for _, w in _SEG)
_KV_ROWS = 2 * B_KV_WIDTH
_N_KV = 3


def _dot(a, b):
    return jnp.dot(a.astype(_MXU), b.astype(_MXU), preferred_element_type=_F32)


def _dot_nt(a, b):
    return lax.dot_general(a.astype(_MXU), b.astype(_MXU), (((1,), (1,)), ((), ())),
                           preferred_element_type=_F32)


def _dot_tn(a, b):
    return lax.dot_general(a.astype(_MXU), b.astype(_MXU), (((0,), (0,)), ((), ())),
                           preferred_element_type=_F32)


def _dot_exact(a, b):
    return jnp.dot(a, b, preferred_element_type=_F32, precision=lax.Precision.HIGHEST)


def _rms(x, g):
    return x * lax.rsqrt(jnp.mean(x * x, axis=-1, keepdims=True) + EPS) * g


def _silu(x):
    return x * jax.nn.sigmoid(x)


def _cparams(sem):
    return pltpu.CompilerParams(dimension_semantics=sem, vmem_limit_bytes=VMEM_LIMIT_BYTES)


def _bucket_np(dist):
    n = np.maximum(dist, 0)
    nf = np.maximum(n, 1).astype(np.float32)
    large = REL_MAX_EXACT + (np.log(nf / np.float32(REL_MAX_EXACT))
                             / np.float32(math.log(REL_MAX_DIST / REL_MAX_EXACT))
                             * np.float32(REL_BUCKETS - REL_MAX_EXACT)).astype(np.int32)
    large = np.minimum(large, REL_BUCKETS - 1)
    return np.where(n < REL_MAX_EXACT, n, large).astype(np.int32)


def _ffn_kernel(x_ref, g_ref, wg_ref, wu_ref, wd_ref, o_ref, *, tf):
    x = x_ref[...]
    h = _rms(x, g_ref[0:1, :]).astype(_MXU)
    y = jnp.zeros(x.shape, _F32)
    for f in range(D_FF // tf):
        cols = slice(f * tf, (f + 1) * tf)
        a = _silu(_dot(h, wg_ref[:, cols])) * _dot(h, wu_ref[:, cols])
        y = y + _dot(a, wd_ref[cols, :])
    o_ref[...] = x + 0.5 * _rms(y, g_ref[1:2, :])


def _ffn(x, g2, wg, wu, wd, l, j, tm, tf=256):
    rows = x.shape[0]
    once = pl.Buffered(1)
    return pl.pallas_call(
        functools.partial(_ffn_kernel, tf=tf),
        grid=(rows // tm,),
        in_specs=[pl.BlockSpec((tm, D_MODEL), lambda r: (r, 0)),
                  pl.BlockSpec((2, D_MODEL), lambda r: (0, 0)),
                  pl.BlockSpec((None, None, D_MODEL, D_FF), lambda r: (l, j, 0, 0), pipeline_mode=once),
                  pl.BlockSpec((None, None, D_MODEL, D_FF), lambda r: (l, j, 0, 0), pipeline_mode=once),
                  pl.BlockSpec((None, None, D_FF, D_MODEL), lambda r: (l, j, 0, 0), pipeline_mode=once)],
        out_specs=pl.BlockSpec((tm, D_MODEL), lambda r: (r, 0)),
        out_shape=jax.ShapeDtypeStruct((rows, D_MODEL), _F32),
        compiler_params=_cparams(("parallel",)),
        name="half_ffn",
    )(x, g2, wg, wu, wd)


def _inproj_kernel(x_ref, g_ref, w_ref, wkv_ref, *o_refs):
    h = _rms(x_ref[0], g_ref[...]).astype(_MXU)
    off = 0
    for (_, width), o_ref in zip(_SEG, o_refs):
        o_ref[0] = _dot(h, w_ref[:, off:off + width])
        off += width
    for k, o_ref in enumerate(o_refs[len(_SEG):]):
        o_ref[0] = _dot_nt(wkv_ref[k * _KV_ROWS:(k + 1) * _KV_ROWS, :], h)


def _inproj(x, g_row, w_in_p, w_kv_t, l, tm):
    nb, t = x.shape[:2]
    return pl.pallas_call(
        _inproj_kernel,
        grid=(nb, t // tm),
        in_specs=[pl.BlockSpec((1, tm, D_MODEL), lambda b, r: (b, r, 0)),
                  pl.BlockSpec((1, D_MODEL), lambda b, r: (0, 0)),
                  pl.BlockSpec((None, D_MODEL, _D_IN_PAD), lambda b, r: (l, 0, 0)),
                  pl.BlockSpec((None, _N_KV * _KV_ROWS, D_MODEL), lambda b, r: (l, 0, 0))],
        out_specs=[pl.BlockSpec((1, tm, w), lambda b, r: (b, r, 0)) for _, w in _SEG]
        + [pl.BlockSpec((1, _KV_ROWS, tm), lambda b, r: (b, 0, r))] * _N_KV,
        out_shape=[jax.ShapeDtypeStruct((nb, t, w), _F32) for _, w in _SEG]
        + [jax.ShapeDtypeStruct((nb, _KV_ROWS, t), _F32)] * _N_KV,
        compiler_params=_cparams(("parallel", "parallel")),
        name="in_proj",
    )(x, g_row, w_in_p, w_kv_t)


def _outproj_kernel(x_ref, oa_ref, ob_ref, oc_ref, g_ref, w_ref, o_ref):
    y = (_dot(oa_ref[...], w_ref[0:A_WIDTH, :])
         + _dot(ob_ref[...], w_ref[A_WIDTH:A_WIDTH + B_WIDTH, :])
         + _dot(oc_ref[...], w_ref[A_WIDTH + B_WIDTH:, :]))
    o_ref[...] = x_ref[...] + _rms(y, g_ref[...])


def _outproj(x, oa, ob, oc, g_row, w_out, l, tm):
    rows = x.shape[0]
    return pl.pallas_call(
        _outproj_kernel,
        grid=(rows // tm,),
        in_specs=[pl.BlockSpec((tm, D_MODEL), lambda r: (r, 0)),
                  pl.BlockSpec((tm, A_WIDTH), lambda r: (r, 0)),
                  pl.BlockSpec((tm, B_WIDTH), lambda r: (r, 0)),
                  pl.BlockSpec((tm, C_WIDTH), lambda r: (r, 0)),
                  pl.BlockSpec((1, D_MODEL), lambda r: (0, 0)),
                  pl.BlockSpec((None, D_MODEL, D_MODEL), lambda r: (l, 0, 0))],
        out_specs=pl.BlockSpec((tm, D_MODEL), lambda r: (r, 0)),
        out_shape=jax.ShapeDtypeStruct((rows, D_MODEL), _F32),
        compiler_params=_cparams(("parallel",)),
        name="out_proj",
    )(x, oa, ob, oc, g_row, w_out)


def _mix_ffn_kernel(x_ref, oa_ref, ob_ref, oc_ref, gm_ref, wo_ref, g_ref, wg_ref, wu_ref, wd_ref, o_ref, *, tf):
    y = (_dot(oa_ref[...], wo_ref[0:A_WIDTH, :])
         + _dot(ob_ref[...], wo_ref[A_WIDTH:A_WIDTH + B_WIDTH, :])
         + _dot(oc_ref[...], wo_ref[A_WIDTH + B_WIDTH:, :]))
    x = x_ref[...] + _rms(y, gm_ref[...])
    h = _rms(x, g_ref[0:1, :]).astype(_MXU)
    y = jnp.zeros(x.shape, _F32)
    for f in range(D_FF // tf):
        cols = slice(f * tf, (f + 1) * tf)
        a = _silu(_dot(h, wg_ref[:, cols])) * _dot(h, wu_ref[:, cols])
        y = y + _dot(a, wd_ref[cols, :])
    o_ref[...] = x + 0.5 * _rms(y, g_ref[1:2, :])


def _mix_ffn(x, oa, ob, oc, gm_row, w_out, g2, wg, wu, wd, l, j, tm, tf=256):
    rows = x.shape[0]
    once = pl.Buffered(1)
    rowblk = lambda w: pl.BlockSpec((tm, w), lambda r: (r, 0))
    return pl.pallas_call(
        functools.partial(_mix_ffn_kernel, tf=tf),
        grid=(rows // tm,),
        in_specs=[rowblk(D_MODEL), rowblk(A_WIDTH), rowblk(B_WIDTH), rowblk(C_WIDTH),
                  pl.BlockSpec((1, D_MODEL), lambda r: (0, 0)),
                  pl.BlockSpec((None, D_MODEL, D_MODEL), lambda r: (l, 0, 0), pipeline_mode=once),
                  pl.BlockSpec((2, D_MODEL), lambda r: (0, 0)),
                  pl.BlockSpec((None, None, D_MODEL, D_FF), lambda r: (l, j, 0, 0), pipeline_mode=once),
                  pl.BlockSpec((None, None, D_MODEL, D_FF), lambda r: (l, j, 0, 0), pipeline_mode=once),
                  pl.BlockSpec((None, None, D_FF, D_MODEL), lambda r: (l, j, 0, 0), pipeline_mode=once)],
        out_specs=rowblk(D_MODEL),
        out_shape=jax.ShapeDtypeStruct((rows, D_MODEL), _F32),
        compiler_params=_cparams(("parallel",)),
        name="out_proj_ffn",
    )(x, oa, ob, oc, gm_row, w_out, g2, wg, wu, wd)


def _gelu_ln(uv, ng):
    u = jax.nn.gelu(uv[:, :A_WIDTH])
    v = jax.nn.gelu(uv[:, A_WIDTH:])
    mu = jnp.mean(v, axis=-1, keepdims=True)
    var = jnp.mean(jnp.square(v - mu), axis=-1, keepdims=True)
    return u, (v - mu) * lax.rsqrt(var + EPS) * ng


def _gmlp_kernel(uv_ref, ng_ref, ws_ref, bs_ref, o_ref, *, chunks):
    row = lax.broadcasted_iota(jnp.int32, (A_CHUNK, A_CHUNK), 0)
    col = lax.broadcasted_iota(jnp.int32, (A_CHUNK, A_CHUNK), 1)
    ws = [jnp.where(col <= row, ws_ref[h], 0.0).astype(_MXU) for h in range(A_HEADS)]
    for c in range(chunks):
        u, v = _gelu_ln(uv_ref[c * A_CHUNK:(c + 1) * A_CHUNK, :], ng_ref[...])
        sg = [_dot(ws[h], v[:, h * A_HEAD_DIM:(h + 1) * A_HEAD_DIM]) + bs_ref[:, h:h + 1]
              for h in range(A_HEADS)]
        o_ref[c * A_CHUNK:(c + 1) * A_CHUNK, :] = u * jnp.concatenate(sg, axis=-1)


def _gmlp_prompt(uv, ng_row, ws, bs_t, l, chunks=4):
    rows = uv.shape[0]
    tm = chunks * A_CHUNK
    return pl.pallas_call(
        functools.partial(_gmlp_kernel, chunks=chunks),
        grid=(rows // tm,),
        in_specs=[pl.BlockSpec((tm, 2 * A_WIDTH), lambda r: (r, 0)),
                  pl.BlockSpec((1, A_WIDTH), lambda r: (0, 0)),
                  pl.BlockSpec((None, A_HEADS, A_CHUNK, A_CHUNK), lambda r: (l, 0, 0, 0)),
                  pl.BlockSpec((None, A_CHUNK, A_HEADS), lambda r: (l, 0, 0))],
        out_specs=pl.BlockSpec((tm, A_WIDTH), lambda r: (r, 0)),
        out_shape=jax.ShapeDtypeStruct((rows, A_WIDTH), _F32),
        compiler_params=_cparams(("parallel",)),
        name="gmlp_prompt",
    )(uv, ng_row, ws, bs_t)


def _mamba_kernel(z_ref, xbc_ref, dt_ref, cw_ref, cb_ref, hp_ref, dsk_ref, ng_ref,
                  o_ref, hout_ref, xp_ref, hs_ref):
    t = pl.program_id(0)
    L = SSD_CHUNK
    hist = SUBLANES

    @pl.when(t == 0)
    def _():
        xp_ref[:, 0:hist, :] = jnp.zeros((xp_ref.shape[0], hist, C_CONV_DIM), _F32)
        hs_ref[...] = jnp.zeros_like(hs_ref)

    a_row = -jnp.exp(hp_ref[1:2, :])
    row = lax.broadcasted_iota(jnp.int32, (L, L), 0)
    col = lax.broadcasted_iota(jnp.int32, (L, L), 1)
    causal = col <= row
    tril = jnp.where(causal, 1.0, 0.0)
    for b in range(z_ref.shape[0]):
        xp_ref[b, hist:hist + L, :] = xbc_ref[b]
        conv = cb_ref[...]
        for k in range(C_CONV):
            conv = conv + xp_ref[b, pl.ds(hist - (C_CONV - 1) + k, L), :] * cw_ref[k:k + 1, :]
        xp_ref[b, 0:hist, :] = xp_ref[b, L:L + hist, :]
        xc = _silu(conv)
        xs = xc[:, :C_WIDTH]
        bm = xc[:, C_WIDTH:C_WIDTH + C_GROUPS * C_STATE]
        cm = xc[:, C_WIDTH + C_GROUPS * C_STATE:]

        dt = jax.nn.softplus(dt_ref[b] + hp_ref[0:1, :])
        acum = _dot_exact(tril, dt * a_row)
        acum_t = acum.T
        dt_t = dt.T
        cb = [_dot_nt(cm[:, g * C_STATE:(g + 1) * C_STATE], bm[:, g * C_STATE:(g + 1) * C_STATE])
              for g in range(C_GROUPS)]
        ys = []
        for h in range(C_HEADS):
            g = h // (C_HEADS // C_GROUPS)
            ac_col = acum[:, h:h + 1]
            seg = ac_col - acum_t[h:h + 1, :]
            decay = jnp.where(causal, jnp.exp(jnp.where(causal, seg, 0.0)), 0.0)
            scores = cb[g] * decay * dt_t[h:h + 1, :]
            x_h = xs[:, h * C_HEAD_DIM:(h + 1) * C_HEAD_DIM]
            b_g = bm[:, g * C_STATE:(g + 1) * C_STATE]
            c_g = cm[:, g * C_STATE:(g + 1) * C_STATE]
            hs = hs_ref[b, h]
            ys.append(_dot(scores, x_h) + _dot_nt(c_g, hs) * jnp.exp(ac_col))
            ac_last = acum[L - 1:L, h:h + 1]
            w_end = jnp.exp(ac_last - ac_col) * dt[:, h:h + 1]
            hs_ref[b, h] = hs * jnp.exp(ac_last) + _dot_tn(x_h * w_end, b_g)
        y = jnp.concatenate(ys, axis=-1) + dsk_ref[...] * xs
        o_ref[b] = _rms(y * _silu(z_ref[b]), ng_ref[...])

    @pl.when(t == pl.num_programs(0) - 1)
    def _():
        hout_ref[...] = hs_ref[...]


def _mamba_prompt(z, xbc, dt, conv_w, conv_b_row, hp, dsk_row, ng_row, l):
    nb, t = z.shape[:2]
    L = SSD_CHUNK
    return pl.pallas_call(
        _mamba_kernel,
        grid=(t // L,),
        in_specs=[pl.BlockSpec((nb, L, C_WIDTH), lambda c: (0, c, 0)),
                  pl.BlockSpec((nb, L, C_CONV_DIM), lambda c: (0, c, 0)),
                  pl.BlockSpec((nb, L, LANES), lambda c: (0, c, 0)),
                  pl.BlockSpec((None, C_CONV, C_CONV_DIM), lambda c: (l, 0, 0)),
                  pl.BlockSpec((1, C_CONV_DIM), lambda c: (0, 0)),
                  pl.BlockSpec((SUBLANES, LANES), lambda c: (0, 0)),
                  pl.BlockSpec((1, C_WIDTH), lambda c: (0, 0)),
                  pl.BlockSpec((1, C_WIDTH), lambda c: (0, 0))],
        out_specs=[pl.BlockSpec((nb, L, C_WIDTH), lambda c: (0, c, 0)),
                   pl.BlockSpec((nb, C_HEADS, C_HEAD_DIM, C_STATE), lambda c: (0, 0, 0, 0))],
        out_shape=[jax.ShapeDtypeStruct((nb, t, C_WIDTH), _F32),
                   jax.ShapeDtypeStruct((nb, C_HEADS, C_HEAD_DIM, C_STATE), _F32)],
        scratch_shapes=[pltpu.VMEM((nb, L + 2 * SUBLANES, C_CONV_DIM), _F32),
                        pltpu.VMEM((nb, C_HEADS, C_HEAD_DIM, C_STATE), _F32)],
        compiler_params=_cparams(("arbitrary",)),
        name="mamba_prompt",
    )(z, xbc, dt, conv_w, conv_b_row, hp, dsk_row, ng_row)


def _split_dot(x, p):
    hi = x.astype(_MXU)
    lo = (x - hi.astype(_F32)).astype(_MXU)
    return jnp.dot(hi, p, preferred_element_type=_F32) + jnp.dot(lo, p, preferred_element_type=_F32)


def _pool_matrix(n_tok, per, n_out, first=0):
    r = lax.broadcasted_iota(jnp.int32, (n_tok, n_out), 0)
    c = lax.broadcasted_iota(jnp.int32, (n_tok, n_out), 1)
    return jnp.where(c == first + r // per, 1.0, 0.0).astype(_MXU)


def _pe_sums_t(pet_ref):
    pk = jnp.sum(pet_ref[0], axis=-1, keepdims=True)
    pv = jnp.sum(pet_ref[1], axis=-1, keepdims=True)
    return jnp.concatenate([pk, pk, pv, pv], axis=0)


def _compress_cols(m, wt_ref):
    hd = B_HEAD_DIM
    return jnp.concatenate([_dot(wt_ref[0], m[0:hd]), _dot(wt_ref[0], m[hd:2 * hd]),
                            _dot(wt_ref[1], m[2 * hd:3 * hd]), _dot(wt_ref[1], m[3 * hd:])], axis=0)


def _compress_kernel(kv_ref, pet_ref, wt_ref, o_ref, *, chunk):
    t = kv_ref.shape[-1]
    pool = _pool_matrix(chunk, CMP_BLOCK, chunk // CMP_BLOCK)
    sums = jnp.concatenate([_split_dot(kv_ref[0, :, c * chunk:(c + 1) * chunk], pool) for c in range(t // chunk)],
                           axis=-1)
    o_ref[0] = _compress_cols((sums + _pe_sums_t(pet_ref)) * (1.0 / CMP_BLOCK), wt_ref)


def _compress_prompt(kvc_t, pe_t, w_t, l, chunk=2048):
    nb, _, t = kvc_t.shape
    return pl.pallas_call(
        functools.partial(_compress_kernel, chunk=min(chunk, t)),
        grid=(nb,),
        in_specs=[pl.BlockSpec((1, _KV_ROWS, t), lambda b: (b, 0, 0)),
                  pl.BlockSpec((None, 2, B_HEAD_DIM, CMP_BLOCK), lambda b: (l, 0, 0, 0)),
                  pl.BlockSpec((None, 2, B_HEAD_DIM, B_HEAD_DIM), lambda b: (l, 0, 0, 0))],
        out_specs=pl.BlockSpec((1, _KV_ROWS, t // CMP_BLOCK), lambda b: (b, 0, 0)),
        out_shape=jax.ShapeDtypeStruct((nb, _KV_ROWS, t // CMP_BLOCK), _F32),
        compiler_params=_cparams(("parallel",)),
        name="nsa_compress_prompt",
    )(kvc_t, pe_t, w_t)


_KEY_CHUNK = 1024
_FRONT_PAD = Q_BLOCK
_NEAR = 2 * Q_BLOCK
_MAIN_UNROLL = 1
_Q_PER_STEP = 1
_V_ROWS = B_HEAD_DIM + 16


def _nsa_tables(n_sel):
    r = np.arange(Q_BLOCK)[:, None]
    cmp_idx = np.full((2, Q_BLOCK, n_sel), REL_BUCKETS - 1, np.int32)
    for par in range(2):
        for u in (-2, -1, 0, 1):
            dist = r[:, 0] - (CMP_BLOCK - 1) - CMP_BLOCK * (2 * u + par)
            cmp_idx[par, :, u % n_sel] = _bucket_np(dist)
    c = np.arange(_NEAR)[None, :]
    dist = Q_BLOCK + r - c
    near_idx = np.where(dist >= 0, _bucket_np(dist), -1).astype(np.int32)
    c = np.arange(WINDOW + Q_BLOCK)[None, :]
    dist = r + WINDOW - c
    win_idx = np.where((dist >= 0) & (dist < WINDOW), _bucket_np(dist), -1).astype(np.int32)
    return cmp_idx, near_idx, win_idx


def _fill_bias(idx, rb_ref, hg, rel_to_last):
    base = rb_ref[(REL_BUCKETS - 1) * B_HEADS + hg] if rel_to_last else 0.0
    tile = jnp.where(idx < 0, NEG_INF, 0.0)
    for b in range(REL_BUCKETS):
        tile = jnp.where(idx == b, rb_ref[b * B_HEADS + hg] - base, tile)
    return tile


def _topk_mask(score, k, taken):
    n = float(score.shape[-1])
    lane = lax.broadcasted_iota(jnp.int32, score.shape, 1).astype(_F32)
    s = jnp.where(taken, -jnp.inf, score)
    for _ in range(k):
        m = jnp.max(s, axis=-1, keepdims=True)
        first = jnp.min(jnp.where(s == m, lane, n), axis=-1, keepdims=True)
        s = jnp.where(lane == first, -jnp.inf, s)
    return s == -jnp.inf


def _masked_softmax(l):
    m = jnp.maximum(jnp.max(l, axis=-1, keepdims=True), SOFTMAX_FLOOR)
    e = jnp.exp(l - m)
    return e / jnp.maximum(jnp.sum(e, axis=-1, keepdims=True), 1e-20)


def _online_step(s, vt, carry):
    m_run, acc = carry
    m_new = jnp.maximum(m_run, jnp.max(s, axis=-1, keepdims=True))
    return m_new, jnp.exp(m_run - m_new) * acc + _dot_nt(jnp.exp(s - m_new), vt)


def _key_tiles(ref, first, n, rows=slice(None)):
    return jnp.concatenate([ref[0, first + j, rows, :] for j in range(n)], axis=-1)


def _nsa_kernel(q_ref, gt_ref, kc_ref, vc_ref, ksa_ref, vs_ref, kwa_ref, vwa_ref,
                cidx_ref, nidx_ref, widx_ref, rb_ref, o_ref, bc_ref, bn_ref, bw_ref, *, n_sel):
    step = pl.program_id(1)
    hd, G = B_HEAD_DIM, B_GROUP
    QB = Q_BLOCK

    @pl.when((pl.program_id(0) == 0) & (step == 0))
    def _():
        for hg in range(B_HEADS):
            for par in range(2):
                bc_ref[hg, par] = _fill_bias(cidx_ref[par], rb_ref, hg, False)
            bn_ref[hg] = _fill_bias(nidx_ref[...], rb_ref, hg, True)
            bw_ref[hg] = _fill_bias(widx_ref[...], rb_ref, hg, False)

    heads = range(B_KV_HEADS)
    hrows = [slice(h * hd, (h + 1) * hd) for h in heads]
    vrows = [slice(h * _V_ROWS, (h + 1) * _V_ROWS) for h in heads]
    r_col = lax.broadcasted_iota(jnp.int32, (QB, 1), 0)
    n_lane = lax.broadcasted_iota(jnp.int32, (QB, n_sel), 1)
    c_near = lax.broadcasted_iota(jnp.int32, (QB, _NEAR), 1)
    c_win = lax.broadcasted_iota(jnp.int32, (QB, WINDOW + QB), 1)
    zeros_h = jnp.zeros((G * QB, hd), _F32)
    n_win = (WINDOW + QB) // LANES
    per_chunk = _KEY_CHUNK // LANES

    def front(i, q, gate_logits):
        q = q * (hd ** -0.5)
        qpos = i * QB + r_col
        cur = qpos // SEL_BLOCK
        vis = [CMP_BLOCK * (2 * n_lane + par) + (CMP_BLOCK - 1) <= qpos for par in range(2)]
        vis3 = jnp.concatenate([jnp.concatenate(vis, axis=-1)] * G, axis=0)
        win_ok = jnp.concatenate([c_win >= WINDOW - i * QB] * G, axis=0)
        q3 = [jnp.concatenate([q[:, (h * G + g) * hd:(h * G + g + 1) * hd] for g in range(G)], axis=0)
              for h in heads]
        q3h = [jnp.concatenate([q3[h], zeros_h] if h == 0 else [zeros_h, q3[h]], axis=-1) for h in heads]
        t0 = i * (QB // LANES)

        forced = (n_lane == cur) | (n_lane == 0)
        started = n_lane <= cur
        o_c, imp = [], []
        for h in heads:
            lc = _dot(q3[h], kc_ref[0, hrows[h], :])
            bias_c = jnp.concatenate(
                [jnp.concatenate([pltpu.roll(bc_ref[h * G + g, par], 2 * i, 1) for par in range(2)], axis=-1)
                 for g in range(G)], axis=0)
            p_c = _masked_softmax(jnp.where(vis3, lc + bias_c, NEG_INF))
            o_c.append(_dot_nt(p_c, vc_ref[0, hrows[h], :]))
            imp_h = sum(p_c[g * QB:(g + 1) * QB, :n_sel] + p_c[g * QB:(g + 1) * QB, n_sel:] for g in range(G))
            imp.append(jnp.where(started, imp_h, -1.0))

        chosen = _topk_mask(jnp.concatenate(imp, axis=0), min(SEL_TOPK, n_sel) - 2,
                            jnp.concatenate([forced] * B_KV_HEADS, axis=0))
        kw = _key_tiles(kwa_ref, t0, n_win)
        o_w = []
        for h in heads:
            bias_w = jnp.concatenate([bw_ref[h * G + g] for g in range(G)], axis=0)
            p_w = _masked_softmax(jnp.where(win_ok, _dot(q3h[h], kw) + bias_w, NEG_INF))
            o_w.append(_dot_nt(p_w, _key_tiles(vwa_ref, t0, n_win, hrows[h])))

        qa_main, qa_near = [], []
        for h in heads:
            allowed = chosen[h * QB:(h + 1) * QB] & started
            m_main = jnp.where(allowed & (n_lane < 2 * i - 2), 0.0, NEG_INF)
            m_near = jnp.where(allowed, 0.0, NEG_INF)
            qa_main.append(jnp.concatenate([q3h[h], jnp.concatenate([m_main] * G, axis=0)], axis=-1).astype(_MXU))
            qa_near.append(jnp.concatenate([q3h[h], jnp.concatenate([m_near] * G, axis=0)], axis=-1).astype(_MXU))
        return dict(i=i, gate=jax.nn.sigmoid(gate_logits), o_c=o_c, o_w=o_w, qa_main=qa_main, qa_near=qa_near)

    def back(f, carry):
        i = f["i"]
        t0 = i * (QB // LANES)
        near_ok = jnp.concatenate([c_near >= _FRONT_PAD - i * QB] * G, axis=0)
        ks = _key_tiles(ksa_ref, t0, _NEAR // LANES)
        outs = []
        for h in heads:
            corr = jnp.concatenate([bn_ref[h * G + g] for g in range(G)], axis=0)
            s = jnp.where(near_ok, _dot(f["qa_near"][h], ks) + corr, NEG_INF)
            _, acc_s = _online_step(s, _key_tiles(vs_ref, t0, _NEAR // LANES, vrows[h]), carry[h])
            o_s = acc_s[:, :hd] / acc_s[:, hd:hd + 1]
            for g in range(G):
                k0 = (h * G + g) * N_BRANCH
                rows = slice(g * QB, (g + 1) * QB)
                gate = f["gate"]
                outs.append(gate[:, k0:k0 + 1] * f["o_c"][h][rows] + gate[:, k0 + 1:k0 + 2] * o_s[rows]
                            + gate[:, k0 + 2:k0 + 3] * f["o_w"][h][rows])
        return jnp.concatenate(outs, axis=-1)

    blocks = [front(step * _Q_PER_STEP + u, q_ref[0, u * QB:(u + 1) * QB, :], gt_ref[0, u * QB:(u + 1) * QB, :])
              for u in range(_Q_PER_STEP)]
    n_chain = _Q_PER_STEP * B_KV_HEADS
    qa_main_all = jnp.concatenate([f["qa_main"][h] for f in blocks for h in heads], axis=0)
    i_last = step * _Q_PER_STEP + _Q_PER_STEP - 1
    n_main = (jnp.maximum(i_last - 1, 0) * QB + _KEY_CHUNK - 1) // _KEY_CHUNK

    def main_body(c, carry):
        for u in range(_MAIN_UNROLL):
            tc = _FRONT_PAD // LANES + (c * _MAIN_UNROLL + u) * per_chunk
            s = _dot(qa_main_all, _key_tiles(ksa_ref, tc, per_chunk))
            carry = tuple(_online_step(s[k * G * QB:(k + 1) * G * QB],
                                       _key_tiles(vs_ref, tc, per_chunk, vrows[k % B_KV_HEADS]), carry[k])
                          for k in range(n_chain))
        return carry

    init = (jnp.full((G * QB, 1), SOFTMAX_FLOOR, _F32), jnp.zeros((G * QB, _V_ROWS), _F32))
    carry = lax.fori_loop(0, (n_main + _MAIN_UNROLL - 1) // _MAIN_UNROLL, main_body, (init,) * n_chain)
    for u, f in enumerate(blocks):
        o_ref[0, u * QB:(u + 1) * QB, :] = back(f, carry[u * B_KV_HEADS:(u + 1) * B_KV_HEADS])


def _nsa_prompt(q, gate, kcp, vcp, ksa, vs, kwa, vwa, rb_flat):
    nb, t = q.shape[:2]
    n_sel = t // SEL_BLOCK
    cidx, nidx, widx = _nsa_tables(n_sel)
    tp, tw = ksa.shape[1], kwa.shape[1]
    q_rows = _Q_PER_STEP * Q_BLOCK
    assert t % q_rows == 0
    full = lambda shape: pl.BlockSpec(shape, lambda b, i: (0,) * len(shape))
    return pl.pallas_call(
        functools.partial(_nsa_kernel, n_sel=n_sel),
        grid=(nb, t // q_rows),
        in_specs=[pl.BlockSpec((1, q_rows, B_WIDTH), lambda b, i: (b, i, 0)),
                  pl.BlockSpec((1, q_rows, LANES), lambda b, i: (b, i, 0)),
                  pl.BlockSpec((1, B_KV_WIDTH, 2 * n_sel), lambda b, i: (b, 0, 0)),
                  pl.BlockSpec((1, B_KV_WIDTH, 2 * n_sel), lambda b, i: (b, 0, 0)),
                  pl.BlockSpec((1, tp, B_KV_WIDTH + n_sel, LANES), lambda b, i: (b, 0, 0, 0)),
                  pl.BlockSpec((1, tp, B_KV_HEADS * _V_ROWS, LANES), lambda b, i: (b, 0, 0, 0)),
                  pl.BlockSpec((1, tw, B_KV_WIDTH, LANES), lambda b, i: (b, 0, 0, 0)),
                  pl.BlockSpec((1, tw, B_KV_WIDTH, LANES), lambda b, i: (b, 0, 0, 0)),
                  full(cidx.shape), full(nidx.shape), full(widx.shape),
                  pl.BlockSpec(memory_space=pltpu.SMEM)],
        out_specs=pl.BlockSpec((1, q_rows, B_WIDTH), lambda b, i: (b, i, 0)),
        out_shape=jax.ShapeDtypeStruct((nb, t, B_WIDTH), _F32),
        scratch_shapes=[pltpu.VMEM((B_HEADS, 2, Q_BLOCK, n_sel), _F32),
                        pltpu.VMEM((B_HEADS, Q_BLOCK, _NEAR), _F32),
                        pltpu.VMEM((B_HEADS, Q_BLOCK, WINDOW + Q_BLOCK), _F32)],
        compiler_params=_cparams(("arbitrary", "arbitrary")),
        name="nsa_prompt",
    )(q, gate, kcp, vcp, ksa, vs, kwa, vwa, jnp.asarray(cidx), jnp.asarray(nidx), jnp.asarray(widx), rb_flat)


def _nsa_prompt_inputs(kvcmp_t, kvs_t, kvw_t):
    nb, _, t = kvs_t.shape
    n_sel = t // SEL_BLOCK
    kvp = kvcmp_t.reshape(nb, _KV_ROWS, n_sel, 2).transpose(0, 1, 3, 2).reshape(nb, _KV_ROWS, 2 * n_sel)
    kcp = kvp[:, :B_KV_WIDTH].astype(_MXU)
    vcp = kvp[:, B_KV_WIDTH:].astype(_MXU)
    blk = (jnp.arange(n_sel)[:, None] == jnp.arange(t)[None, :] // SEL_BLOCK).astype(_MXU)
    ksa = jnp.concatenate([kvs_t[:, :B_KV_WIDTH].astype(_MXU), jnp.broadcast_to(blk, (nb, n_sel, t))], axis=1)
    pad_s = ((0, 0), (0, 0), (_FRONT_PAD, _MAIN_UNROLL * _KEY_CHUNK - _FRONT_PAD))
    pad_w = ((0, 0), (0, 0), (WINDOW, 0))
    def tiles(a, pad):
        a = jnp.pad(a, pad)
        return a.reshape(nb, a.shape[1], a.shape[2] // LANES, LANES).transpose(0, 2, 1, 3)

    ones = jnp.pad(jnp.ones((nb, 1, t), _MXU), ((0, 0), (0, _V_ROWS - B_HEAD_DIM - 1), (0, 0)))
    v_t = kvs_t[:, B_KV_WIDTH:].astype(_MXU)
    vs = tiles(jnp.concatenate([a for h in range(B_KV_HEADS)
                                for a in (v_t[:, h * B_HEAD_DIM:(h + 1) * B_HEAD_DIM], ones)], axis=1), pad_s)
    kwa = tiles(kvw_t[:, :B_KV_WIDTH].astype(_MXU), pad_w)
    vwa = tiles(kvw_t[:, B_KV_WIDTH:].astype(_MXU), pad_w)
    return kcp, vcp, tiles(ksa, pad_s), vs, kwa, vwa


def _bias_rows(dist, rbt):
    n = jnp.maximum(dist, 0)
    nf = jnp.maximum(n, 1).astype(_F32)
    large = REL_MAX_EXACT + (jnp.log(nf / REL_MAX_EXACT) / math.log(REL_MAX_DIST / REL_MAX_EXACT)
                             * (REL_BUCKETS - REL_MAX_EXACT)).astype(jnp.int32)
    bucket = jnp.where(n < REL_MAX_EXACT, n, jnp.minimum(large, REL_BUCKETS - 1))
    out = jnp.zeros((SUBLANES, dist.shape[-1]), _F32)
    for b in range(REL_BUCKETS):
        out = jnp.where(bucket == b, rbt[:, b:b + 1], out)
    return out


def _scmp_kernel(pt_ref, *refs, n_pages, group):
    pages = refs[:group]
    q_ref, pet_ref, wt_ref, rbt_ref, oc_ref, imp_ref, kvm_ref = refs[group:]
    pg = pl.program_id(1)
    n_cmp = n_pages * (PAGE_SIZE // CMP_BLOCK)
    per = PAGE_SIZE // CMP_BLOCK * group

    x = jnp.concatenate([p[...] for p in pages], axis=-1)
    kvm_ref[pg] = _split_dot(x, _pool_matrix(group * PAGE_SIZE, CMP_BLOCK, per))

    @pl.when(pg == pl.num_programs(1) - 1)
    def _():
        hd = B_HEAD_DIM
        n_sel = n_cmp // 2
        past = n_pages * PAGE_SIZE
        sums = jnp.concatenate([kvm_ref[s] for s in range(n_pages // group)], axis=-1)
        kv = _compress_cols((sums + _pe_sums_t(pet_ref)) * (1.0 / CMP_BLOCK), wt_ref)
        q8 = q_ref[0] * (hd ** -0.5)
        row = lax.broadcasted_iota(jnp.int32, (SUBLANES, 1), 0)
        head0 = row < B_GROUP
        lc = jnp.where(head0, _dot(q8, kv[0:hd]), _dot(q8, kv[hd:2 * hd]))
        blk = lax.broadcasted_iota(jnp.int32, (1, n_cmp), 1)
        dist = past - (blk * CMP_BLOCK + CMP_BLOCK - 1)
        p = _masked_softmax(jnp.where(dist >= 0, lc + _bias_rows(dist, rbt_ref[...]), NEG_INF))
        oc_ref[0] = jnp.where(head0, _dot_nt(p, kv[2 * hd:3 * hd]), _dot_nt(p, kv[3 * hd:]))
        pool = (lax.broadcasted_iota(jnp.int32, (n_cmp, n_sel), 0) // (SEL_BLOCK // CMP_BLOCK)
                == lax.broadcasted_iota(jnp.int32, (n_cmp, n_sel), 1))
        pp = _dot_exact(p, jnp.where(pool, 1.0, 0.0))
        imp0 = jnp.sum(pp[0:B_GROUP], axis=0, keepdims=True)
        imp1 = jnp.sum(pp[B_GROUP:2 * B_GROUP], axis=0, keepdims=True)
        imp_ref[0] = jnp.where(row == 0, imp0, jnp.where(row == 1, imp1, 0.0))


def _sample_cmp(page_table, cache_t, q8, pe_t, w_t, rbt, l, group=32):
    ns, n_pages = page_table.shape
    group = min(group, n_pages)
    n_cmp = n_pages * (PAGE_SIZE // CMP_BLOCK)
    page_spec = lambda k: pl.BlockSpec((None, None, _KV_ROWS, PAGE_SIZE),
                                       lambda b, g, pt: (l, pt[b, g * group + k], 0, 0))
    return pl.pallas_call(
        functools.partial(_scmp_kernel, n_pages=n_pages, group=group),
        grid_spec=pltpu.PrefetchScalarGridSpec(
            num_scalar_prefetch=1,
            grid=(ns, n_pages // group),
            in_specs=[page_spec(k) for k in range(group)] + [
                pl.BlockSpec((1, SUBLANES, B_HEAD_DIM), lambda b, g, pt: (b, 0, 0)),
                pl.BlockSpec((None, 2, B_HEAD_DIM, CMP_BLOCK), lambda b, g, pt: (l, 0, 0, 0)),
                pl.BlockSpec((None, 2, B_HEAD_DIM, B_HEAD_DIM), lambda b, g, pt: (l, 0, 0, 0)),
                pl.BlockSpec((SUBLANES, REL_BUCKETS), lambda b, g, pt: (0, 0))],
            out_specs=[pl.BlockSpec((1, SUBLANES, B_HEAD_DIM), lambda b, g, pt: (b, 0, 0)),
                       pl.BlockSpec((1, SUBLANES, n_cmp // 2), lambda b, g, pt: (b, 0, 0))],
            scratch_shapes=[pltpu.VMEM((n_pages // group, _KV_ROWS, n_cmp * group // n_pages), _F32)]),
        out_shape=[jax.ShapeDtypeStruct((ns, SUBLANES, B_HEAD_DIM), _F32),
                   jax.ShapeDtypeStruct((ns, SUBLANES, n_cmp // 2), _F32)],
        compiler_params=_cparams(("arbitrary", "arbitrary")),
        name="nsa_sample_cmp",
    )(page_table, *([cache_t] * group), q8, pe_t, w_t, rbt)


def _stopk_kernel(imp_ref, idx_ref, *, k):
    s = imp_ref[...]
    n = s.shape[-1]
    lane = lax.broadcasted_iota(jnp.int32, s.shape, 1)
    s = jnp.where(lane == 0, FORCE_SCORE, s)

    def body(it, carry):
        s, out = carry
        m = jnp.max(s, axis=-1, keepdims=True)
        first = jnp.min(jnp.where(s == m, lane, n), axis=-1, keepdims=True)
        return jnp.where(lane == first, -jnp.inf, s), jnp.where(lane == it, first, out)

    _, out = lax.fori_loop(0, k, body, (s, jnp.zeros(s.shape, jnp.int32)))
    idx_ref[...] = out


def _sample_topk(imp2d, k):
    return pl.pallas_call(
        functools.partial(_stopk_kernel, k=k),
        out_shape=jax.ShapeDtypeStruct(imp2d.shape, jnp.int32),
        name="nsa_sample_topk",
    )(imp2d)


def _sattn_kernel(pt_ref, idx_ref, *refs, n_blk, past):
    blocks = refs[:2 * n_blk]
    q_ref, gt_ref, new_ref, oc_ref, win_ref, rbt_ref, o_ref = refs[2 * n_blk:]
    b = pl.program_id(0)
    hd = B_HEAD_DIM
    q8 = q_ref[0] * (hd ** -0.5)
    rbt = rbt_ref[...]
    row = lax.broadcasted_iota(jnp.int32, (SUBLANES, 1), 0)
    head0 = row < B_GROUP
    new = new_ref[0]
    bias0 = _bias_rows(jnp.zeros((1, 1), jnp.int32), rbt)

    def attend(s, v_of_head, k_new, v_new):
        s_new = jnp.sum(q8 * k_new, axis=-1, keepdims=True) + bias0
        m = jnp.maximum(jnp.max(s, axis=-1, keepdims=True), s_new)
        p = jnp.exp(s - m)
        p_new = jnp.exp(s_new - m)
        den = jnp.sum(p, axis=-1, keepdims=True) + p_new
        num = jnp.where(head0, _dot_nt(p, v_of_head(0)), _dot_nt(p, v_of_head(1))) + p_new * v_new
        return num / den

    def per_head(a0, a1):
        return jnp.where(head0, a0, a1)

    per_page = PAGE_SIZE // SEL_BLOCK
    t_in = lax.broadcasted_iota(jnp.int32, (1, PAGE_SIZE), 1)
    s_h, vs = [], []
    for h in range(B_KV_HEADS):
        kt = jnp.concatenate([blocks[h * n_blk + k][h * hd:(h + 1) * hd, :] for k in range(n_blk)], axis=-1)
        vs.append(jnp.concatenate(
            [blocks[h * n_blk + k][B_KV_WIDTH + h * hd:B_KV_WIDTH + (h + 1) * hd, :] for k in range(n_blk)], axis=-1))
        dist = []
        for k in range(n_blk):
            blk = idx_ref[b, h, k]
            in_blk = t_in // SEL_BLOCK == blk % per_page
            dist.append(jnp.where(in_blk, past - ((blk // per_page) * PAGE_SIZE + t_in), -1))
        dist = jnp.concatenate(dist, axis=-1)
        s_h.append(jnp.where(dist >= 0, _dot(q8, kt) + _bias_rows(dist, rbt), NEG_INF))
    ksn = new[:, 2 * B_KV_WIDTH:3 * B_KV_WIDTH]
    vsn = new[:, 3 * B_KV_WIDTH:4 * B_KV_WIDTH]
    o_s = attend(per_head(s_h[0], s_h[1]), lambda h: vs[h],
                 per_head(ksn[:, :hd], ksn[:, hd:]), per_head(vsn[:, :hd], vsn[:, hd:]))

    win = win_ref[0]
    wb = win.shape[1]
    dist = wb - lax.broadcasted_iota(jnp.int32, (1, wb), 1)
    okw = (dist < WINDOW) & (past - dist >= 0)
    lw = per_head(_dot(q8, win[0:hd]), _dot(q8, win[hd:2 * hd]))
    sw = jnp.where(okw, lw + _bias_rows(dist, rbt), NEG_INF)
    kwn = new[:, 4 * B_KV_WIDTH:5 * B_KV_WIDTH]
    vwn = new[:, 5 * B_KV_WIDTH:6 * B_KV_WIDTH]
    o_w = attend(sw, lambda h: win[B_KV_WIDTH + h * hd:B_KV_WIDTH + (h + 1) * hd],
                 per_head(kwn[:, :hd], kwn[:, hd:]), per_head(vwn[:, :hd], vwn[:, hd:]))

    gate = jax.nn.sigmoid(gt_ref[0])
    o_ref[0] = gate[:, 0:1] * oc_ref[0] + gate[:, 1:2] * o_s + gate[:, 2:3] * o_w


def _sample_attn(page_table, idx, cache_t, q8, gate8, kv_new, oc, win_t, rbt, l):
    ns, n_pages = page_table.shape
    n_blk = idx.shape[-1]
    past = n_pages * PAGE_SIZE
    per_page = PAGE_SIZE // SEL_BLOCK

    pages = jnp.take_along_axis(page_table[:, None, :], idx // per_page, axis=-1)

    def blk_spec(h, k):
        return pl.BlockSpec((None, None, _KV_ROWS, PAGE_SIZE), lambda b, pg, ix: (l, pg[b, h, k], 0, 0))

    wb = win_t.shape[3]
    return pl.pallas_call(
        functools.partial(_sattn_kernel, n_blk=n_blk, past=past),
        grid_spec=pltpu.PrefetchScalarGridSpec(
            num_scalar_prefetch=2,
            grid=(ns,),
            in_specs=[blk_spec(h, k) for h in range(B_KV_HEADS) for k in range(n_blk)] + [
                pl.BlockSpec((1, SUBLANES, B_HEAD_DIM), lambda b, pt, ix: (b, 0, 0)),
                pl.BlockSpec((1, SUBLANES, LANES), lambda b, pt, ix: (b, 0, 0)),
                pl.BlockSpec((1, 1, 6 * B_KV_WIDTH), lambda b, pt, ix: (b, 0, 0)),
                pl.BlockSpec((1, SUBLANES, B_HEAD_DIM), lambda b, pt, ix: (b, 0, 0)),
                pl.BlockSpec((None, 1, _KV_ROWS, wb), lambda b, pt, ix: (l, b, 0, 0)),
                pl.BlockSpec((SUBLANES, REL_BUCKETS), lambda b, pt, ix: (0, 0))],
            out_specs=pl.BlockSpec((1, SUBLANES, B_HEAD_DIM), lambda b, pt, ix: (b, 0, 0))),
        out_shape=jax.ShapeDtypeStruct((ns, SUBLANES, B_HEAD_DIM), _F32),
        compiler_params=_cparams(("arbitrary",)),
        name="nsa_sample_attn",
    )(pages, idx, *([cache_t] * (B_KV_HEADS * n_blk)), q8, gate8, kv_new, oc, win_t, rbt)


def _smix_kernel(uv_ref, ng_ref, wd_ref, b0_ref, xbc_ref, st_ref, cw_ref, cb_ref, dt_ref, hp_ref,
                 dsk_ref, ex_ref, oa_ref, v_ref, xdt_ref, ea_ref, y1_ref, bc_ref):
    u, v = _gelu_ln(uv_ref[...], ng_ref[...])
    v_ref[...] = v
    oa_ref[...] = u * (v * wd_ref[...] + b0_ref[...])
    conv = cb_ref[...] + xbc_ref[...] * cw_ref[C_CONV - 1:C_CONV, :]
    for k in range(C_CONV - 1):
        conv = conv + st_ref[k] * cw_ref[k:k + 1, :]
    xc = _silu(conv)
    xs = xc[:, :C_WIDTH]
    bm = xc[:, C_WIDTH:C_WIDTH + C_GROUPS * C_STATE]
    cm = xc[:, C_WIDTH + C_GROUPS * C_STATE:]
    bc_ref[...] = xc[:, C_WIDTH:]
    dt = jax.nn.softplus(dt_ref[...] + hp_ref[0:1, :])
    acum = dt * (-jnp.exp(hp_ref[1:2, :]))
    dt_rep = _dot_exact(dt, ex_ref[...])
    ea_ref[...] = jnp.exp(_dot_exact(acum, ex_ref[...]))
    xdt = dt_rep * xs
    xdt_ref[...] = xdt
    per_g = C_WIDTH // C_GROUPS
    cb = [jnp.sum(cm[:, g * C_STATE:(g + 1) * C_STATE] * bm[:, g * C_STATE:(g + 1) * C_STATE],
                  axis=-1, keepdims=True) for g in range(C_GROUPS)]
    lane = lax.broadcasted_iota(jnp.int32, xs.shape, 1)
    y1_ref[...] = jnp.where(lane < per_g, cb[0], cb[1]) * xdt + dsk_ref[...] * xs


def _sample_mix(uv, ng_row, wd_row, b0_row, xbc, st, conv_w_l, conv_b_row, dt, hp, dsk_row, expand):
    ns = uv.shape[0]
    f = lambda w: jax.ShapeDtypeStruct((ns, w), _F32)
    return pl.pallas_call(
        _smix_kernel,
        out_shape=[f(A_WIDTH), f(A_WIDTH), f(C_WIDTH), f(C_WIDTH), f(C_WIDTH), f(2 * C_GROUPS * C_STATE)],
        name="sample_gmlp_conv",
    )(uv, ng_row, wd_row, b0_row, xbc, st, conv_w_l, conv_b_row, dt, hp, dsk_row, expand)


def _sssm_kernel(h0_ref, xdt_ref, ea_ref, y1_ref, bc_ref, z_ref, ng_ref, o_ref, hout_ref):
    per_g = C_WIDTH // C_GROUPS
    h0 = h0_ref[0]
    bc = bc_ref[0]
    bm = bc[:, :C_GROUPS * C_STATE]
    cm = bc[:, C_GROUPS * C_STATE:]
    c8 = [jnp.broadcast_to(cm[:, g * C_STATE:(g + 1) * C_STATE], (SUBLANES, C_STATE)) for g in range(C_GROUPS)]
    ch = jnp.concatenate([_dot_nt(c8[g], h0[g * per_g:(g + 1) * per_g, :])[0:1] for g in range(C_GROUPS)], axis=-1)
    y = y1_ref[0] + ch * ea_ref[0]
    o_ref[0] = _rms(y * _silu(z_ref[0]), ng_ref[...])
    row = lax.broadcasted_iota(jnp.int32, (LANES, C_WIDTH), 0)
    cols = jnp.where(row == 0, xdt_ref[0], jnp.where(row == 1, ea_ref[0], 0.0)).T
    rsel = lax.broadcasted_iota(jnp.int32, (C_WIDTH, C_STATE), 0) < per_g
    b_full = jnp.where(rsel, bm[:, :C_STATE], bm[:, C_STATE:])
    hout_ref[0] = h0 * cols[:, 1:2] + cols[:, 0:1] * b_full


def _sample_ssm(h0, xdt, ea, y1, bc, z, ng_row, l):
    ns = xdt.shape[0]
    r3 = lambda a: a.reshape(ns, 1, a.shape[-1])
    row_spec = lambda w: pl.BlockSpec((1, 1, w), lambda b: (b, 0, 0))
    out, hout = pl.pallas_call(
        _sssm_kernel,
        grid=(ns,),
        in_specs=[pl.BlockSpec((None, 1, C_WIDTH, C_STATE), lambda b: (l, b, 0, 0)),
                  row_spec(C_WIDTH), row_spec(C_WIDTH), row_spec(C_WIDTH), row_spec(2 * C_GROUPS * C_STATE),
                  row_spec(C_WIDTH), pl.BlockSpec((1, C_WIDTH), lambda b: (0, 0))],
        out_specs=[row_spec(C_WIDTH), pl.BlockSpec((1, C_WIDTH, C_STATE), lambda b: (b, 0, 0))],
        out_shape=[jax.ShapeDtypeStruct((ns, 1, C_WIDTH), _F32),
                   jax.ShapeDtypeStruct((ns, C_WIDTH, C_STATE), _F32)],
        compiler_params=_cparams(("parallel",)),
        name="sample_ssm",
    )(h0, r3(xdt), r3(ea), r3(y1), r3(bc), r3(z), ng_row)
    return out.reshape(ns, C_WIDTH), hout


def _pad_cols(a, width):
    return jnp.pad(a, ((0, 0),) * (a.ndim - 1) + ((0, width - a.shape[-1]),))


def _pack_w_in(w_in):
    o = np.cumsum((0, A_WIDTH, A_WIDTH, B_WIDTH, 6 * B_KV_WIDTH, _GATE_COLS, C_WIDTH, C_CONV_DIM, C_HEADS))
    u0, q0, kv0, gate0, z0, xbc0, dt0, end = o[0], o[2], o[3], o[4], o[5], o[6], o[7], o[8]
    parts = [w_in[..., u0:kv0], w_in[..., z0:xbc0], w_in[..., xbc0:dt0],
             _pad_cols(w_in[..., gate0:z0], LANES), _pad_cols(w_in[..., dt0:end], LANES)]
    w_rows = jnp.concatenate(parts, axis=-1).astype(_MXU)
    w_kv_t = jnp.swapaxes(w_in[..., kv0:gate0], -1, -2).astype(_MXU)
    return w_rows, w_kv_t


def _head_rows(dt_bias, a_log):
    hp = jnp.zeros((DEPTH, SUBLANES, LANES), _F32)
    hp = hp.at[:, 0, :C_HEADS].set(dt_bias)
    return hp.at[:, 1, :C_HEADS].set(a_log)


def kernel(x_prompt, x_sample, cache_cmp_kv, cache_slc_kv, page_table, state_win_kv, state_conv, state_ssm,
           norm_g, ffn_w_gate, ffn_w_up, ffn_w_down, w_in, w_out, gmlp_norm_g, gmlp_w_s, gmlp_b_s,
           nsa_pe_cmp, nsa_w_cmp, rel_bias, conv_w, conv_b, dt_bias, a_log, d_skip, ssm_norm_g):
    bp, t = x_prompt.shape[:2]
    ns = x_sample.shape[0]
    n_pages = page_table.shape[1]
    n_phys = cache_cmp_kv.shape[1]
    tm_p, tm_s = 512, ns
    assert t % _KEY_CHUNK == 0 and t % tm_p == 0

    wg, wu, wd = (w.astype(_MXU) for w in (ffn_w_gate, ffn_w_up, ffn_w_down))
    w_in_p, w_kv_t = _pack_w_in(w_in)
    w_out_b = w_out.astype(_MXU)
    pe_t = jnp.swapaxes(nsa_pe_cmp, -1, -2)
    wc_t = jnp.swapaxes(nsa_w_cmp, -1, -2)
    bs_t = jnp.swapaxes(gmlp_b_s, 1, 2)
    wdiag = jnp.repeat(gmlp_w_s[:, :, 0, 0], A_HEAD_DIM, axis=-1)
    b0 = jnp.repeat(gmlp_b_s[:, :, 0], A_HEAD_DIM, axis=-1)
    hp = _head_rows(dt_bias, a_log)
    dsk = jnp.repeat(d_skip, C_HEAD_DIM, axis=-1)
    expand = (jnp.arange(LANES)[:, None] == jnp.arange(C_WIDTH)[None, :] // C_HEAD_DIM).astype(_F32)
    rb_flat = rel_bias.reshape(-1)
    rbt = _pad_cols(rel_bias, SUBLANES).T
    def chan_major(a):
        lead, tok = a.shape[:-4], a.shape[-4]
        perm = tuple(range(len(lead))) + tuple(len(lead) + k for k in (1, 2, 3, 0))
        return a.transpose(perm).reshape(*lead, _KV_ROWS, tok)

    def token_major(a_t):
        lead, tok = a_t.shape[:-2], a_t.shape[-1]
        a5 = a_t.reshape(*lead, 2, B_KV_HEADS, B_HEAD_DIM, tok)
        perm = tuple(range(len(lead))) + tuple(len(lead) + k for k in (3, 0, 1, 2))
        return a5.transpose(perm)

    cache_c = chan_major(cache_cmp_kv)
    cache_s = chan_major(cache_slc_kv)
    win_state = chan_major(state_win_kv)
    ssm_state = state_ssm.reshape(DEPTH, ns, C_WIDTH, C_STATE)

    xp = x_prompt.reshape(bp * t, D_MODEL)
    xs = x_sample.reshape(ns, D_MODEL)
    outs = [[] for _ in range(11)]
    for l in range(DEPTH):
        g = norm_g[l]
        row = lambda a: a.reshape(1, -1)
        xp = _ffn(xp, g[0:2], wg, wu, wd, l, 0, tm_p)
        xs = _ffn(xs, g[0:2], wg, wu, wd, l, 0, tm_s)

        uv, q, z, xbc, gate, dt, kvc_t, kvs_t, kvw_t = _inproj(xp.reshape(bp, t, D_MODEL), row(g[2]), w_in_p,
                                                               w_kv_t, l, tm_p)
        oa = _gmlp_prompt(uv.reshape(bp * t, 2 * A_WIDTH), row(gmlp_norm_g[l]), gmlp_w_s, bs_t, l)
        kvcmp_t = _compress_prompt(kvc_t, pe_t, wc_t, l)
        ob = _nsa_prompt(q, gate, *_nsa_prompt_inputs(kvcmp_t, kvs_t, kvw_t), rb_flat)
        oc, h_p = _mamba_prompt(z, xbc, dt, conv_w, row(conv_b[l]), hp[l], row(dsk[l]), row(ssm_norm_g[l]), l)
        mix_p = (oa, ob.reshape(bp * t, B_WIDTH), oc.reshape(bp * t, C_WIDTH))
        wkeep = min(WINDOW, t)
        outs[0].append(token_major(kvc_t))
        outs[1].append(token_major(kvs_t))
        outs[2].append(token_major(kvw_t[:, :, t - wkeep:]))
        outs[3].append(xbc[:, t - (C_CONV - 1):])
        outs[4].append(h_p)

        uv, q, z, xbc, gate, dt, kvc_t, kvs_t, kvw_t = (
            a[0] for a in _inproj(xs.reshape(1, ns, D_MODEL), row(g[2]), w_in_p, w_kv_t, l, tm_s))
        q8 = jnp.pad(q.reshape(ns, B_HEADS, B_HEAD_DIM), ((0, 0), (0, SUBLANES - B_HEADS), (0, 0)))
        gate8 = jnp.pad(gate[:, :_GATE_COLS].reshape(ns, B_HEADS, N_BRANCH),
                        ((0, 0), (0, SUBLANES - B_HEADS), (0, LANES - N_BRANCH)))
        o_cmp, imp = _sample_cmp(page_table, cache_c, q8, pe_t, wc_t, rbt, l)
        n_sel_past = imp.shape[-1]
        idx = _sample_topk(imp.reshape(ns * SUBLANES, n_sel_past), SEL_TOPK - 1)
        idx = idx.reshape(ns, SUBLANES, n_sel_past)[:, :B_KV_HEADS, :SEL_TOPK - 1]
        kvc, kvs, kvwin = kvc_t.T, kvs_t.T, kvw_t.T
        kv_new = jnp.concatenate([kvc, kvs, kvwin], axis=-1).reshape(ns, 1, 6 * B_KV_WIDTH)
        ob8 = _sample_attn(page_table, idx, cache_s, q8, gate8, kv_new, o_cmp, win_state, rbt, l)
        ob = ob8[:, :B_HEADS].reshape(ns, B_WIDTH)
        st = jnp.swapaxes(state_conv[l], 0, 1)
        oa, v_rows, xdt, ea, y1, bc = _sample_mix(uv, row(gmlp_norm_g[l]), row(wdiag[l]), row(b0[l]), xbc, st,
                                                  conv_w[l], row(conv_b[l]), dt, hp[l], row(dsk[l]), expand)
        oc, h_s = _sample_ssm(ssm_state, xdt, ea, y1, bc, z, row(ssm_norm_g[l]), l)
        mix_s = (oa, ob, oc)
        outs[5].append(kvc.reshape(ns, 1, 2, B_KV_HEADS, B_HEAD_DIM))
        outs[6].append(kvs.reshape(ns, 1, 2, B_KV_HEADS, B_HEAD_DIM))
        outs[7].append(token_major(jnp.concatenate([win_state[l][:, :, 1:], kvwin[:, :, None]], axis=-1)))
        outs[8].append(jnp.concatenate([state_conv[l][:, 1:], xbc[:, None]], axis=1))
        outs[9].append(h_s.reshape(ns, C_HEADS, C_HEAD_DIM, C_STATE))
        outs[10].append(v_rows.reshape(ns, 1, A_WIDTH))

        xp = _mix_ffn(xp, *mix_p, row(g[3]), w_out_b, g[4:6], wg, wu, wd, l, 1, tm_p)
        xs = _mix_ffn(xs, *mix_s, row(g[3]), w_out_b, g[4:6], wg, wu, wd, l, 1, tm_s)
    stacked = [jnp.stack(o) for o in outs]
    return (xp.reshape(bp, t, D_MODEL), xs.reshape(ns, 1, D_MODEL), *stacked)
```
